```python
import math
import jax
import jax.numpy as jnp
from jax import lax
import numpy as np

D_MODEL = 1024
BATCH = 4
SEQ = 4096
DEPTH = 4
DEC_BATCH = 128
DEC_SEQ = 4
PAST_LEN = 2048
PAGE_SIZE = 128

N_A_LAYERS = DEPTH // 2
N_B_LAYERS = DEPTH - N_A_LAYERS
RET_HEADS = 6
RET_DK = 128
RET_DV = 128
RET_W = RET_HEADS * RET_DV
RET_CHUNK = 128
ROPE_BASE = 10000.0
FOX_HEADS = 12
FOX_HD = 64
FOX_W = FOX_HEADS * FOX_HD
FOX_QBLOCK = 128
FORGET_BIAS = 3.0
MEM_LEN = 256
MEM_HEADS = 4
MEM_HD = 64
MEM_W = MEM_HEADS * MEM_HD
MIX_W = RET_W + MEM_W
IN_A_W = 4 * RET_W + MEM_W
IN_B_W = FOX_W + MEM_W
KV_SHARED_W = 2 * FOX_W + FOX_HEADS
N_GROUPS = 4
EXPERTS_PER_GROUP = 8
TOP_K_IN_GROUP = 2
D_EXPERT = 256
DEEPNORM_ALPHA = (2 * DEPTH) ** 0.25
DEEPNORM_BETA = (8 * DEPTH) ** -0.25
LN_EPS = 1e-5
HEAD_NORM_EPS = 1e-6

kernel_name = 'yoco_retention_fox_hmoe_step'

F32 = jnp.float32


def layer_norm(x, g, b):
    xf = x.astype(F32)
    mu = jnp.mean(xf, axis=-1, keepdims=True)
    var = jnp.mean(jnp.square(xf - mu), axis=-1, keepdims=True)
    return ((xf - mu) * lax.rsqrt(var + LN_EPS) * g.astype(F32) + b.astype(F32)).astype(x.dtype)


def head_norm(r):
    mu = jnp.mean(r, axis=-1, keepdims=True)
    var = jnp.mean(jnp.square(r - mu), axis=-1, keepdims=True)
    return (r - mu) * lax.rsqrt(var + HEAD_NORM_EPS)


def rotary(x, pos):
    half = x.shape[-1] // 2
    inv_freq = ROPE_BASE ** (-jnp.arange(half, dtype=F32) / half)
    ang = pos.astype(F32)[:, None] * inv_freq[None, :]
    cos = jnp.cos(ang)[None, :, None, :]
    sin = jnp.sin(ang)[None, :, None, :]
    xf = x.astype(F32)
    x1, x2 = xf[..., :half], xf[..., half:]
    return jnp.concatenate([x1 * cos - x2 * sin, x1 * sin + x2 * cos], axis=-1).astype(x.dtype)


def retention(q, k, v, s0, log_gamma):
    B, L, H, DK = q.shape
    DV = v.shape[-1]
    C = math.gcd(L, RET_CHUNK)
    n = L // C
    qc = q.astype(F32).reshape(B, n, C, H, DK)
    kc = k.astype(F32).reshape(B, n, C, H, DK)
    vc = v.astype(F32).reshape(B, n, C, H, DV)
    i = jnp.arange(C, dtype=F32)
    diff = i[:, None] - i[None, :]
    intra_decay = jnp.where(diff >= 0, jnp.exp(log_gamma[:, None, None] * jnp.maximum(diff, 0.0)[None]), 0.0)
    scores = jnp.einsum('bnihd,bnjhd->bnhij', qc, kc) * intra_decay
    intra = jnp.einsum('bnhij,bnjhe->bnihe', scores, vc)
    k_decay = jnp.exp(log_gamma[None, :] * (C - 1 - i)[:, None])
    kv = jnp.einsum('bnjhd,bnjhe->nbhde', kc * k_decay[None, None, :, :, None], vc)
    chunk_decay = jnp.exp(log_gamma * C)[None, :, None, None]

    def step(s, kv_n):
        return chunk_decay * s + kv_n, s

    s_final, s_prev = lax.scan(step, s0.astype(F32), kv)
    q_decay = jnp.exp(log_gamma[None, :] * (i + 1.0)[:, None])
    cross = jnp.einsum('bnihd,nbhde->bnihe', qc * q_decay[None, None, :, :, None], s_prev)
    return (intra + cross).reshape(B, L, H, DV), s_final


def forgetting_attention(q, k, v, c_q, c_k, q_pos, k_pos):
    B, Lq, H, hd = q.shape
    QB = math.gcd(Lq, FOX_QBLOCK)
    nb = Lq // QB
    scale = hd ** -0.5
    qb = q.reshape(B, nb, QB, H, hd).swapaxes(0, 1)
    cqb = c_q.reshape(B, nb, QB, H).swapaxes(0, 1)
    pqb = q_pos.reshape(nb, QB)
    ck = jnp.transpose(c_k, (0, 2, 1))

    def block(args):
        qi, ci, pi = args
        logits = jnp.einsum('bqhd,bkhd->bhqk', qi, k).astype(F32) * scale
        logits = logits + jnp.transpose(ci, (0, 2, 1))[..., None] - ck[:, :, None, :]
        logits = jnp.where(k_pos[None, :] <= pi[:, None], logits, -jnp.inf)
        p = jax.nn.softmax(logits, axis=-1).astype(v.dtype)
        return jnp.einsum('bhqk,bkhd->bqhd', p, v)

    out = lax.map(block, (qb, cqb, pqb))
    return out.swapaxes(0, 1).reshape(B, Lq, H, hd)


def memory_attention(q, mk, mv):
    logits = jnp.einsum('blhd,bmhd->bhlm', q, mk).astype(F32) * (q.shape[-1] ** -0.5)
    p = jax.nn.softmax(logits, axis=-1).astype(mv.dtype)
    return jnp.einsum('bhlm,bmhd->blhd', p, mv)


def hierarchical_moe(x, w_group, b_group, w_route, b_route, w_up, w_down):
    B, L, D = x.shape
    t = x.reshape(B * L, D)
    g_logits = (t @ w_group).astype(F32) + b_group.astype(F32)
    g_idx = jnp.argmax(g_logits, axis=-1)
    g_w = jnp.take_along_axis(jax.nn.softmax(g_logits, axis=-1), g_idx[:, None], axis=-1)
    e_logits = jnp.einsum('td,gde->tge', t, w_route).astype(F32) + b_route.astype(F32)
    e_sel = jnp.take_along_axis(e_logits, g_idx[:, None, None], axis=1)[:, 0]
    top_v, top_i = lax.top_k(e_sel, TOP_K_IN_GROUP)
    top_w = jax.nn.softmax(top_v, axis=-1) * g_w
    e_w = jnp.sum(jax.nn.one_hot(top_i, EXPERTS_PER_GROUP, dtype=F32) * top_w[..., None], axis=1)
    comb = (jax.nn.one_hot(g_idx, N_GROUPS, dtype=F32)[:, :, None] * e_w[:, None, :]).astype(x.dtype)
    out = jnp.zeros_like(t)
    for g in range(N_GROUPS):
        up = jnp.einsum('td,edf->tef', t, w_up[g])
        gate, val = jnp.split(up, 2, axis=-1)
        hid = jax.nn.silu(gate) * val * comb[:, g, :, None]
        out = out + jnp.einsum('tef,efd->td', hid, w_down[g])
    return out.reshape(B, L, D)


def gather_pages(cache, page_table):
    g = cache[page_table]
    return g.reshape((page_table.shape[0], page_table.shape[1] * cache.shape[1]) + cache.shape[2:])


def trunk(x, ret_s0, mem_k, mem_v, past_k, past_v, past_logf,
          w_in_a, w_in_b, w_kv_shared, b_forget, w_o,
          ln1_g, ln1_b, ln2_g, ln2_b, w_group, b_group, w_route, b_route, w_up, w_down):
    B, L, _ = x.shape
    P = past_k.shape[1]
    pos = P + jnp.arange(L)
    log_gamma = jnp.log(1.0 - 2.0 ** (-5.0 - jnp.arange(RET_HEADS, dtype=F32)))
    new_ret = []
    for l in range(DEPTH):
        if l < N_A_LAYERS:
            h = x @ w_in_a[l]
            q, k, v, g, mq = jnp.split(h, [RET_W, 2 * RET_W, 3 * RET_W, 4 * RET_W], axis=-1)
            q = rotary(q.reshape(B, L, RET_HEADS, RET_DK), pos)
            k = rotary(k.reshape(B, L, RET_HEADS, RET_DK), pos) * (RET_DK ** -0.5)
            r, s_new = retention(q, k, v.reshape(B, L, RET_HEADS, RET_DV), ret_s0[l], log_gamma)
            new_ret.append(s_new.astype(x.dtype))
            tok = head_norm(r).astype(x.dtype).reshape(B, L, RET_W) * jax.nn.silu(g)
        else:
            if l == N_A_LAYERS:
                kvf = x @ w_kv_shared
                k_new, v_new, f_logit = jnp.split(kvf, [FOX_W, 2 * FOX_W], axis=-1)
                k_new = k_new.reshape(B, L, FOX_HEADS, FOX_HD)
                v_new = v_new.reshape(B, L, FOX_HEADS, FOX_HD)
                logf_new = jax.nn.log_sigmoid(f_logit.astype(F32) + b_forget.astype(F32))
                k_all = jnp.concatenate([past_k, k_new.astype(past_k.dtype)], axis=1)
                v_all = jnp.concatenate([past_v, v_new.astype(past_v.dtype)], axis=1)
                c_all = jnp.cumsum(jnp.concatenate([past_logf.astype(F32), logf_new], axis=1), axis=1)
                c_q = c_all[:, P:]
                k_pos = jnp.arange(P + L)
            h = x @ w_in_b[l - N_A_LAYERS]
            fq, mq = jnp.split(h, [FOX_W], axis=-1)
            tok = forgetting_attention(fq.reshape(B, L, FOX_HEADS, FOX_HD), k_all, v_all,
                                       c_q, c_all, pos, k_pos).reshape(B, L, FOX_W).astype(x.dtype)
        mem_out = memory_attention(mq.reshape(B, L, MEM_HEADS, MEM_HD), mem_k[l], mem_v[l]).reshape(B, L, MEM_W)
        mixed = jnp.concatenate([tok, mem_out.astype(x.dtype)], axis=-1) @ w_o[l]
        x = layer_norm(DEEPNORM_ALPHA * x + mixed, ln1_g[l], ln1_b[l])
        ffn = hierarchical_moe(x, w_group[l], b_group[l], w_route[l], b_route[l], w_up[l], w_down[l])
        x = layer_norm(DEEPNORM_ALPHA * x + ffn, ln2_g[l], ln2_b[l])
    return x, jnp.stack(new_ret), k_new, v_new, logf_new.astype(x.dtype)


def setup_inputs(seed: int = 0) -> dict:
    key = jax.random.key(seed)
    ks = jax.random.split(key, 32)
    n_pages = PAST_LEN // PAGE_SIZE
    n_used = DEC_BATCH * n_pages
    n_phys = n_used + max(1, n_used // 4)

    def nrm(k, shape, s):
        return jax.random.normal(k, shape, F32) * s

    s_d = D_MODEL ** -0.5
    x_prompt = nrm(ks[0], (BATCH, SEQ, D_MODEL), 1.0)
    x_sample = nrm(ks[1], (DEC_BATCH, DEC_SEQ, D_MODEL), 1.0)
    mem_prompt = nrm(ks[2], (BATCH, MEM_LEN, D_MODEL), 1.0)
    state_ret = nrm(ks[3], (N_A_LAYERS, DEC_BATCH, RET_HEADS, RET_DK, RET_DV), 0.1)
    cache_fox_k = nrm(ks[4], (n_phys, PAGE_SIZE, FOX_HEADS, FOX_HD), 1.0)
    cache_fox_v = nrm(ks[5], (n_phys, PAGE_SIZE, FOX_HEADS, FOX_HD), DEEPNORM_BETA)
    cache_fox_logf = jax.nn.log_sigmoid(FORGET_BIAS + nrm(ks[6], (n_phys, PAGE_SIZE, FOX_HEADS), 1.0))
    cache_mem_k = nrm(ks[7], (DEPTH, DEC_BATCH, MEM_LEN, MEM_HEADS, MEM_HD), 1.0)
    cache_mem_v = nrm(ks[8], (DEPTH, DEC_BATCH, MEM_LEN, MEM_HEADS, MEM_HD), DEEPNORM_BETA)
    page_table = jax.random.permutation(ks[9], n_phys)[:n_used].reshape(DEC_BATCH, n_pages).astype(jnp.int32)
    w_in_a = nrm(ks[10], (N_A_LAYERS, D_MODEL, IN_A_W), s_d).at[:, :, 2 * RET_W:3 * RET_W].multiply(DEEPNORM_BETA)
    w_in_b = nrm(ks[11], (N_B_LAYERS, D_MODEL, IN_B_W), s_d)
    w_kv_shared = nrm(ks[12], (D_MODEL, KV_SHARED_W), s_d).at[:, FOX_W:2 * FOX_W].multiply(DEEPNORM_BETA)
    b_forget = FORGET_BIAS + nrm(ks[13], (FOX_HEADS,), 0.1)
    w_o = nrm(ks[14], (DEPTH, MIX_W, D_MODEL), MIX_W ** -0.5 * DEEPNORM_BETA)
    w_mem_kv = nrm(ks[15], (DEPTH, D_MODEL, 2 * MEM_W), s_d).at[:, :, MEM_W:].multiply(DEEPNORM_BETA)
    ln1_g = 1.0 + nrm(ks[16], (DEPTH, D_MODEL), 0.01)
    ln1_b = nrm(ks[17], (DEPTH, D_MODEL), 0.01)
    ln2_g = 1.0 + nrm(ks[18], (DEPTH, D_MODEL), 0.01)
    ln2_b = nrm(ks[19], (DEPTH, D_MODEL), 0.01)
    w_group = nrm(ks[20], (DEPTH, D_MODEL, N_GROUPS), s_d)
    b_group = nrm(ks[21], (DEPTH, N_GROUPS), 0.01)
    w_route = nrm(ks[22], (DEPTH, N_GROUPS, D_MODEL, EXPERTS_PER_GROUP), s_d)
    b_route = nrm(ks[23], (DEPTH, N_GROUPS, EXPERTS_PER_GROUP), 0.01)
    w_up = nrm(ks[24], (DEPTH, N_GROUPS, EXPERTS_PER_GROUP, D_MODEL, 2 * D_EXPERT), s_d)
    w_down = nrm(ks[25], (DEPTH, N_GROUPS, EXPERTS_PER_GROUP, D_EXPERT, D_MODEL), D_EXPERT ** -0.5 * DEEPNORM_BETA)
    return {'x_prompt': x_prompt, 'x_sample': x_sample, 'mem_prompt': mem_prompt,
            'state_ret': state_ret, 'cache_fox_k': cache_fox_k, 'cache_fox_v': cache_fox_v,
            'cache_fox_logf': cache_fox_logf, 'cache_mem_k': cache_mem_k, 'cache_mem_v': cache_mem_v,
            'page_table': page_table, 'w_in_a': w_in_a, 'w_in_b': w_in_b, 'w_kv_shared': w_kv_shared,
            'b_forget': b_forget, 'w_o': w_o, 'w_mem_kv': w_mem_kv, 'ln1_g': ln1_g, 'ln1_b': ln1_b,
            'ln2_g': ln2_g, 'ln2_b': ln2_b, 'w_group': w_group, 'b_group': b_group,
            'w_route': w_route, 'b_route': b_route, 'w_up': w_up, 'w_down': w_down}


def reference(x_prompt, x_sample, mem_prompt, state_ret, cache_fox_k, cache_fox_v, cache_fox_logf,
              cache_mem_k, cache_mem_v, page_table, w_in_a, w_in_b, w_kv_shared, b_forget, w_o,
              w_mem_kv, ln1_g, ln1_b, ln2_g, ln2_b, w_group, b_group, w_route, b_route, w_up, w_down):
    bp = x_prompt.shape[0]
    dt = x_prompt.dtype
    mem_kv = jnp.einsum('bmd,ldf->lbmf', mem_prompt, w_mem_kv).reshape(DEPTH, bp, MEM_LEN, 2, MEM_HEADS, MEM_HD)
    mem_k_prompt = mem_kv[:, :, :, 0]
    mem_v_prompt = mem_kv[:, :, :, 1]
    y_prompt, ret_state_prompt, fox_k_prompt, fox_v_prompt, fox_logf_prompt = trunk(
        x_prompt, jnp.zeros((N_A_LAYERS, bp, RET_HEADS, RET_DK, RET_DV), dt), mem_k_prompt, mem_v_prompt,
        jnp.zeros((bp, 0, FOX_HEADS, FOX_HD), dt), jnp.zeros((bp, 0, FOX_HEADS, FOX_HD), dt),
        jnp.zeros((bp, 0, FOX_HEADS), dt),
        w_in_a, w_in_b, w_kv_shared, b_forget, w_o, ln1_g, ln1_b, ln2_g, ln2_b,
        w_group, b_group, w_route, b_route, w_up, w_down)
    past_k = gather_pages(cache_fox_k, page_table)
    past_v = gather_pages(cache_fox_v, page_table)
    past_logf = gather_pages(cache_fox_logf, page_table)
    y_sample, ret_state_sample, fox_k_sample, fox_v_sample, fox_logf_sample = trunk(
        x_sample, state_ret, cache_mem_k, cache_mem_v, past_k, past_v, past_logf,
        w_in_a, w_in_b, w_kv_shared, b_forget, w_o, ln1_g, ln1_b, ln2_g, ln2_b,
        w_group, b_group, w_route, b_route, w_up, w_down)
    return (y_prompt, y_sample, ret_state_prompt, ret_state_sample,
            fox_k_prompt, fox_v_prompt, fox_logf_prompt,
            fox_k_sample, fox_v_sample, fox_logf_sample,
            mem_k_prompt, mem_v_prompt)
```

```python
import functools
import math

import jax
import jax.numpy as jnp
from jax import lax
from jax.experimental import pallas as pl
from jax.experimental.pallas import tpu as pltpu

F32 = jnp.float32
BF16 = jnp.bfloat16
HIGHEST = lax.Precision.HIGHEST

RET_HEADS = 6
RET_D = 128
RET_W = RET_HEADS * RET_D
RET_CHUNK = 128
ROPE_BASE = 10000.0
FOX_HEADS = 12
FOX_HD = 64
FOX_W = FOX_HEADS * FOX_HD
MEM_HEADS = 4
MEM_HD = 64
MEM_W = MEM_HEADS * MEM_HD
N_GROUPS = 4
EXPERTS_PER_GROUP = 8
N_EXPERTS = N_GROUPS * EXPERTS_PER_GROUP
D_EXPERT = 256
LN_EPS = 1e-5
HEAD_NORM_EPS = 1e-6

LANES = 128
SUBLANES = 8
VMEM_LIMIT_BYTES = 48 * 1024 * 1024

ROW_TILE = 512
MOE_TILE = 256
FOX_BLOCK = 512
SAMPLE_BATCH_TILE = 8


def _cparams(n_axes):
    return pltpu.CompilerParams(
        dimension_semantics=("arbitrary",) * n_axes, vmem_limit_bytes=VMEM_LIMIT_BYTES)


def _dot(a, b):
    return jnp.dot(a, b, preferred_element_type=F32)


def _dot_nt(a, b):
    return lax.dot_general(a, b, (((1,), (1,)), ((), ())), preferred_element_type=F32)


def _dot_tn(a, b):
    return lax.dot_general(a, b, (((0,), (0,)), ((), ())), preferred_element_type=F32)


def _layer_norm(y, g, b):
    mu = jnp.mean(y, axis=-1, keepdims=True)
    d = y - mu
    var = jnp.mean(d * d, axis=-1, keepdims=True)
    return d * lax.rsqrt(var + LN_EPS) * g + b


def _matmul_kernel(x_ref, w_ref, o_ref):
    o_ref[...] = _dot(x_ref[...].astype(BF16), w_ref[...])


def matmul(x, w_bf16):
    t, k = x.shape
    n = w_bf16.shape[1]
    tm = min(ROW_TILE, t)
    return pl.pallas_call(
        _matmul_kernel,
        out_shape=jax.ShapeDtypeStruct((t, n), F32),
        grid=(t // tm,),
        in_specs=[pl.BlockSpec((tm, k), lambda i: (i, 0)),
                  pl.BlockSpec((k, n), lambda i: (0, 0))],
        out_specs=pl.BlockSpec((tm, n), lambda i: (i, 0)),
        compiler_params=_cparams(1),
        name="matmul",
    )(x, w_bf16)


def _proj_a_kernel(x_ref, w_ref, cos_ref, sin_ref, o_ref, *, k_scale):
    x = x_ref[...].astype(BF16)
    cos2 = cos_ref[...]
    sin2 = sin_ref[...]
    for c in range(2 * RET_HEADS):
        cols = slice(c * RET_D, (c + 1) * RET_D)
        y = _dot(x, w_ref[:, cols])
        y = y * cos2 + pltpu.roll(y, RET_D // 2, 1) * sin2
        if c >= RET_HEADS:
            y = y * k_scale
        o_ref[:, cols] = y
    o_ref[:, 2 * RET_W:] = _dot(x, w_ref[:, 2 * RET_W:])


def proj_a(x, w_bf16, cos2, sin2):
    t, k = x.shape
    n = w_bf16.shape[1]
    tm = min(ROW_TILE, t)
    return pl.pallas_call(
        functools.partial(_proj_a_kernel, k_scale=RET_D ** -0.5),
        out_shape=jax.ShapeDtypeStruct((t, n), F32),
        grid=(t // tm,),
        in_specs=[pl.BlockSpec((tm, k), lambda i: (i, 0)),
                  pl.BlockSpec((k, n), lambda i: (0, 0)),
                  pl.BlockSpec((tm, RET_D), lambda i: (i, 0)),
                  pl.BlockSpec((tm, RET_D), lambda i: (i, 0))],
        out_specs=pl.BlockSpec((tm, n), lambda i: (i, 0)),
        compiler_params=_cparams(1),
        name="proj_a",
    )(x, w_bf16, cos2, sin2)


def _kv_kernel(x_ref, w_ref, bf_ref, kv_ref, logf_ref):
    x = x_ref[...].astype(BF16)
    kv_ref[...] = _dot(x, w_ref[:, :2 * FOX_W])
    z = _dot(x, w_ref[:, 2 * FOX_W:]) + bf_ref[...]
    logf_ref[...] = jnp.minimum(z, 0.0) - jnp.log(1.0 + jnp.exp(-jnp.abs(z)))


def kv_shared(x, w_pad_bf16, b_forget_pad):
    t, k = x.shape
    n = w_pad_bf16.shape[1]
    tm = min(ROW_TILE, t)
    return pl.pallas_call(
        _kv_kernel,
        out_shape=(jax.ShapeDtypeStruct((t, 2 * FOX_W), F32),
                   jax.ShapeDtypeStruct((t, LANES), F32)),
        grid=(t // tm,),
        in_specs=[pl.BlockSpec((tm, k), lambda i: (i, 0)),
                  pl.BlockSpec((k, n), lambda i: (0, 0)),
                  pl.BlockSpec((1, LANES), lambda i: (0, 0))],
        out_specs=(pl.BlockSpec((tm, 2 * FOX_W), lambda i: (i, 0)),
                   pl.BlockSpec((tm, LANES), lambda i: (i, 0))),
        compiler_params=_cparams(1),
        name="kv_shared",
    )(x, w_pad_bf16, b_forget_pad)


def _out_ln_kernel(tok_ref, mem_ref, x_ref, w1_ref, w2_ref, g_ref, b_ref, o_ref, *, alpha):
    mixed = _dot(tok_ref[...].astype(BF16), w1_ref[...]) + _dot(mem_ref[...].astype(BF16), w2_ref[...])
    o_ref[...] = _layer_norm(alpha * x_ref[...] + mixed, g_ref[...], b_ref[...])


def out_proj_ln(tok, mem, x, w1_bf16, w2_bf16, g, b, alpha):
    t, d = x.shape
    tm = min(ROW_TILE, t)
    wt, wm = tok.shape[1], mem.shape[1]
    return pl.pallas_call(
        functools.partial(_out_ln_kernel, alpha=alpha),
        out_shape=jax.ShapeDtypeStruct((t, d), F32),
        grid=(t // tm,),
        in_specs=[pl.BlockSpec((tm, wt), lambda i: (i, 0)),
                  pl.BlockSpec((tm, wm), lambda i: (i, 0)),
                  pl.BlockSpec((tm, d), lambda i: (i, 0)),
                  pl.BlockSpec((wt, d), lambda i: (0, 0)),
                  pl.BlockSpec((wm, d), lambda i: (0, 0)),
                  pl.BlockSpec((1, d), lambda i: (0, 0)),
                  pl.BlockSpec((1, d), lambda i: (0, 0))],
        out_specs=pl.BlockSpec((tm, d), lambda i: (i, 0)),
        compiler_params=_cparams(1),
        name="out_proj_ln",
    )(tok, mem, x, w1_bf16, w2_bf16, g, b)


def _head_norm_gate(r, g):
    mu = jnp.mean(r, axis=-1, keepdims=True)
    d = r - mu
    var = jnp.mean(d * d, axis=-1, keepdims=True)
    return d * lax.rsqrt(var + HEAD_NORM_EPS) * (g * jax.nn.sigmoid(g))


def _retention_prompt_kernel(q_ref, k_ref, v_ref, g_ref, s0_ref, dm_ref, qd_ref, kd_ref, cd_ref,
                             o_ref, sout_ref, s_scr, *, chunks):
    i = pl.program_id(1)

    @pl.when(i == 0)
    def _():
        s_scr[...] = s0_ref[0]

    for c in range(chunks):
        rows = slice(c * RET_CHUNK, (c + 1) * RET_CHUNK)
        for h in range(RET_HEADS):
            cols = slice(h * RET_D, (h + 1) * RET_D)
            q = q_ref[rows, cols]
            k = k_ref[rows, cols]
            v = v_ref[rows, cols].astype(BF16)
            s_prev = s_scr[h]
            scores = _dot_nt(q.astype(BF16), k.astype(BF16)) * dm_ref[h]
            intra = _dot(scores.astype(BF16), v)
            cross = _dot((q * qd_ref[h]).astype(BF16), s_prev.astype(BF16))
            kd = (k * kd_ref[h]).astype(BF16)
            s_scr[h] = cd_ref[h] * s_prev + _dot_tn(kd, v)
            o_ref[rows, cols] = _head_norm_gate(intra + cross, g_ref[rows, cols])

    @pl.when(i == pl.num_programs(1) - 1)
    def _():
        sout_ref[0] = s_scr[...]


def retention_prompt(h, s0, decays, batch, seq):
    chunks = math.gcd(4, seq // RET_CHUNK)
    rows = chunks * RET_CHUNK
    n_i = seq // rows
    dm, qd, kd, cd = decays
    const = pl.BlockSpec((RET_HEADS, RET_CHUNK, RET_D), lambda b, i: (0, 0, 0))
    state = pl.BlockSpec((1, RET_HEADS, RET_D, RET_D), lambda b, i: (b, 0, 0, 0))

    def col(j):
        return pl.BlockSpec((rows, RET_W), lambda b, i, j=j: (b * n_i + i, j))

    return pl.pallas_call(
        functools.partial(_retention_prompt_kernel, chunks=chunks),
        out_shape=(jax.ShapeDtypeStruct((batch * seq, RET_W), F32),
                   jax.ShapeDtypeStruct((batch, RET_HEADS, RET_D, RET_D), F32)),
        grid=(batch, n_i),
        in_specs=[col(0), col(1), col(2), col(3), state, const, const, const, const],
        out_specs=(pl.BlockSpec((rows, RET_W), lambda b, i: (b * n_i + i, 0)), state),
        scratch_shapes=[pltpu.VMEM((RET_HEADS, RET_D, RET_D), F32)],
        compiler_params=_cparams(2),
        name="retention_prompt",
    )(h, h, h, h, s0, dm, qd, kd, cd)


def _retention_sample_kernel(qt_ref, kt_ref, v_ref, g_ref, s0_ref, o_ref, sout_ref, r_scr,
                             *, n_req, n_tok, gammas):
    for j in range(n_req):
        for h in range(RET_HEADS):
            cols = slice(h * RET_D, (h + 1) * RET_D)
            s = s0_ref[j, h]
            for t in range(n_tok):
                r = j * n_tok + t
                s = gammas[h] * s + kt_ref[0, h, :, r:r + 1] * v_ref[r:r + 1, cols]
                r_scr[r:r + 1, cols] = jnp.sum(qt_ref[0, h, :, r:r + 1] * s, axis=0, keepdims=True)
            sout_ref[j, h] = s
    for h in range(RET_HEADS):
        cols = slice(h * RET_D, (h + 1) * RET_D)
        o_ref[:, cols] = _head_norm_gate(r_scr[:, cols], g_ref[:, cols])


def retention_sample(h, qt, kt, s0, row0, n_req_total, n_tok):
    n_req = SAMPLE_BATCH_TILE
    rows = n_req * n_tok
    steps = n_req_total // n_req
    blk0 = row0 // rows
    gammas = tuple(1.0 - 2.0 ** (-5.0 - hh) for hh in range(RET_HEADS))
    state = pl.BlockSpec((n_req, RET_HEADS, RET_D, RET_D), lambda i: (i, 0, 0, 0))
    tr = pl.BlockSpec((1, RET_HEADS, RET_D, rows), lambda i: (i, 0, 0, 0))
    return pl.pallas_call(
        functools.partial(_retention_sample_kernel, n_req=n_req, n_tok=n_tok, gammas=gammas),
        out_shape=(jax.ShapeDtypeStruct((n_req_total * n_tok, RET_W), F32),
                   jax.ShapeDtypeStruct(s0.shape, F32)),
        grid=(steps,),
        in_specs=[tr, tr,
                  pl.BlockSpec((rows, RET_W), lambda i: (blk0 + i, 2)),
                  pl.BlockSpec((rows, RET_W), lambda i: (blk0 + i, 3)),
                  state],
        out_specs=(pl.BlockSpec((rows, RET_W), lambda i: (i, 0)), state),
        scratch_shapes=[pltpu.VMEM((rows, RET_W), F32)],
        compiler_params=_cparams(1),
        name="retention_sample",
    )(qt, kt, h, h, s0)


def _mem_attn_kernel(q_ref, mk_ref, mv_ref, o_ref, *, banks, rows_per_bank):
    q = q_ref[...] * (MEM_HD ** -0.5)
    n_rows = q.shape[0]
    out = None
    for j in range(banks):
        mk = mk_ref[j].astype(BF16)
        mv = mv_ref[j].astype(BF16)
        heads = []
        for h in range(MEM_HEADS):
            cols = slice(h * MEM_HD, (h + 1) * MEM_HD)
            s = _dot_nt(q[:, cols].astype(BF16), mk[:, cols])
            p = jnp.exp(s - jnp.max(s, axis=-1, keepdims=True))
            p = p / jnp.sum(p, axis=-1, keepdims=True)
            heads.append(_dot(p.astype(BF16), mv[:, cols]))
        o = jnp.concatenate(heads, axis=1)
        if out is None:
            out = o
        else:
            row = lax.broadcasted_iota(jnp.int32, (n_rows, MEM_W), 0)
            out = jnp.where(row >= j * rows_per_bank, o, out)
    o_ref[...] = out


def mem_attention(h, q_col_block, mk, mv, bank0, *, row_block0, n_steps, rows, banks, steps_per_bank):
    mem_len = mk.shape[1]
    if banks == 1:
        bank_map = lambda i: (bank0 + i // steps_per_bank, 0, 0)
    else:
        bank_map = lambda i: (bank0 // banks + i, 0, 0)
    return pl.pallas_call(
        functools.partial(_mem_attn_kernel, banks=banks, rows_per_bank=rows // banks),
        out_shape=jax.ShapeDtypeStruct((n_steps * rows, MEM_W), F32),
        grid=(n_steps,),
        in_specs=[pl.BlockSpec((rows, MEM_W), lambda i: (row_block0 + i, q_col_block)),
                  pl.BlockSpec((banks, mem_len, MEM_W), bank_map),
                  pl.BlockSpec((banks, mem_len, MEM_W), bank_map)],
        out_specs=pl.BlockSpec((rows, MEM_W), lambda i: (i, 0)),
        compiler_params=_cparams(1),
        name="mem_attention",
    )(h, mk, mv)


def _cumsum_kernel(x_ref, o_ref, carry):
    @pl.when(pl.program_id(1) == 0)
    def _():
        carry[...] = jnp.zeros_like(carry)

    n = x_ref.shape[0]
    row = lax.broadcasted_iota(jnp.int32, (n, n), 0)
    col = lax.broadcasted_iota(jnp.int32, (n, n), 1)
    tri = (row >= col).astype(F32)
    c = jnp.dot(tri, x_ref[...], precision=HIGHEST, preferred_element_type=F32) + carry[...]
    o_ref[...] = c
    carry[...] = c[n - 1:n, :]


def cumsum_rows(x, batch, seq):
    tm = min(ROW_TILE, seq)
    n_i = seq // tm
    return pl.pallas_call(
        _cumsum_kernel,
        out_shape=jax.ShapeDtypeStruct((batch * seq, x.shape[1]), F32),
        grid=(batch, n_i),
        in_specs=[pl.BlockSpec((tm, x.shape[1]), lambda b, i: (b * n_i + i, 0))],
        out_specs=pl.BlockSpec((tm, x.shape[1]), lambda b, i: (b * n_i + i, 0)),
        scratch_shapes=[pltpu.VMEM((1, x.shape[1]), F32)],
        compiler_params=_cparams(2),
        name="cumsum_rows",
    )(x)


def _fox_prompt_kernel(q_ref, k_ref, v_ref, c_ref, cka_ref, ckb_ref, o_ref, m_scr, l_scr, acc_scr, *, blk):
    p = pl.program_id(1)
    i = pl.program_id(2)
    lane = lax.broadcasted_iota(jnp.int32, (blk, LANES), 1)
    q2 = q_ref[...] * (FOX_HD ** -0.5)
    c2 = c_ref[...]
    qs = (jnp.where(lane < FOX_HD, q2, 0.0).astype(BF16), jnp.where(lane >= FOX_HD, q2, 0.0).astype(BF16))
    cqs = tuple(jnp.sum(jnp.where(lane == 2 * p + e, c2, 0.0), axis=1, keepdims=True) for e in range(2))
    ck_refs = (cka_ref, ckb_ref)
    m_scr[...] = jnp.full(m_scr.shape, -jnp.inf, F32)
    l_scr[...] = jnp.zeros(l_scr.shape, F32)
    acc_scr[...] = jnp.zeros(acc_scr.shape, F32)

    def block(j, masked):
        start = pl.multiple_of(j * blk, blk)
        kb = k_ref[pl.ds(start, blk), :].astype(BF16)
        vb = v_ref[pl.ds(start, blk), :].astype(BF16)
        for e in range(2):
            ck = ck_refs[e][0, 0, :, pl.ds(start, blk)]
            s = _dot_nt(qs[e], kb) + (cqs[e] - ck)
            if masked:
                row = lax.broadcasted_iota(jnp.int32, (blk, blk), 0)
                col = lax.broadcasted_iota(jnp.int32, (blk, blk), 1)
                s = jnp.where(col <= row, s, -jnp.inf)
            m_old = m_scr[e]
            m_new = jnp.maximum(m_old, jnp.max(s, axis=-1, keepdims=True))
            a = jnp.exp(m_old - m_new)
            pe = jnp.exp(s - m_new)
            l_scr[e] = a * l_scr[e] + jnp.sum(pe, axis=-1, keepdims=True)
            acc_scr[e] = a * acc_scr[e] + _dot(pe.astype(BF16), vb)
            m_scr[e] = m_new

    def body(j, carry):
        block(j, False)
        return carry

    lax.fori_loop(0, i, body, 0)
    block(i, True)
    o_ref[...] = jnp.where(lane < FOX_HD, acc_scr[0] / l_scr[0], acc_scr[1] / l_scr[1])


def fox_prompt(h, kv, c, ct, batch, seq):
    blk = min(FOX_BLOCK, seq)
    n_q = seq // blk
    pairs = FOX_HEADS // 2
    return pl.pallas_call(
        functools.partial(_fox_prompt_kernel, blk=blk),
        out_shape=jax.ShapeDtypeStruct((batch * seq, FOX_W), F32),
        grid=(batch, pairs, n_q),
        in_specs=[pl.BlockSpec((blk, LANES), lambda b, p, i: (b * n_q + i, p)),
                  pl.BlockSpec((seq, LANES), lambda b, p, i: (b, p)),
                  pl.BlockSpec((seq, LANES), lambda b, p, i: (b, pairs + p)),
                  pl.BlockSpec((blk, LANES), lambda b, p, i: (b * n_q + i, 0)),
                  pl.BlockSpec((1, 1, 1, seq), lambda b, p, i: (b, 2 * p, 0, 0)),
                  pl.BlockSpec((1, 1, 1, seq), lambda b, p, i: (b, 2 * p + 1, 0, 0))],
        out_specs=pl.BlockSpec((blk, LANES), lambda b, p, i: (b * n_q + i, p)),
        scratch_shapes=[pltpu.VMEM((2, blk, 1), F32), pltpu.VMEM((2, blk, 1), F32),
                        pltpu.VMEM((2, blk, LANES), F32)],
        compiler_params=_cparams(3),
        name="fox_prompt",
    )(h, kv, kv, c, ct, ct)


FOX_COLS = 16


def _fox_sample_kernel(pt_ref, q_ref, kvn_ref, lfn_ref, *refs, n_pages, page, n_tok):
    k_refs = refs[:n_pages]
    v_refs = refs[n_pages:2 * n_pages]
    lf_refs = refs[2 * n_pages:3 * n_pages]
    o_ref = refs[3 * n_pages]
    s_scr = refs[3 * n_pages + 1]
    n_past = n_pages * page

    q = q_ref[0][:, :FOX_W] * (FOX_HD ** -0.5)
    head_of_lane = lax.broadcasted_iota(jnp.int32, (FOX_COLS, FOX_W), 1) // FOX_HD
    head_of_row = lax.broadcasted_iota(jnp.int32, (FOX_COLS, FOX_W), 0)
    q_rows = [jnp.where(head_of_lane == head_of_row, jnp.broadcast_to(q[t:t + 1, :], (FOX_COLS, FOX_W)), 0.0)
              for t in range(n_tok)]
    pad_rows = LANES - n_tok * FOX_COLS
    q_bd = jnp.concatenate(q_rows + [jnp.zeros((pad_rows, FOX_W), F32)], axis=0).astype(BF16)

    e_row = lax.broadcasted_iota(jnp.int32, (FOX_HEADS, LANES), 0)
    e_col = lax.broadcasted_iota(jnp.int32, (FOX_HEADS, LANES), 1)
    expand = ((e_col % FOX_COLS == e_row) & (e_col < n_tok * FOX_COLS)).astype(F32)
    t_row = lax.broadcasted_iota(jnp.int32, (page, page), 0)
    t_col = lax.broadcasted_iota(jnp.int32, (page, page), 1)
    tri = (t_row >= t_col).astype(F32)

    carry = jnp.zeros((1, FOX_HEADS), F32)
    for pg in range(n_pages):
        c_in = jnp.dot(tri, lf_refs[pg][0], precision=HIGHEST, preferred_element_type=F32) + carry
        carry = c_in[page - 1:page, :]
        ck = jnp.dot(c_in, expand, precision=HIGHEST, preferred_element_type=F32)
        s_scr[pg * page:(pg + 1) * page, :] = _dot_nt(k_refs[pg][0].astype(BF16), q_bd) - ck

    lfn = lfn_ref[0][:, :FOX_HEADS]
    c_rows = []
    for t in range(n_tok):
        carry = carry + lfn[t:t + 1, :]
        c_rows.append(carry)
    c_new = jnp.dot(jnp.concatenate(c_rows, axis=0), expand, precision=HIGHEST,
                    preferred_element_type=F32)
    n_row = lax.broadcasted_iota(jnp.int32, (n_tok, LANES), 0)
    n_col = lax.broadcasted_iota(jnp.int32, (n_tok, LANES), 1)
    tok_of_col = n_col // FOX_COLS
    cq = jnp.sum(jnp.where(tok_of_col == n_row, c_new, 0.0), axis=0, keepdims=True)
    kvn = kvn_ref[0]
    s_new = _dot_nt(kvn[:, :FOX_W].astype(BF16), q_bd) - c_new
    s_new = jnp.where(n_row <= tok_of_col, s_new, -jnp.inf)
    s_scr[n_past:n_past + n_tok, :] = s_new
    s_scr[n_past + n_tok:, :] = jnp.full((s_scr.shape[0] - n_past - n_tok, LANES), -jnp.inf, F32)

    s_all = s_scr[...] + cq
    pr = jnp.exp(s_all - jnp.max(s_all, axis=0, keepdims=True))
    s_scr[...] = pr / jnp.sum(pr, axis=0, keepdims=True)

    acc = jnp.zeros((LANES, FOX_W), F32)
    for pg in range(n_pages):
        pt = s_scr[pg * page:(pg + 1) * page, :].T.astype(BF16)
        acc = acc + _dot(pt, v_refs[pg][0].astype(BF16))
    n_pad = s_scr.shape[0] - n_past
    v_new = jnp.concatenate([kvn[:, FOX_W:], jnp.zeros((n_pad - n_tok, FOX_W), F32)], axis=0).astype(BF16)
    acc = acc + _dot_tn(s_scr[n_past:, :].astype(BF16), v_new)

    a_row = lax.broadcasted_iota(jnp.int32, (LANES, FOX_W), 0)
    a_col = lax.broadcasted_iota(jnp.int32, (LANES, FOX_W), 1)
    acc = jnp.where(a_col // FOX_HD == a_row % FOX_COLS, acc, 0.0)
    out = acc.reshape(LANES // FOX_COLS, FOX_COLS, FOX_W).sum(axis=1)
    o_ref[0] = out[:n_tok, :]


def fox_sample(h3, kv3, logf3, cache_k, cache_v, cache_logf, page_table, tok0, n_req, n_tok):
    n_pages = page_table.shape[1]
    n_phys, page = cache_k.shape[0], cache_k.shape[1]
    ck = cache_k.reshape(n_phys, page, FOX_W)
    cv = cache_v.reshape(n_phys, page, FOX_W)
    pt_flat = page_table.reshape(-1)

    def page_spec(width, pg):
        return pl.BlockSpec((1, page, width), lambda r, pt, pg=pg: (pt[r * n_pages + pg], 0, 0))

    def new_spec(width):
        return pl.BlockSpec((1, n_tok, width), lambda r, pt: (tok0 + r, 0, 0))

    in_specs = ([new_spec(h3.shape[2]), new_spec(kv3.shape[2]), new_spec(logf3.shape[2])]
                + [page_spec(FOX_W, pg) for pg in range(n_pages)]
                + [page_spec(FOX_W, pg) for pg in range(n_pages)]
                + [page_spec(FOX_HEADS, pg) for pg in range(n_pages)])
    grid_spec = pltpu.PrefetchScalarGridSpec(
        num_scalar_prefetch=1,
        grid=(n_req,),
        in_specs=in_specs,
        out_specs=pl.BlockSpec((1, n_tok, FOX_W), lambda r, pt: (r, 0, 0)),
        scratch_shapes=[pltpu.VMEM((n_pages * page + SUBLANES, LANES), F32)],
    )
    return pl.pallas_call(
        functools.partial(_fox_sample_kernel, n_pages=n_pages, page=page, n_tok=n_tok),
        out_shape=jax.ShapeDtypeStruct((n_req, n_tok, FOX_W), F32),
        grid_spec=grid_spec,
        compiler_params=_cparams(1),
        name="fox_sample",
    )(pt_flat, h3, kv3, logf3, *([ck] * n_pages), *([cv] * n_pages), *([cache_logf] * n_pages))


def _router_kernel(x_ref, w_ref, b_ref, o_ref):
    logits = jnp.dot(x_ref[...], w_ref[...], precision=HIGHEST, preferred_element_type=F32) + b_ref[...]
    lane = lax.broadcasted_iota(jnp.int32, logits.shape, 1)
    big = jnp.int32(LANES)
    neg = -jnp.inf
    gl = jnp.where(lane < N_GROUPS, logits, neg)
    g_max = jnp.max(gl, axis=-1, keepdims=True)
    g_idx = jnp.min(jnp.where(gl == g_max, lane, big), axis=-1, keepdims=True)
    g_w = 1.0 / jnp.sum(jnp.exp(gl - g_max), axis=-1, keepdims=True)
    in_group = (lane >= N_GROUPS) & (lane < N_GROUPS + N_EXPERTS) & (((lane - N_GROUPS) >> 3) == g_idx)
    el = jnp.where(in_group, logits, neg)
    v1 = jnp.max(el, axis=-1, keepdims=True)
    i1 = jnp.min(jnp.where(el == v1, lane, big), axis=-1, keepdims=True)
    el2 = jnp.where(lane == i1, neg, el)
    v2 = jnp.max(el2, axis=-1, keepdims=True)
    i2 = jnp.min(jnp.where(el2 == v2, lane, big), axis=-1, keepdims=True)
    e2 = jnp.exp(v2 - v1)
    w1 = g_w / (1.0 + e2)
    w2 = g_w * e2 / (1.0 + e2)
    out = jnp.where(lane == 0, (i1 - N_GROUPS).astype(F32),
                    jnp.where(lane == 1, (i2 - N_GROUPS).astype(F32),
                              jnp.where(lane == 2, w1, jnp.where(lane == 3, w2, 0.0))))
    o_ref[...] = out


def router(x, w_pad, b_pad):
    t, d = x.shape
    tm = min(ROW_TILE, t)
    return pl.pallas_call(
        _router_kernel,
        out_shape=jax.ShapeDtypeStruct((t, LANES), F32),
        grid=(t // tm,),
        in_specs=[pl.BlockSpec((tm, d), lambda i: (i, 0)),
                  pl.BlockSpec((d, LANES), lambda i: (0, 0)),
                  pl.BlockSpec((1, LANES), lambda i: (0, 0))],
        out_specs=pl.BlockSpec((tm, LANES), lambda i: (i, 0)),
        compiler_params=_cparams(1),
        name="router",
    )(x, w_pad, b_pad)


def _row_copy(src_ref, src_row, dst_ref, dst_row, sem):
    return pltpu.make_async_copy(src_ref.at[pl.ds(src_row, 1)], dst_ref.at[pl.ds(dst_row, 1)], sem)


def _dispatch_kernel(pos_ref, x_ref, xs_in_ref, xs_ref, sem, *, tm):
    del xs_in_ref

    def issue(r, carry):
        for k in range(2):
            _row_copy(x_ref, r, xs_ref, pos_ref[0, 0, k * tm + r], sem).start()
        return carry

    lax.fori_loop(0, tm, issue, 0, unroll=8)

    def drain(r, carry):
        for k in range(2):
            _row_copy(x_ref, 0, xs_ref, 0, sem).wait()
        return carry

    lax.fori_loop(0, tm, drain, 0, unroll=8)


def moe_dispatch(x, pos3, xs_init):
    t, d = x.shape
    tm = MOE_TILE
    return pl.pallas_call(
        functools.partial(_dispatch_kernel, tm=tm),
        out_shape=jax.ShapeDtypeStruct(xs_init.shape, F32),
        grid=(t // tm,),
        in_specs=[pl.BlockSpec((1, 1, 2 * tm), lambda i: (i, 0, 0), memory_space=pltpu.SMEM),
                  pl.BlockSpec((tm, d), lambda i: (i, 0)),
                  pl.BlockSpec(memory_space=pl.ANY)],
        out_specs=pl.BlockSpec(memory_space=pl.ANY),
        scratch_shapes=[pltpu.SemaphoreType.DMA(())],
        input_output_aliases={2: 0},
        compiler_params=_cparams(1),
        name="moe_dispatch",
    )(pos3, x, xs_init)


def _expert_kernel(te_ref, na_ref, x_ref, wu_ref, wd_ref, o_ref):
    @pl.when(pl.program_id(0) < na_ref[0])
    def _():
        up = _dot(x_ref[...].astype(BF16), wu_ref[0].astype(BF16))
        hid = jax.nn.silu(up[:, :D_EXPERT]) * up[:, D_EXPERT:]
        o_ref[...] = _dot(hid.astype(BF16), wd_ref[0].astype(BF16))

    @pl.when(pl.program_id(0) >= na_ref[0])
    def _():
        o_ref[...] = jnp.zeros_like(o_ref)


def moe_experts(xs, w_up, w_down, tile_expert, n_active):
    r, d = xs.shape
    tm = MOE_TILE
    n_tiles = r // tm

    def row_map(i, te, na):
        return (jnp.minimum(i, na[0] - 1), 0)

    grid_spec = pltpu.PrefetchScalarGridSpec(
        num_scalar_prefetch=2,
        grid=(n_tiles,),
        in_specs=[pl.BlockSpec((tm, d), row_map),
                  pl.BlockSpec((1, d, 2 * D_EXPERT), lambda i, te, na: (te[i], 0, 0)),
                  pl.BlockSpec((1, D_EXPERT, d), lambda i, te, na: (te[i], 0, 0))],
        out_specs=pl.BlockSpec((tm, d), lambda i, te, na: (i, 0)),
    )
    return pl.pallas_call(
        _expert_kernel,
        out_shape=jax.ShapeDtypeStruct((r, d), F32),
        grid_spec=grid_spec,
        compiler_params=_cparams(1),
        name="moe_experts",
    )(tile_expert, n_active, xs, w_up, w_down)


def _combine_kernel(pos_ref, x_ref, rt_ref, g_ref, b_ref, ys_ref, o_ref, buf, sem, *, tm, alpha):
    def issue(r, carry):
        for k in range(2):
            _row_copy(ys_ref, pos_ref[0, 0, k * tm + r], buf.at[k], r, sem).start()
        return carry

    lax.fori_loop(0, tm, issue, 0, unroll=8)

    def drain(r, carry):
        for k in range(2):
            _row_copy(ys_ref, 0, buf.at[k], 0, sem).wait()
        return carry

    lax.fori_loop(0, tm, drain, 0, unroll=8)
    rt = rt_ref[...]
    ffn = rt[:, 2:3] * buf[0] + rt[:, 3:4] * buf[1]
    o_ref[...] = _layer_norm(alpha * x_ref[...] + ffn, g_ref[...], b_ref[...])


def moe_combine(x, routing, ys, pos3, g, b, alpha):
    t, d = x.shape
    tm = MOE_TILE
    return pl.pallas_call(
        functools.partial(_combine_kernel, tm=tm, alpha=alpha),
        out_shape=jax.ShapeDtypeStruct((t, d), F32),
        grid=(t // tm,),
        in_specs=[pl.BlockSpec((1, 1, 2 * tm), lambda i: (i, 0, 0), memory_space=pltpu.SMEM),
                  pl.BlockSpec((tm, d), lambda i: (i, 0)),
                  pl.BlockSpec((tm, LANES), lambda i: (i, 0)),
                  pl.BlockSpec((1, d), lambda i: (0, 0)),
                  pl.BlockSpec((1, d), lambda i: (0, 0)),
                  pl.BlockSpec(memory_space=pl.ANY)],
        out_specs=pl.BlockSpec((tm, d), lambda i: (i, 0)),
        scratch_shapes=[pltpu.VMEM((2, tm, d), F32), pltpu.SemaphoreType.DMA(())],
        compiler_params=_cparams(1),
        name="moe_combine",
    )(pos3, x, routing, g, b, ys)


def moe_plan(routing, n_tiles):
    t = routing.shape[0]
    tm = MOE_TILE
    ids = routing[:, :2].astype(jnp.int32).T.reshape(-1)
    onehot = (ids[:, None] == jnp.arange(N_EXPERTS, dtype=jnp.int32)[None, :]).astype(jnp.int32)
    csum = jnp.cumsum(onehot, axis=0)
    rank = jnp.sum(csum * onehot, axis=1) - 1
    counts = csum[-1]
    tiles_e = (counts + tm - 1) // tm
    tile_end = jnp.cumsum(tiles_e)
    offs = (tile_end - tiles_e) * tm
    pos = offs[ids] + rank
    n_active = tile_end[-1:]
    tile_ids = jnp.minimum(jnp.arange(n_tiles, dtype=jnp.int32), n_active[0] - 1)
    tile_expert = jnp.minimum(jnp.searchsorted(tile_end, tile_ids, side="right"), N_EXPERTS - 1).astype(jnp.int32)
    pos3 = pos.reshape(2, t // tm, tm).transpose(1, 0, 2).reshape(t // tm, 1, 2 * tm).astype(jnp.int32)
    return pos3, tile_expert, n_active.astype(jnp.int32)


def hierarchical_moe_ln(x, w_router, b_router, w_up, w_down, g, b, alpha):
    t, d = x.shape
    n_tiles = (2 * t) // MOE_TILE + N_EXPERTS
    routing = router(x, w_router, b_router)
    pos3, tile_expert, n_active = moe_plan(routing, n_tiles)
    xs = moe_dispatch(x, pos3, jnp.zeros((n_tiles * MOE_TILE, d), F32))
    ys = moe_experts(xs, w_up, w_down, tile_expert, n_active)
    return moe_combine(x, routing, ys, pos3, g, b, alpha)


def _rotary_tables(pos):
    half = RET_D // 2
    inv_freq = ROPE_BASE ** (-jnp.arange(half, dtype=F32) / half)
    ang = pos.astype(F32)[:, None] * inv_freq[None, :]
    cos, sin = jnp.cos(ang), jnp.sin(ang)
    return jnp.concatenate([cos, cos], axis=1), jnp.concatenate([-sin, sin], axis=1)


def _retention_decays():
    log_gamma = jnp.log(1.0 - 2.0 ** (-5.0 - jnp.arange(RET_HEADS, dtype=F32)))
    c = RET_CHUNK
    i = jnp.arange(c, dtype=F32)
    diff = i[:, None] - i[None, :]
    dm = jnp.where(diff >= 0, jnp.exp(log_gamma[:, None, None] * jnp.maximum(diff, 0.0)[None]), 0.0)
    shape = (RET_HEADS, c, RET_D)
    qd = jnp.broadcast_to(jnp.exp(log_gamma[:, None] * (i + 1.0)[None, :])[:, :, None], shape)
    kd = jnp.broadcast_to(jnp.exp(log_gamma[:, None] * (c - 1 - i)[None, :])[:, :, None], shape)
    cd = jnp.broadcast_to(jnp.exp(log_gamma * c)[:, None, None], shape)
    return dm, qd, kd, cd


def kernel(x_prompt, x_sample, mem_prompt, state_ret, cache_fox_k, cache_fox_v, cache_fox_logf, cache_mem_k, cache_mem_v, page_table, w_in_a, w_in_b, w_kv_shared, b_forget, w_o, w_mem_kv, ln1_g, ln1_b, ln2_g, ln2_b, w_group, b_group, w_route, b_route, w_up, w_down):
    bp, seq, d = x_prompt.shape
    bs, n_tok, _ = x_sample.shape
    depth = w_o.shape[0]
    n_a = w_in_a.shape[0]
    mem_len = mem_prompt.shape[1]
    n_pages, page = page_table.shape[1], cache_fox_k.shape[1]
    past = n_pages * page
    tp, ts = bp * seq, bs * n_tok
    alpha = (2 * depth) ** 0.25
    assert w_in_a.shape[2] == 4 * RET_W + MEM_W and w_in_b.shape[2] == FOX_W + MEM_W
    assert w_kv_shared.shape[1] == 2 * FOX_W + FOX_HEADS and w_up.shape[1:3] == (N_GROUPS, EXPERTS_PER_GROUP)
    assert seq % RET_CHUNK == 0 and n_tok < RET_CHUNK and tp % ROW_TILE == 0 and ts % ROW_TILE == 0

    x = jnp.concatenate([x_prompt.reshape(tp, d), x_sample.reshape(ts, d)], axis=0)
    t_all = tp + ts

    w_mem_all = jnp.transpose(w_mem_kv, (1, 0, 2)).reshape(d, depth * 2 * MEM_W).astype(BF16)
    mem_kv = matmul(mem_prompt.reshape(bp * mem_len, d), w_mem_all)
    mem_kv5 = mem_kv.reshape(bp, mem_len, depth, 2, MEM_HEADS, MEM_HD)
    mem_k_prompt = jnp.transpose(mem_kv5[:, :, :, 0], (2, 0, 1, 3, 4))
    mem_v_prompt = jnp.transpose(mem_kv5[:, :, :, 1], (2, 0, 1, 3, 4))
    mem_kv_banks = mem_kv.reshape(bp, mem_len, depth * 2 * MEM_W)
    cache_mk = cache_mem_k.reshape(depth * bs, mem_len, MEM_W)
    cache_mv = cache_mem_v.reshape(depth * bs, mem_len, MEM_W)

    pos_all = jnp.concatenate([jnp.tile(jnp.arange(seq), bp), jnp.tile(past + jnp.arange(n_tok), bs)])
    cos2, sin2 = _rotary_tables(pos_all)
    decays = _retention_decays()
    zero_state = jnp.zeros((bp, RET_HEADS, RET_D, RET_D), F32)

    w_router = jnp.concatenate(
        [w_group, jnp.transpose(w_route, (0, 2, 1, 3)).reshape(depth, d, N_EXPERTS),
         jnp.zeros((depth, d, LANES - N_GROUPS - N_EXPERTS), F32)], axis=2)
    b_router = jnp.concatenate(
        [b_group, b_route.reshape(depth, N_EXPERTS), jnp.zeros((depth, LANES - N_GROUPS - N_EXPERTS), F32)],
        axis=1)[:, None, :]
    w_up_e = w_up.reshape(depth, N_EXPERTS, d, 2 * D_EXPERT)
    w_down_e = w_down.reshape(depth, N_EXPERTS, D_EXPERT, d)

    sample_rows = SAMPLE_BATCH_TILE * n_tok
    ret_prompt, ret_sample = [], []
    kv = logf = None
    for l in range(depth):
        if l < n_a:
            h = proj_a(x, w_in_a[l].astype(BF16), cos2, sin2)
            tok_p, s_p = retention_prompt(h, zero_state, decays, bp, seq)
            hs = h[tp:, :2 * RET_W].reshape(bs // SAMPLE_BATCH_TILE, sample_rows, 2, RET_HEADS, RET_D)
            hs = jnp.transpose(hs, (2, 0, 3, 4, 1))
            tok_s, s_s = retention_sample(h, hs[0], hs[1], state_ret[l], tp, bs, n_tok)
            ret_prompt.append(s_p)
            ret_sample.append(s_s)
            mq_block = (4 * RET_W) // MEM_W
        else:
            if l == n_a:
                w_kv_pad = jnp.concatenate(
                    [w_kv_shared, jnp.zeros((d, LANES - FOX_HEADS), F32)], axis=1).astype(BF16)
                bf_pad = jnp.concatenate([b_forget, jnp.zeros((LANES - FOX_HEADS,), F32)])[None, :]
                kv, logf = kv_shared(x, w_kv_pad, bf_pad)
                c_prompt = cumsum_rows(logf, bp, seq)
                ct = jnp.transpose(c_prompt.reshape(bp, seq, LANES)[:, :, :FOX_COLS], (0, 2, 1))
                ct = ct.reshape(bp, FOX_COLS, 1, seq)
                kv3 = kv.reshape(t_all // n_tok, n_tok, 2 * FOX_W)
                logf3 = logf.reshape(t_all // n_tok, n_tok, LANES)
            h = matmul(x, w_in_b[l - n_a].astype(BF16))
            tok_p = fox_prompt(h, kv, c_prompt, ct, bp, seq)
            tok_s = fox_sample(h.reshape(t_all // n_tok, n_tok, FOX_W + MEM_W), kv3, logf3,
                               cache_fox_k, cache_fox_v, cache_fox_logf, page_table,
                               tp // n_tok, bs, n_tok).reshape(ts, FOX_W)
            mq_block = FOX_W // MEM_W
        mem_p = mem_attention(h, mq_block, mem_kv_banks[:, :, l * 2 * MEM_W:(l * 2 + 1) * MEM_W],
                              mem_kv_banks[:, :, (l * 2 + 1) * MEM_W:(l * 2 + 2) * MEM_W], 0,
                              row_block0=0, n_steps=tp // ROW_TILE, rows=ROW_TILE, banks=1,
                              steps_per_bank=seq // ROW_TILE)
        mem_s = mem_attention(h, mq_block, cache_mk, cache_mv, l * bs,
                              row_block0=tp // sample_rows, n_steps=bs // SAMPLE_BATCH_TILE,
                              rows=sample_rows, banks=SAMPLE_BATCH_TILE, steps_per_bank=1)
        tok = jnp.concatenate([tok_p, tok_s], axis=0)
        mem = jnp.concatenate([mem_p, mem_s], axis=0)
        x = out_proj_ln(tok, mem, x, w_o[l, :tok.shape[1]].astype(BF16), w_o[l, tok.shape[1]:].astype(BF16),
                        ln1_g[l][None, :], ln1_b[l][None, :], alpha)
        x = hierarchical_moe_ln(x, w_router[l], b_router[l], w_up_e[l], w_down_e[l],
                                ln2_g[l][None, :], ln2_b[l][None, :], alpha)

    y_prompt = x[:tp].reshape(bp, seq, d)
    y_sample = x[tp:].reshape(bs, n_tok, d)
    fox_k = kv[:, :FOX_W].reshape(t_all, FOX_HEADS, FOX_HD)
    fox_v = kv[:, FOX_W:].reshape(t_all, FOX_HEADS, FOX_HD)
    fox_lf = logf[:, :FOX_HEADS]
    return (y_prompt, y_sample, jnp.stack(ret_prompt), jnp.stack(ret_sample),
            fox_k[:tp].reshape(bp, seq, FOX_HEADS, FOX_HD), fox_v[:tp].reshape(bp, seq, FOX_HEADS, FOX_HD),
            fox_lf[:tp].reshape(bp, seq, FOX_HEADS),
            fox_k[tp:].reshape(bs, n_tok, FOX_HEADS, FOX_HD), fox_v[tp:].reshape(bs, n_tok, FOX_HEADS, FOX_HD),
            fox_lf[tp:].reshape(bs, n_tok, FOX_HEADS),
            mem_k_prompt, mem_v_prompt)
```

```python
import functools
import math

import jax
import jax.numpy as jnp
from jax import lax
from jax.experimental import pallas as pl
from jax.experimental.pallas import tpu as pltpu

F32 = jnp.float32
BF16 = jnp.bfloat16
HIGHEST = lax.Precision.HIGHEST

RET_HEADS = 6
RET_D = 128
RET_W = RET_HEADS * RET_D
RET_CHUNK = 128
ROPE_BASE = 10000.0
FOX_HEADS = 12
FOX_HD = 64
FOX_W = FOX_HEADS * FOX_HD
MEM_HEADS = 4
MEM_HD = 64
MEM_W = MEM_HEADS * MEM_HD
N_GROUPS = 4
EXPERTS_PER_GROUP = 8
N_EXPERTS = N_GROUPS * EXPERTS_PER_GROUP
D_EXPERT = 256
LN_EPS = 1e-5
HEAD_NORM_EPS = 1e-6

LANES = 128
SUBLANES = 8
VMEM_LIMIT_BYTES = 48 * 1024 * 1024

ROW_TILE = 512
MOE_TILE = 256
FOX_BLOCK = 512
SAMPLE_BATCH_TILE = 8


def _cparams(n_axes):
    return pltpu.CompilerParams(
        dimension_semantics=("arbitrary",) * n_axes, vmem_limit_bytes=VMEM_LIMIT_BYTES)


def _dot(a, b):
    return jnp.dot(a, b, preferred_element_type=F32)


def _dot_nt(a, b):
    return lax.dot_general(a, b, (((1,), (1,)), ((), ())), preferred_element_type=F32)


def _dot_tn(a, b):
    return lax.dot_general(a, b, (((0,), (0,)), ((), ())), preferred_element_type=F32)


def _layer_norm(y, g, b):
    mu = jnp.mean(y, axis=-1, keepdims=True)
    d = y - mu
    var = jnp.mean(d * d, axis=-1, keepdims=True)
    return d * lax.rsqrt(var + LN_EPS) * g + b


def _matmul_kernel(x_ref, w_ref, o_ref):
    o_ref[...] = _dot(x_ref[...].astype(BF16), w_ref[...])


def matmul(x, w_bf16):
    t, k = x.shape
    n = w_bf16.shape[1]
    tm = min(ROW_TILE, t)
    return pl.pallas_call(
        _matmul_kernel,
        out_shape=jax.ShapeDtypeStruct((t, n), F32),
        grid=(t // tm,),
        in_specs=[pl.BlockSpec((tm, k), lambda i: (i, 0)),
                  pl.BlockSpec((k, n), lambda i: (0, 0))],
        out_specs=pl.BlockSpec((tm, n), lambda i: (i, 0)),
        compiler_params=_cparams(1),
        name="matmul",
    )(x, w_bf16)


def _proj_a_kernel(x_ref, w_ref, cos_ref, sin_ref, o_ref, *, k_scale):
    x = x_ref[...].astype(BF16)
    cos2 = cos_ref[...]
    sin2 = sin_ref[...]
    for c in range(2 * RET_HEADS):
        cols = slice(c * RET_D, (c + 1) * RET_D)
        y = _dot(x, w_ref[:, cols])
        y = y * cos2 + pltpu.roll(y, RET_D // 2, 1) * sin2
        if c >= RET_HEADS:
            y = y * k_scale
        o_ref[:, cols] = y
    o_ref[:, 2 * RET_W:] = _dot(x, w_ref[:, 2 * RET_W:])


def proj_a(x, w_bf16, cos2, sin2):
    t, k = x.shape
    n = w_bf16.shape[1]
    tm = min(ROW_TILE, t)
    return pl.pallas_call(
        functools.partial(_proj_a_kernel, k_scale=RET_D ** -0.5),
        out_shape=jax.ShapeDtypeStruct((t, n), F32),
        grid=(t // tm,),
        in_specs=[pl.BlockSpec((tm, k), lambda i: (i, 0)),
                  pl.BlockSpec((k, n), lambda i: (0, 0)),
                  pl.BlockSpec((tm, RET_D), lambda i: (i, 0)),
                  pl.BlockSpec((tm, RET_D), lambda i: (i, 0))],
        out_specs=pl.BlockSpec((tm, n), lambda i: (i, 0)),
        compiler_params=_cparams(1),
        name="proj_a",
    )(x, w_bf16, cos2, sin2)


def _kv_kernel(x_ref, w_ref, bf_ref, kv_ref, logf_ref):
    x = x_ref[...].astype(BF16)
    kv_ref[...] = _dot(x, w_ref[:, :2 * FOX_W])
    z = _dot(x, w_ref[:, 2 * FOX_W:]) + bf_ref[...]
    logf_ref[...] = jnp.minimum(z, 0.0) - jnp.log(1.0 + jnp.exp(-jnp.abs(z)))


def kv_shared(x, w_pad_bf16, b_forget_pad):
    t, k = x.shape
    n = w_pad_bf16.shape[1]
    tm = min(ROW_TILE, t)
    return pl.pallas_call(
        _kv_kernel,
        out_shape=(jax.ShapeDtypeStruct((t, 2 * FOX_W), F32),
                   jax.ShapeDtypeStruct((t, LANES), F32)),
        grid=(t // tm,),
        in_specs=[pl.BlockSpec((tm, k), lambda i: (i, 0)),
                  pl.BlockSpec((k, n), lambda i: (0, 0)),
                  pl.BlockSpec((1, LANES), lambda i: (0, 0))],
        out_specs=(pl.BlockSpec((tm, 2 * FOX_W), lambda i: (i, 0)),
                   pl.BlockSpec((tm, LANES), lambda i: (i, 0))),
        compiler_params=_cparams(1),
        name="kv_shared",
    )(x, w_pad_bf16, b_forget_pad)


def _out_ln_kernel(tokp_ref, memp_ref, toks_ref, mems_ref, x_ref, w1_ref, w2_ref, g_ref, b_ref, o_ref,
                   *, alpha, prompt_tiles):
    def run(tok_ref, mem_ref):
        mixed = _dot(tok_ref[...].astype(BF16), w1_ref[...]) + _dot(mem_ref[...].astype(BF16), w2_ref[...])
        o_ref[...] = _layer_norm(alpha * x_ref[...] + mixed, g_ref[...], b_ref[...])

    @pl.when(pl.program_id(0) < prompt_tiles)
    def _():
        run(tokp_ref, memp_ref)

    @pl.when(pl.program_id(0) >= prompt_tiles)
    def _():
        run(toks_ref, mems_ref)


def out_proj_ln(tok_p, mem_p, tok_s, mem_s, x, w1_bf16, w2_bf16, g, b, alpha):
    t, d = x.shape
    tm = min(ROW_TILE, t)
    wt, wm = tok_p.shape[1], mem_p.shape[1]
    n_p = tok_p.shape[0] // tm
    n_s = tok_s.shape[0] // tm

    def prompt_map(i):
        return (jnp.minimum(i, n_p - 1), 0)

    def sample_map(i):
        return (jnp.clip(i - n_p, 0, n_s - 1), 0)

    return pl.pallas_call(
        functools.partial(_out_ln_kernel, alpha=alpha, prompt_tiles=n_p),
        out_shape=jax.ShapeDtypeStruct((t, d), F32),
        grid=(t // tm,),
        in_specs=[pl.BlockSpec((tm, wt), prompt_map),
                  pl.BlockSpec((tm, wm), prompt_map),
                  pl.BlockSpec((tm, wt), sample_map),
                  pl.BlockSpec((tm, wm), sample_map),
                  pl.BlockSpec((tm, d), lambda i: (i, 0)),
                  pl.BlockSpec((wt, d), lambda i: (0, 0)),
                  pl.BlockSpec((wm, d), lambda i: (0, 0)),
                  pl.BlockSpec((1, d), lambda i: (0, 0)),
                  pl.BlockSpec((1, d), lambda i: (0, 0))],
        out_specs=pl.BlockSpec((tm, d), lambda i: (i, 0)),
        compiler_params=_cparams(1),
        name="out_proj_ln",
    )(tok_p, mem_p, tok_s, mem_s, x, w1_bf16, w2_bf16, g, b)


def _head_norm_gate(r, g):
    mu = jnp.mean(r, axis=-1, keepdims=True)
    d = r - mu
    var = jnp.mean(d * d, axis=-1, keepdims=True)
    return d * lax.rsqrt(var + HEAD_NORM_EPS) * (g * jax.nn.sigmoid(g))


def _retention_prompt_kernel(q_ref, k_ref, v_ref, g_ref, s0_ref, dm_ref, qd_ref, kd_ref, cd_ref,
                             o_ref, sout_ref, s_scr, *, chunks):
    i = pl.program_id(1)

    @pl.when(i == 0)
    def _():
        s_scr[...] = s0_ref[0]

    for c in range(chunks):
        rows = slice(c * RET_CHUNK, (c + 1) * RET_CHUNK)
        for h in range(RET_HEADS):
            cols = slice(h * RET_D, (h + 1) * RET_D)
            q = q_ref[rows, cols]
            k = k_ref[rows, cols]
            v = v_ref[rows, cols].astype(BF16)
            s_prev = s_scr[h]
            scores = _dot_nt(q.astype(BF16), k.astype(BF16)) * dm_ref[h]
            intra = _dot(scores.astype(BF16), v)
            cross = _dot((q * qd_ref[h]).astype(BF16), s_prev.astype(BF16))
            kd = (k * kd_ref[h]).astype(BF16)
            s_scr[h] = cd_ref[h] * s_prev + _dot_tn(kd, v)
            o_ref[rows, cols] = _head_norm_gate(intra + cross, g_ref[rows, cols])

    @pl.when(i == pl.num_programs(1) - 1)
    def _():
        sout_ref[0] = s_scr[...]


def retention_prompt(h, s0, decays, batch, seq):
    chunks = math.gcd(4, seq // RET_CHUNK)
    rows = chunks * RET_CHUNK
    n_i = seq // rows
    dm, qd, kd, cd = decays
    const = pl.BlockSpec((RET_HEADS, RET_CHUNK, RET_D), lambda b, i: (0, 0, 0))
    state = pl.BlockSpec((1, RET_HEADS, RET_D, RET_D), lambda b, i: (b, 0, 0, 0))

    def col(j):
        return pl.BlockSpec((rows, RET_W), lambda b, i, j=j: (b * n_i + i, j))

    return pl.pallas_call(
        functools.partial(_retention_prompt_kernel, chunks=chunks),
        out_shape=(jax.ShapeDtypeStruct((batch * seq, RET_W), F32),
                   jax.ShapeDtypeStruct((batch, RET_HEADS, RET_D, RET_D), F32)),
        grid=(batch, n_i),
        in_specs=[col(0), col(1), col(2), col(3), state, const, const, const, const],
        out_specs=(pl.BlockSpec((rows, RET_W), lambda b, i: (b * n_i + i, 0)), state),
        scratch_shapes=[pltpu.VMEM((RET_HEADS, RET_D, RET_D), F32)],
        compiler_params=_cparams(2),
        name="retention_prompt",
    )(h, h, h, h, s0, dm, qd, kd, cd)


def _retention_sample_kernel(qt_ref, kt_ref, v_ref, g_ref, s0_ref, sprev_ref, o_ref, sout_ref, r_scr,
                             *, n_req, n_tok, gammas):
    del sprev_ref
    for j in range(n_req):
        for h in range(RET_HEADS):
            cols = slice(h * RET_D, (h + 1) * RET_D)
            s = s0_ref[j, h]
            for t in range(n_tok):
                r = j * n_tok + t
                s = gammas[h] * s + kt_ref[0, h, :, r:r + 1] * v_ref[r:r + 1, cols]
                r_scr[r:r + 1, cols] = jnp.sum(qt_ref[0, h, :, r:r + 1] * s, axis=0, keepdims=True)
            sout_ref[j, h] = s
    for h in range(RET_HEADS):
        cols = slice(h * RET_D, (h + 1) * RET_D)
        o_ref[:, cols] = _head_norm_gate(r_scr[:, cols], g_ref[:, cols])


def retention_sample(h, qt, kt, s_all, s_new_all, layer, row0, n_req_total, n_tok):
    n_req = SAMPLE_BATCH_TILE
    rows = n_req * n_tok
    steps = n_req_total // n_req
    blk0 = row0 // rows
    gammas = tuple(1.0 - 2.0 ** (-5.0 - hh) for hh in range(RET_HEADS))
    state = pl.BlockSpec((n_req, RET_HEADS, RET_D, RET_D), lambda i: (layer * steps + i, 0, 0, 0))
    tr = pl.BlockSpec((1, RET_HEADS, RET_D, rows), lambda i: (i, 0, 0, 0))
    return pl.pallas_call(
        functools.partial(_retention_sample_kernel, n_req=n_req, n_tok=n_tok, gammas=gammas),
        out_shape=(jax.ShapeDtypeStruct((n_req_total * n_tok, RET_W), F32),
                   jax.ShapeDtypeStruct(s_new_all.shape, F32)),
        grid=(steps,),
        in_specs=[tr, tr,
                  pl.BlockSpec((rows, RET_W), lambda i: (blk0 + i, 2)),
                  pl.BlockSpec((rows, RET_W), lambda i: (blk0 + i, 3)),
                  state,
                  pl.BlockSpec(memory_space=pl.ANY)],
        out_specs=(pl.BlockSpec((rows, RET_W), lambda i: (i, 0)), state),
        scratch_shapes=[pltpu.VMEM((rows, RET_W), F32)],
        input_output_aliases={5: 1},
        compiler_params=_cparams(1),
        name="retention_sample",
    )(qt, kt, h, h, s_all, s_new_all)


def _mem_attn_kernel(q_ref, mk_ref, mv_ref, o_ref, *, banks, rows_per_bank):
    q = q_ref[...] * (MEM_HD ** -0.5)
    n_rows = q.shape[0]
    out = None
    for j in range(banks):
        mk = mk_ref[j].astype(BF16)
        mv = mv_ref[j].astype(BF16)
        heads = []
        for h in range(MEM_HEADS):
            cols = slice(h * MEM_HD, (h + 1) * MEM_HD)
            s = _dot_nt(q[:, cols].astype(BF16), mk[:, cols])
            p = jnp.exp(s - jnp.max(s, axis=-1, keepdims=True))
            p = p / jnp.sum(p, axis=-1, keepdims=True)
            heads.append(_dot(p.astype(BF16), mv[:, cols]))
        o = jnp.concatenate(heads, axis=1)
        if out is None:
            out = o
        else:
            row = lax.broadcasted_iota(jnp.int32, (n_rows, MEM_W), 0)
            out = jnp.where(row >= j * rows_per_bank, o, out)
    o_ref[...] = out


def mem_attention(h, q_col_block, mk, mv, bank0, *, row_block0, n_steps, rows, banks, steps_per_bank):
    mem_len = mk.shape[1]
    if banks == 1:
        bank_map = lambda i: (bank0 + i // steps_per_bank, 0, 0)
    else:
        bank_map = lambda i: (bank0 // banks + i, 0, 0)
    return pl.pallas_call(
        functools.partial(_mem_attn_kernel, banks=banks, rows_per_bank=rows // banks),
        out_shape=jax.ShapeDtypeStruct((n_steps * rows, MEM_W), F32),
        grid=(n_steps,),
        in_specs=[pl.BlockSpec((rows, MEM_W), lambda i: (row_block0 + i, q_col_block)),
                  pl.BlockSpec((banks, mem_len, MEM_W), bank_map),
                  pl.BlockSpec((banks, mem_len, MEM_W), bank_map)],
        out_specs=pl.BlockSpec((rows, MEM_W), lambda i: (i, 0)),
        compiler_params=_cparams(1),
        name="mem_attention",
    )(h, mk, mv)


def _cumsum_kernel(x_ref, o_ref, carry):
    @pl.when(pl.program_id(1) == 0)
    def _():
        carry[...] = jnp.zeros_like(carry)

    n = x_ref.shape[0]
    row = lax.broadcasted_iota(jnp.int32, (n, n), 0)
    col = lax.broadcasted_iota(jnp.int32, (n, n), 1)
    tri = (row >= col).astype(F32)
    c = jnp.dot(tri, x_ref[...], precision=HIGHEST, preferred_element_type=F32) + carry[...]
    o_ref[...] = c
    carry[...] = c[n - 1:n, :]


def cumsum_rows(x, batch, seq):
    tm = min(ROW_TILE, seq)
    n_i = seq // tm
    return pl.pallas_call(
        _cumsum_kernel,
        out_shape=jax.ShapeDtypeStruct((batch * seq, x.shape[1]), F32),
        grid=(batch, n_i),
        in_specs=[pl.BlockSpec((tm, x.shape[1]), lambda b, i: (b * n_i + i, 0))],
        out_specs=pl.BlockSpec((tm, x.shape[1]), lambda b, i: (b * n_i + i, 0)),
        scratch_shapes=[pltpu.VMEM((1, x.shape[1]), F32)],
        compiler_params=_cparams(2),
        name="cumsum_rows",
    )(x)


BIAS_COLS = 12


def _split3(x):
    hi = x.astype(BF16).astype(F32)
    mid = (x - hi).astype(BF16).astype(F32)
    lo = x - hi - mid
    return hi, mid, lo


def _place(lane, base, parts):
    out = jnp.zeros(lane.shape, F32)
    for idx, part in enumerate(parts):
        out = jnp.where(lane == base + idx, part, out)
    return out


def _fox_prep_kernel(k_ref, v_ref, c_ref, ka_ref, vt_ref):
    p = pl.program_id(0)
    rows = k_ref.shape[0]
    lane = lax.broadcasted_iota(jnp.int32, (rows, LANES), 1)
    c2 = c_ref[...]
    bias = jnp.zeros((rows, LANES), F32)
    for e in range(2):
        ck = jnp.sum(jnp.where(lane == 2 * p + e, c2, 0.0), axis=1, keepdims=True)
        base = e * (BIAS_COLS // 2)
        ones = jnp.where((lane >= base) & (lane < base + 3), 1.0, 0.0)
        bias = bias + ones + _place(lane, base + 3, [-part for part in _split3(ck)])
    ka_ref[0] = jnp.concatenate([k_ref[...], bias], axis=1).astype(BF16)
    vt_ref[0] = v_ref[...].T.astype(BF16)


def fox_prep(kv, c, n_rows):
    pairs = FOX_HEADS // 2
    tm = min(FOX_BLOCK, n_rows)
    return pl.pallas_call(
        _fox_prep_kernel,
        out_shape=(jax.ShapeDtypeStruct((pairs, n_rows, 2 * LANES), BF16),
                   jax.ShapeDtypeStruct((pairs, LANES, n_rows), BF16)),
        grid=(pairs, n_rows // tm),
        in_specs=[pl.BlockSpec((tm, LANES), lambda p, i: (i, p)),
                  pl.BlockSpec((tm, LANES), lambda p, i: (i, pairs + p)),
                  pl.BlockSpec((tm, LANES), lambda p, i: (i, 0))],
        out_specs=(pl.BlockSpec((1, tm, 2 * LANES), lambda p, i: (p, i, 0)),
                   pl.BlockSpec((1, LANES, tm), lambda p, i: (p, 0, i))),
        compiler_params=_cparams(2),
        name="fox_prep",
    )(kv, kv, c)


def _fox_prompt_kernel(q_ref, c_ref, ka_ref, vt_ref, o_ref, acc_a, acc_b, *, blk):
    p = pl.program_id(1)
    i = pl.program_id(2)
    lane = lax.broadcasted_iota(jnp.int32, (blk, LANES), 1)
    q2 = q_ref[...] * (FOX_HD ** -0.5)
    c2 = c_ref[...]
    qts = []
    for e in range(2):
        head_lanes = (lane >= e * FOX_HD) & (lane < (e + 1) * FOX_HD)
        cq = jnp.sum(jnp.where(lane == 2 * p + e, c2, 0.0), axis=1, keepdims=True)
        base = e * (BIAS_COLS // 2)
        ones = jnp.where((lane >= base + 3) & (lane < base + 6), 1.0, 0.0)
        q_aug = jnp.concatenate([jnp.where(head_lanes, q2, 0.0), ones + _place(lane, base, _split3(cq))], axis=1)
        qts.append(q_aug.T.astype(BF16))
    accs = (acc_a, acc_b)
    acc_a[...] = jnp.zeros(acc_a.shape, F32)
    acc_b[...] = jnp.zeros(acc_b.shape, F32)

    def block(j, carry, masked):
        start = pl.multiple_of(j * blk, blk)
        kb = ka_ref[0, pl.ds(start, blk), :]
        vt = vt_ref[0, :, pl.ds(start, blk)]
        new = []
        for e in range(2):
            m_old, l_old = carry[2 * e], carry[2 * e + 1]
            s = _dot(kb, qts[e])
            if masked:
                key = lax.broadcasted_iota(jnp.int32, (blk, blk), 0)
                qry = lax.broadcasted_iota(jnp.int32, (blk, blk), 1)
                s = jnp.where(key <= qry, s, -jnp.inf)
            m_new = jnp.maximum(m_old, jnp.max(s, axis=0, keepdims=True))
            a = jnp.exp(m_old - m_new)
            pe = jnp.exp(s - m_new)
            new += [m_new, a * l_old + jnp.sum(pe, axis=0, keepdims=True)]
            acc = accs[e]
            acc[...] = a * acc[...] + _dot(vt[e * FOX_HD:(e + 1) * FOX_HD, :], pe.astype(BF16))
        return tuple(new)

    init = (jnp.full((1, blk), -jnp.inf, F32), jnp.zeros((1, blk), F32)) * 2
    carry = lax.fori_loop(0, i, lambda j, c: block(j, c, False), init)
    carry = block(i, carry, True)
    out_t = jnp.concatenate([acc_a[...] / carry[1], acc_b[...] / carry[3]], axis=0)
    o_ref[...] = out_t.T


def fox_prompt(h, c, k_aug, v_t, batch, seq):
    blk = min(FOX_BLOCK, seq)
    n_q = seq // blk
    pairs = FOX_HEADS // 2
    return pl.pallas_call(
        functools.partial(_fox_prompt_kernel, blk=blk),
        out_shape=jax.ShapeDtypeStruct((batch * seq, FOX_W), F32),
        grid=(batch, pairs, n_q),
        in_specs=[pl.BlockSpec((blk, LANES), lambda b, p, i: (b * n_q + i, p)),
                  pl.BlockSpec((blk, LANES), lambda b, p, i: (b * n_q + i, 0)),
                  pl.BlockSpec((1, seq, 2 * LANES), lambda b, p, i: (p, b, 0)),
                  pl.BlockSpec((1, LANES, seq), lambda b, p, i: (p, 0, b))],
        out_specs=pl.BlockSpec((blk, LANES), lambda b, p, i: (b * n_q + i, p)),
        scratch_shapes=[pltpu.VMEM((FOX_HD, blk), F32), pltpu.VMEM((FOX_HD, blk), F32)],
        compiler_params=_cparams(3),
        name="fox_prompt",
    )(h, c, k_aug, v_t)


FOX_COLS = 16


def _fox_sample_kernel(pt_ref, q_ref, kvn_ref, lfn_ref, *refs, n_pages, page, n_tok):
    k_refs = refs[:n_pages]
    v_refs = refs[n_pages:2 * n_pages]
    lf_refs = refs[2 * n_pages:3 * n_pages]
    o_ref = refs[3 * n_pages]
    s_scr = refs[3 * n_pages + 1]
    n_past = n_pages * page
    reqs = SUBLANES // n_tok
    which = pl.program_id(0) % reqs

    def pick(block):
        out = block[0:n_tok, :]
        for w in range(1, reqs):
            out = jnp.where(which == w, block[w * n_tok:(w + 1) * n_tok, :], out)
        return out

    q = pick(q_ref[...])[:, :FOX_W] * (FOX_HD ** -0.5)
    kvn = pick(kvn_ref[...])
    lfn = pick(lfn_ref[...])[:, :FOX_HEADS]
    head_of_lane = lax.broadcasted_iota(jnp.int32, (FOX_COLS, FOX_W), 1) // FOX_HD
    head_of_row = lax.broadcasted_iota(jnp.int32, (FOX_COLS, FOX_W), 0)
    q_rows = [jnp.where(head_of_lane == head_of_row, jnp.broadcast_to(q[t:t + 1, :], (FOX_COLS, FOX_W)), 0.0)
              for t in range(n_tok)]
    pad_rows = LANES - n_tok * FOX_COLS
    q_bd = jnp.concatenate(q_rows + [jnp.zeros((pad_rows, FOX_W), F32)], axis=0).astype(BF16)

    e_row = lax.broadcasted_iota(jnp.int32, (FOX_HEADS, LANES), 0)
    e_col = lax.broadcasted_iota(jnp.int32, (FOX_HEADS, LANES), 1)
    expand = ((e_col % FOX_COLS == e_row) & (e_col < n_tok * FOX_COLS)).astype(F32)
    t_row = lax.broadcasted_iota(jnp.int32, (page, page), 0)
    t_col = lax.broadcasted_iota(jnp.int32, (page, page), 1)
    tri = (t_row >= t_col).astype(F32)

    carry = jnp.zeros((1, FOX_HEADS), F32)
    for pg in range(n_pages):
        c_in = jnp.dot(tri, lf_refs[pg][0], precision=HIGHEST, preferred_element_type=F32) + carry
        carry = c_in[page - 1:page, :]
        ck = jnp.dot(c_in, expand, precision=HIGHEST, preferred_element_type=F32)
        s_scr[pg * page:(pg + 1) * page, :] = _dot_nt(k_refs[pg][0], q_bd) - ck

    c_rows = []
    for t in range(n_tok):
        carry = carry + lfn[t:t + 1, :]
        c_rows.append(carry)
    c_new = jnp.dot(jnp.concatenate(c_rows, axis=0), expand, precision=HIGHEST,
                    preferred_element_type=F32)
    n_row = lax.broadcasted_iota(jnp.int32, (n_tok, LANES), 0)
    n_col = lax.broadcasted_iota(jnp.int32, (n_tok, LANES), 1)
    tok_of_col = n_col // FOX_COLS
    cq = jnp.sum(jnp.where(tok_of_col == n_row, c_new, 0.0), axis=0, keepdims=True)
    s_new = _dot_nt(kvn[:, :FOX_W].astype(BF16), q_bd) - c_new
    s_new = jnp.where(n_row <= tok_of_col, s_new, -jnp.inf)
    s_scr[n_past:n_past + n_tok, :] = s_new
    s_scr[n_past + n_tok:, :] = jnp.full((s_scr.shape[0] - n_past - n_tok, LANES), -jnp.inf, F32)

    s_all = s_scr[...] + cq
    pr = jnp.exp(s_all - jnp.max(s_all, axis=0, keepdims=True))
    s_scr[...] = pr / jnp.sum(pr, axis=0, keepdims=True)

    acc = jnp.zeros((LANES, FOX_W), F32)
    for pg in range(n_pages):
        pt = s_scr[pg * page:(pg + 1) * page, :].T.astype(BF16)
        acc = acc + _dot(pt, v_refs[pg][0])
    n_pad = s_scr.shape[0] - n_past
    v_new = jnp.concatenate([kvn[:, FOX_W:], jnp.zeros((n_pad - n_tok, FOX_W), F32)], axis=0).astype(BF16)
    acc = acc + _dot_tn(s_scr[n_past:, :].astype(BF16), v_new)

    a_row = lax.broadcasted_iota(jnp.int32, (LANES, FOX_W), 0)
    a_col = lax.broadcasted_iota(jnp.int32, (LANES, FOX_W), 1)
    acc = jnp.where(a_col // FOX_HD == a_row % FOX_COLS, acc, 0.0)
    out = acc.reshape(LANES // FOX_COLS, FOX_COLS, FOX_W).sum(axis=1)[:n_tok, :]
    for w in range(reqs):
        @pl.when(which == w)
        def _(w=w):
            o_ref[w * n_tok:(w + 1) * n_tok, :] = out


def fox_sample(h, kv, logf, cache_k, cache_v, cache_logf, page_table, row0, n_req, n_tok):
    n_pages = page_table.shape[1]
    page = cache_k.shape[1]
    reqs = SUBLANES // n_tok
    blk0 = row0 // SUBLANES
    pt_flat = page_table.reshape(-1)

    def page_spec(width, pg):
        return pl.BlockSpec((1, page, width), lambda r, pt, pg=pg: (pt[r * n_pages + pg], 0, 0))

    def new_spec(width):
        return pl.BlockSpec((SUBLANES, width), lambda r, pt: (blk0 + r // reqs, 0))

    in_specs = ([new_spec(h.shape[1]), new_spec(kv.shape[1]), new_spec(logf.shape[1])]
                + [page_spec(FOX_W, pg) for pg in range(n_pages)]
                + [page_spec(FOX_W, pg) for pg in range(n_pages)]
                + [page_spec(FOX_HEADS, pg) for pg in range(n_pages)])
    grid_spec = pltpu.PrefetchScalarGridSpec(
        num_scalar_prefetch=1,
        grid=(n_req,),
        in_specs=in_specs,
        out_specs=pl.BlockSpec((SUBLANES, FOX_W), lambda r, pt: (r // reqs, 0)),
        scratch_shapes=[pltpu.VMEM((n_pages * page + SUBLANES, LANES), F32)],
    )
    return pl.pallas_call(
        functools.partial(_fox_sample_kernel, n_pages=n_pages, page=page, n_tok=n_tok),
        out_shape=jax.ShapeDtypeStruct((n_req * n_tok, FOX_W), F32),
        grid_spec=grid_spec,
        compiler_params=_cparams(1),
        name="fox_sample",
    )(pt_flat, h, kv, logf, *([cache_k] * n_pages), *([cache_v] * n_pages), *([cache_logf] * n_pages))


def _router_kernel(x_ref, w_ref, b_ref, o_ref, cnt_ref, carry):
    @pl.when(pl.program_id(0) == 0)
    def _():
        carry[...] = jnp.zeros_like(carry)

    logits = jnp.dot(x_ref[...], w_ref[...], precision=HIGHEST, preferred_element_type=F32) + b_ref[...]
    lane = lax.broadcasted_iota(jnp.int32, logits.shape, 1)
    big = jnp.int32(LANES)
    neg = -jnp.inf
    gl = jnp.where(lane < N_GROUPS, logits, neg)
    g_max = jnp.max(gl, axis=-1, keepdims=True)
    g_idx = jnp.min(jnp.where(gl == g_max, lane, big), axis=-1, keepdims=True)
    g_w = 1.0 / jnp.sum(jnp.exp(gl - g_max), axis=-1, keepdims=True)
    in_group = (lane >= N_GROUPS) & (lane < N_GROUPS + N_EXPERTS) & (((lane - N_GROUPS) >> 3) == g_idx)
    el = jnp.where(in_group, logits, neg)
    v1 = jnp.max(el, axis=-1, keepdims=True)
    i1 = jnp.min(jnp.where(el == v1, lane, big), axis=-1, keepdims=True)
    el2 = jnp.where(lane == i1, neg, el)
    v2 = jnp.max(el2, axis=-1, keepdims=True)
    i2 = jnp.min(jnp.where(el2 == v2, lane, big), axis=-1, keepdims=True)
    e2 = jnp.exp(v2 - v1)
    w1 = g_w / (1.0 + e2)
    w2 = g_w * e2 / (1.0 + e2)
    tm = logits.shape[0]
    e1 = i1 - N_GROUPS
    e2 = i2 - N_GROUPS
    oh1 = jnp.where(lane == e1, 1.0, 0.0)
    oh2 = jnp.where(lane == e2, 1.0, 0.0)
    row = lax.broadcasted_iota(jnp.int32, (tm, tm), 0)
    col = lax.broadcasted_iota(jnp.int32, (tm, tm), 1)
    before = jnp.where(col < row, 1.0, 0.0).astype(BF16)
    tot1 = jnp.sum(oh1, axis=0, keepdims=True)
    base = carry[...]
    r1 = jnp.sum(oh1 * (base + _dot(before, oh1.astype(BF16))), axis=-1, keepdims=True)
    r2 = jnp.sum(oh2 * (base + tot1 + _dot(before, oh2.astype(BF16))), axis=-1, keepdims=True)
    carry[...] = base + tot1 + jnp.sum(oh2, axis=0, keepdims=True)
    cnt_ref[...] = carry[...]
    out = jnp.zeros(logits.shape, F32)
    for idx, val in enumerate((e1.astype(F32), e2.astype(F32), w1, w2, r1, r2)):
        out = jnp.where(lane == idx, val, out)
    o_ref[...] = out


def router(x, w_pad, b_pad):
    t, d = x.shape
    tm = min(ROW_TILE, t)
    return pl.pallas_call(
        _router_kernel,
        out_shape=(jax.ShapeDtypeStruct((t, LANES), F32), jax.ShapeDtypeStruct((1, LANES), F32)),
        grid=(t // tm,),
        in_specs=[pl.BlockSpec((tm, d), lambda i: (i, 0)),
                  pl.BlockSpec((d, LANES), lambda i: (0, 0)),
                  pl.BlockSpec((1, LANES), lambda i: (0, 0))],
        out_specs=(pl.BlockSpec((tm, LANES), lambda i: (i, 0)), pl.BlockSpec((1, LANES), lambda i: (0, 0))),
        scratch_shapes=[pltpu.VMEM((1, LANES), F32)],
        compiler_params=_cparams(1),
        name="router",
    )(x, w_pad, b_pad)


def _pack_rows(y):
    n = y.shape[1] // 2
    bits = lax.bitcast_convert_type(y.astype(BF16).astype(F32), jnp.uint32)
    return bits[:, :n] | (bits[:, n:] >> 16)


def _unpack_rows(w):
    return (lax.bitcast_convert_type(w & jnp.uint32(0xFFFF0000), F32),
            lax.bitcast_convert_type(w << 16, F32))


def _row_copy(src_ref, src_row, dst_ref, dst_row, sem):
    return pltpu.make_async_copy(src_ref.at[pl.ds(src_row, 1)], dst_ref.at[pl.ds(dst_row, 1)], sem)


def _drain_rows(src_ref, dst_ref, sem, n_rows):
    def drain(r, carry):
        _row_copy(src_ref, 0, dst_ref, 0, sem).wait()
        return carry

    lax.fori_loop(0, n_rows, drain, 0, unroll=8)


def _dispatch_kernel(pos_ref, x_ref, xs_in_ref, xs_ref, buf, sems, *, tm):
    del xs_in_ref
    i = pl.program_id(0)
    n = pl.num_programs(0)
    for slot in range(2):
        @pl.when(i % 2 == slot)
        def _(slot=slot):
            @pl.when(i >= 2)
            def _():
                _drain_rows(buf.at[slot], xs_ref, sems.at[slot], 2 * tm)

            buf[slot] = _pack_rows(x_ref[...])

            def issue(r, carry):
                for k in range(2):
                    _row_copy(buf.at[slot], r, xs_ref, pos_ref[0, 0, k * tm + r], sems.at[slot]).start(priority=k)
                return carry

            lax.fori_loop(0, tm, issue, 0, unroll=8)

            @pl.when(i == n - 1)
            def _():
                _drain_rows(buf.at[slot], xs_ref, sems.at[slot], 2 * tm)

                @pl.when(n >= 2)
                def _():
                    _drain_rows(buf.at[1 - slot], xs_ref, sems.at[1 - slot], 2 * tm)


def moe_dispatch(x, pos3, xs_init):
    t, d = x.shape
    tm = MOE_TILE
    return pl.pallas_call(
        functools.partial(_dispatch_kernel, tm=tm),
        out_shape=jax.ShapeDtypeStruct(xs_init.shape, jnp.uint32),
        grid=(t // tm,),
        in_specs=[pl.BlockSpec((1, 1, 2 * tm), lambda i: (i, 0, 0), memory_space=pltpu.SMEM),
                  pl.BlockSpec((tm, d), lambda i: (i, 0)),
                  pl.BlockSpec(memory_space=pl.ANY)],
        out_specs=pl.BlockSpec(memory_space=pl.ANY),
        scratch_shapes=[pltpu.VMEM((2, tm, d // 2), jnp.uint32), pltpu.SemaphoreType.DMA((2,))],
        input_output_aliases={2: 0},
        compiler_params=_cparams(1),
        name="moe_dispatch",
    )(pos3, x, xs_init)


def _expert_kernel(te_ref, na_ref, x_ref, wu_ref, wd_ref, o_ref, wu_bf, wd_bf):
    i = pl.program_id(0)
    active = i < na_ref[0]
    changed = jnp.logical_or(i == 0, te_ref[i] != te_ref[jnp.maximum(i - 1, 0)])

    @pl.when(jnp.logical_and(active, changed))
    def _():
        wu_bf[...] = wu_ref[0].astype(BF16)
        wd_bf[...] = wd_ref[0].astype(BF16)

    @pl.when(active)
    def _():
        half = wu_bf.shape[0] // 2
        xa, xb = _unpack_rows(x_ref[...])
        up = _dot(xa.astype(BF16), wu_bf[:half, :]) + _dot(xb.astype(BF16), wu_bf[half:, :])
        hid = jax.nn.silu(up[:, :D_EXPERT]) * up[:, D_EXPERT:]
        o_ref[...] = _pack_rows(_dot(hid.astype(BF16), wd_bf[...]))

    @pl.when(jnp.logical_not(active))
    def _():
        o_ref[...] = jnp.zeros_like(o_ref)


def moe_experts(xs, w_up, w_down, tile_expert, n_active):
    r = xs.shape[0]
    d = w_up.shape[1]
    tm = MOE_TILE
    n_tiles = r // tm

    def row_map(i, te, na):
        return (jnp.minimum(i, na[0] - 1), 0)

    grid_spec = pltpu.PrefetchScalarGridSpec(
        num_scalar_prefetch=2,
        grid=(n_tiles,),
        in_specs=[pl.BlockSpec((tm, d // 2), row_map),
                  pl.BlockSpec((1, d, 2 * D_EXPERT), lambda i, te, na: (te[i], 0, 0)),
                  pl.BlockSpec((1, D_EXPERT, d), lambda i, te, na: (te[i], 0, 0))],
        out_specs=pl.BlockSpec((tm, d // 2), lambda i, te, na: (i, 0)),
        scratch_shapes=[pltpu.VMEM((d, 2 * D_EXPERT), BF16), pltpu.VMEM((D_EXPERT, d), BF16)],
    )
    return pl.pallas_call(
        _expert_kernel,
        out_shape=jax.ShapeDtypeStruct((r, d // 2), jnp.uint32),
        grid_spec=grid_spec,
        compiler_params=_cparams(1),
        name="moe_experts",
    )(tile_expert, n_active, xs, w_up, w_down)


def _combine_kernel(pos_ref, posn_ref, x_ref, rt_ref, g_ref, b_ref, ys_ref, o_ref, buf, sems, *, tm, alpha):
    i = pl.program_id(0)
    n = pl.num_programs(0)

    def gather(p_ref, slot):
        def issue(r, carry):
            for k in range(2):
                _row_copy(ys_ref, p_ref[0, 0, k * tm + r], buf.at[slot, k], r, sems.at[slot]).start(priority=k)
            return carry

        lax.fori_loop(0, tm, issue, 0, unroll=8)

    @pl.when(i == 0)
    def _():
        gather(pos_ref, 0)

    for slot in range(2):
        @pl.when(i % 2 == slot)
        def _(slot=slot):
            @pl.when(i + 1 < n)
            def _():
                gather(posn_ref, 1 - slot)

            _drain_rows(ys_ref, buf.at[slot, 0], sems.at[slot], 2 * tm)
            rt = rt_ref[...]
            a0, b0 = _unpack_rows(buf[slot, 0])
            a1, b1 = _unpack_rows(buf[slot, 1])
            w1 = rt[:, 2:3]
            w2 = rt[:, 3:4]
            ffn = jnp.concatenate([w1 * a0 + w2 * a1, w1 * b0 + w2 * b1], axis=1)
            o_ref[...] = _layer_norm(alpha * x_ref[...] + ffn, g_ref[...], b_ref[...])


def moe_combine(x, routing, ys, pos3, g, b, alpha):
    t, d = x.shape
    tm = MOE_TILE
    n = t // tm
    return pl.pallas_call(
        functools.partial(_combine_kernel, tm=tm, alpha=alpha),
        out_shape=jax.ShapeDtypeStruct((t, d), F32),
        grid=(n,),
        in_specs=[pl.BlockSpec((1, 1, 2 * tm), lambda i: (i, 0, 0), memory_space=pltpu.SMEM),
                  pl.BlockSpec((1, 1, 2 * tm), lambda i: (jnp.minimum(i + 1, n - 1), 0, 0),
                               memory_space=pltpu.SMEM),
                  pl.BlockSpec((tm, d), lambda i: (i, 0)),
                  pl.BlockSpec((tm, LANES), lambda i: (i, 0)),
                  pl.BlockSpec((1, d), lambda i: (0, 0)),
                  pl.BlockSpec((1, d), lambda i: (0, 0)),
                  pl.BlockSpec(memory_space=pl.ANY)],
        out_specs=pl.BlockSpec((tm, d), lambda i: (i, 0)),
        scratch_shapes=[pltpu.VMEM((2, 2, tm, d // 2), jnp.uint32), pltpu.SemaphoreType.DMA((2,))],
        compiler_params=_cparams(1),
        name="moe_combine",
    )(pos3, pos3, x, routing, g, b, ys)


def moe_plan(routing, counts, n_tiles, expert0):
    t = routing.shape[0]
    tm = MOE_TILE
    experts = jnp.arange(N_EXPERTS, dtype=jnp.int32)
    cnt = counts[0, :N_EXPERTS].astype(jnp.int32)
    tiles_e = (cnt + tm - 1) // tm
    tile_end = jnp.cumsum(tiles_e)
    offs = (tile_end - tiles_e) * tm
    ids = routing[:, 0:2].astype(jnp.int32)
    rank = routing[:, 4:6].astype(jnp.int32)
    pos = jnp.sum(jnp.where(ids[:, :, None] == experts, offs, 0), axis=-1) + rank
    n_active = tile_end[-1:]
    tile_ids = jnp.minimum(jnp.arange(n_tiles, dtype=jnp.int32), n_active[0] - 1)
    tile_expert = jnp.minimum(jnp.sum((tile_ids[:, None] >= tile_end[None, :]).astype(jnp.int32), axis=1),
                              N_EXPERTS - 1) + expert0
    pos3 = pos.reshape(t // tm, tm, 2).transpose(0, 2, 1).reshape(t // tm, 1, 2 * tm)
    return pos3, tile_expert, n_active


def hierarchical_moe_ln(x, w_router, b_router, w_up, w_down, expert0, g, b, alpha):
    t, d = x.shape
    n_tiles = (2 * t) // MOE_TILE + N_EXPERTS
    routing, counts = router(x, w_router, b_router)
    pos3, tile_expert, n_active = moe_plan(routing, counts, n_tiles, expert0)
    xs = moe_dispatch(x, pos3, jnp.zeros((n_tiles * MOE_TILE, d // 2), jnp.uint32))
    ys = moe_experts(xs, w_up, w_down, tile_expert, n_active)
    return moe_combine(x, routing, ys, pos3, g, b, alpha)


def _rotary_tables(pos):
    half = RET_D // 2
    inv_freq = ROPE_BASE ** (-jnp.arange(half, dtype=F32) / half)
    ang = pos.astype(F32)[:, None] * inv_freq[None, :]
    cos, sin = jnp.cos(ang), jnp.sin(ang)
    return jnp.concatenate([cos, cos], axis=1), jnp.concatenate([-sin, sin], axis=1)


def _retention_decays():
    log_gamma = jnp.log(1.0 - 2.0 ** (-5.0 - jnp.arange(RET_HEADS, dtype=F32)))
    c = RET_CHUNK
    i = jnp.arange(c, dtype=F32)
    diff = i[:, None] - i[None, :]
    dm = jnp.where(diff >= 0, jnp.exp(log_gamma[:, None, None] * jnp.maximum(diff, 0.0)[None]), 0.0)
    shape = (RET_HEADS, c, RET_D)
    qd = jnp.broadcast_to(jnp.exp(log_gamma[:, None] * (i + 1.0)[None, :])[:, :, None], shape)
    kd = jnp.broadcast_to(jnp.exp(log_gamma[:, None] * (c - 1 - i)[None, :])[:, :, None], shape)
    cd = jnp.broadcast_to(jnp.exp(log_gamma * c)[:, None, None], shape)
    return dm, qd, kd, cd


def kernel(x_prompt, x_sample, mem_prompt, state_ret, cache_fox_k, cache_fox_v, cache_fox_logf, cache_mem_k, cache_mem_v, page_table, w_in_a, w_in_b, w_kv_shared, b_forget, w_o, w_mem_kv, ln1_g, ln1_b, ln2_g, ln2_b, w_group, b_group, w_route, b_route, w_up, w_down):
    bp, seq, d = x_prompt.shape
    bs, n_tok, _ = x_sample.shape
    depth = w_o.shape[0]
    n_a = w_in_a.shape[0]
    mem_len = mem_prompt.shape[1]
    n_pages, page = page_table.shape[1], cache_fox_k.shape[1]
    past = n_pages * page
    tp, ts = bp * seq, bs * n_tok
    alpha = (2 * depth) ** 0.25
    assert w_in_a.shape[2] == 4 * RET_W + MEM_W and w_in_b.shape[2] == FOX_W + MEM_W
    assert w_kv_shared.shape[1] == 2 * FOX_W + FOX_HEADS and w_up.shape[1:3] == (N_GROUPS, EXPERTS_PER_GROUP)
    assert seq % RET_CHUNK == 0 and SUBLANES % n_tok == 0 and tp % ROW_TILE == 0 and ts % ROW_TILE == 0

    x = jnp.concatenate([x_prompt.reshape(tp, d), x_sample.reshape(ts, d)], axis=0)
    t_all = tp + ts

    w_mem_all = jnp.transpose(w_mem_kv, (1, 0, 2)).reshape(d, depth * 2 * MEM_W).astype(BF16)
    mem_kv = matmul(mem_prompt.reshape(bp * mem_len, d), w_mem_all)
    mem_kv5 = mem_kv.reshape(bp, mem_len, depth, 2, MEM_HEADS, MEM_HD)
    mem_k_prompt = jnp.transpose(mem_kv5[:, :, :, 0], (2, 0, 1, 3, 4))
    mem_v_prompt = jnp.transpose(mem_kv5[:, :, :, 1], (2, 0, 1, 3, 4))
    mem_kv_banks = mem_kv.reshape(bp, mem_len, depth * 2 * MEM_W)
    cache_mk = cache_mem_k.reshape(depth * bs, mem_len, MEM_W)
    cache_mv = cache_mem_v.reshape(depth * bs, mem_len, MEM_W)

    pos_all = jnp.concatenate([jnp.tile(jnp.arange(seq), bp), jnp.tile(past + jnp.arange(n_tok), bs)])
    cos2, sin2 = _rotary_tables(pos_all)
    decays = _retention_decays()
    zero_state = jnp.zeros((bp, RET_HEADS, RET_D, RET_D), F32)

    w_router = jnp.concatenate(
        [w_group, jnp.transpose(w_route, (0, 2, 1, 3)).reshape(depth, d, N_EXPERTS),
         jnp.zeros((depth, d, LANES - N_GROUPS - N_EXPERTS), F32)], axis=2)
    b_router = jnp.concatenate(
        [b_group, b_route.reshape(depth, N_EXPERTS), jnp.zeros((depth, LANES - N_GROUPS - N_EXPERTS), F32)],
        axis=1)[:, None, :]
    w_up_e = w_up.reshape(depth * N_EXPERTS, d, 2 * D_EXPERT)
    w_down_e = w_down.reshape(depth * N_EXPERTS, D_EXPERT, d)

    sample_rows = SAMPLE_BATCH_TILE * n_tok
    ret_prompt = []
    state_all = state_ret.reshape(n_a * bs, RET_HEADS, RET_D, RET_D)
    ret_sample = jnp.zeros(state_all.shape, F32)
    kv = logf = None
    for l in range(depth):
        if l < n_a:
            h = proj_a(x, w_in_a[l].astype(BF16), cos2, sin2)
            tok_p, s_p = retention_prompt(h, zero_state, decays, bp, seq)
            hs = h[tp:, :2 * RET_W].reshape(bs // SAMPLE_BATCH_TILE, sample_rows, 2, RET_HEADS, RET_D)
            hs = jnp.transpose(hs, (2, 0, 3, 4, 1))
            tok_s, ret_sample = retention_sample(h, hs[0], hs[1], state_all, ret_sample, l, tp, bs, n_tok)
            ret_prompt.append(s_p)
            mq_block = (4 * RET_W) // MEM_W
        else:
            if l == n_a:
                w_kv_pad = jnp.concatenate(
                    [w_kv_shared, jnp.zeros((d, LANES - FOX_HEADS), F32)], axis=1).astype(BF16)
                bf_pad = jnp.concatenate([b_forget, jnp.zeros((LANES - FOX_HEADS,), F32)])[None, :]
                kv, logf = kv_shared(x, w_kv_pad, bf_pad)
                c_prompt = cumsum_rows(logf, bp, seq)
                k_aug, v_t = fox_prep(kv, c_prompt, tp)
                n_phys = cache_fox_k.shape[0]
                cache_k16 = cache_fox_k.reshape(n_phys, page, FOX_W).astype(BF16)
                cache_v16 = cache_fox_v.reshape(n_phys, page, FOX_W).astype(BF16)
            h = matmul(x, w_in_b[l - n_a].astype(BF16))
            tok_p = fox_prompt(h, c_prompt, k_aug, v_t, bp, seq)
            tok_s = fox_sample(h, kv, logf, cache_k16, cache_v16, cache_fox_logf, page_table, tp, bs, n_tok)
            mq_block = FOX_W // MEM_W
        mem_p = mem_attention(h, mq_block, mem_kv_banks[:, :, l * 2 * MEM_W:(l * 2 + 1) * MEM_W],
                              mem_kv_banks[:, :, (l * 2 + 1) * MEM_W:(l * 2 + 2) * MEM_W], 0,
                              row_block0=0, n_steps=tp // ROW_TILE, rows=ROW_TILE, banks=1,
                              steps_per_bank=seq // ROW_TILE)
        mem_s = mem_attention(h, mq_block, cache_mk, cache_mv, l * bs,
                              row_block0=tp // sample_rows, n_steps=bs // SAMPLE_BATCH_TILE,
                              rows=sample_rows, banks=SAMPLE_BATCH_TILE, steps_per_bank=1)
        wt = tok_p.shape[1]
        x = out_proj_ln(tok_p, mem_p, tok_s, mem_s, x, w_o[l, :wt].astype(BF16), w_o[l, wt:].astype(BF16),
                        ln1_g[l][None, :], ln1_b[l][None, :], alpha)
        x = hierarchical_moe_ln(x, w_router[l], b_router[l], w_up_e, w_down_e, l * N_EXPERTS,
                                ln2_g[l][None, :], ln2_b[l][None, :], alpha)

    y_prompt = x[:tp].reshape(bp, seq, d)
    y_sample = x[tp:].reshape(bs, n_tok, d)
    fox_k = kv[:, :FOX_W].reshape(t_all, FOX_HEADS, FOX_HD)
    fox_v = kv[:, FOX_W:].reshape(t_all, FOX_HEADS, FOX_HD)
    fox_lf = logf[:, :FOX_HEADS]
    return (y_prompt, y_sample, jnp.stack(ret_prompt), ret_sample.reshape(state_ret.shape),
            fox_k[:tp].reshape(bp, seq, FOX_HEADS, FOX_HD), fox_v[:tp].reshape(bp, seq, FOX_HEADS, FOX_HD),
            fox_lf[:tp].reshape(bp, seq, FOX_HEADS),
            fox_k[tp:].reshape(bs, n_tok, FOX_HEADS, FOX_HD), fox_v[tp:].reshape(bs, n_tok, FOX_HEADS, FOX_HD),
            fox_lf[tp:].reshape(bs, n_tok, FOX_HEADS),
            mem_k_prompt, mem_v_prompt)
```

```python
import functools
import math

import jax
import jax.numpy as jnp
from jax import lax
from jax.experimental import pallas as pl
from jax.experimental.pallas import tpu as pltpu

F32 = jnp.float32
BF16 = jnp.bfloat16
HIGHEST = lax.Precision.HIGHEST

RET_HEADS = 6
RET_D = 128
RET_W = RET_HEADS * RET_D
RET_CHUNK = 128
ROPE_BASE = 10000.0
FOX_HEADS = 12
FOX_HD = 64
FOX_W = FOX_HEADS * FOX_HD
MEM_HEADS = 4
MEM_HD = 64
MEM_W = MEM_HEADS * MEM_HD
N_GROUPS = 4
EXPERTS_PER_GROUP = 8
N_EXPERTS = N_GROUPS * EXPERTS_PER_GROUP
D_EXPERT = 256
LN_EPS = 1e-5
HEAD_NORM_EPS = 1e-6

LANES = 128
SUBLANES = 8
VMEM_LIMIT_BYTES = 48 * 1024 * 1024

ROW_TILE = 512
MOE_TILE = 256
FOX_BLOCK = 512
SAMPLE_BATCH_TILE = 8


def _cparams(n_axes):
    return pltpu.CompilerParams(
        dimension_semantics=("arbitrary",) * n_axes, vmem_limit_bytes=VMEM_LIMIT_BYTES)


def _dot(a, b):
    return jnp.dot(a, b, preferred_element_type=F32)


def _dot_nt(a, b):
    return lax.dot_general(a, b, (((1,), (1,)), ((), ())), preferred_element_type=F32)


def _dot_tn(a, b):
    return lax.dot_general(a, b, (((0,), (0,)), ((), ())), preferred_element_type=F32)


def _layer_norm(y, g, b):
    mu = jnp.mean(y, axis=-1, keepdims=True)
    d = y - mu
    var = jnp.mean(d * d, axis=-1, keepdims=True)
    return d * lax.rsqrt(var + LN_EPS) * g + b


def _matmul_kernel(x_ref, w_ref, o_ref):
    o_ref[...] = _dot(x_ref[...].astype(BF16), w_ref[...])


def matmul(x, w_bf16):
    t, k = x.shape
    n = w_bf16.shape[1]
    tm = min(ROW_TILE, t)
    return pl.pallas_call(
        _matmul_kernel,
        out_shape=jax.ShapeDtypeStruct((t, n), F32),
        grid=(t // tm,),
        in_specs=[pl.BlockSpec((tm, k), lambda i: (i, 0)),
                  pl.BlockSpec((k, n), lambda i: (0, 0))],
        out_specs=pl.BlockSpec((tm, n), lambda i: (i, 0)),
        compiler_params=_cparams(1),
        name="matmul",
    )(x, w_bf16)


def _proj_a_kernel(x_ref, w_ref, cos_ref, sin_ref, o_ref, *, k_scale):
    x = x_ref[...].astype(BF16)
    cos2 = cos_ref[...]
    sin2 = sin_ref[...]
    for c in range(2 * RET_HEADS):
        cols = slice(c * RET_D, (c + 1) * RET_D)
        y = _dot(x, w_ref[:, cols])
        y = y * cos2 + pltpu.roll(y, RET_D // 2, 1) * sin2
        if c >= RET_HEADS:
            y = y * k_scale
        o_ref[:, cols] = y
    o_ref[:, 2 * RET_W:] = _dot(x, w_ref[:, 2 * RET_W:])


def proj_a(x, w_bf16, cos2, sin2):
    t, k = x.shape
    n = w_bf16.shape[1]
    tm = min(ROW_TILE, t)
    return pl.pallas_call(
        functools.partial(_proj_a_kernel, k_scale=RET_D ** -0.5),
        out_shape=jax.ShapeDtypeStruct((t, n), F32),
        grid=(t // tm,),
        in_specs=[pl.BlockSpec((tm, k), lambda i: (i, 0)),
                  pl.BlockSpec((k, n), lambda i: (0, 0)),
                  pl.BlockSpec((tm, RET_D), lambda i: (i, 0)),
                  pl.BlockSpec((tm, RET_D), lambda i: (i, 0))],
        out_specs=pl.BlockSpec((tm, n), lambda i: (i, 0)),
        compiler_params=_cparams(1),
        name="proj_a",
    )(x, w_bf16, cos2, sin2)


def _kv_kernel(x_ref, w_ref, bf_ref, kv_ref, logf_ref):
    x = x_ref[...].astype(BF16)
    kv_ref[...] = _dot(x, w_ref[:, :2 * FOX_W])
    z = _dot(x, w_ref[:, 2 * FOX_W:]) + bf_ref[...]
    logf_ref[...] = jnp.minimum(z, 0.0) - jnp.log(1.0 + jnp.exp(-jnp.abs(z)))


def kv_shared(x, w_pad_bf16, b_forget_pad):
    t, k = x.shape
    n = w_pad_bf16.shape[1]
    tm = min(ROW_TILE, t)
    return pl.pallas_call(
        _kv_kernel,
        out_shape=(jax.ShapeDtypeStruct((t, 2 * FOX_W), F32),
                   jax.ShapeDtypeStruct((t, LANES), F32)),
        grid=(t // tm,),
        in_specs=[pl.BlockSpec((tm, k), lambda i: (i, 0)),
                  pl.BlockSpec((k, n), lambda i: (0, 0)),
                  pl.BlockSpec((1, LANES), lambda i: (0, 0))],
        out_specs=(pl.BlockSpec((tm, 2 * FOX_W), lambda i: (i, 0)),
                   pl.BlockSpec((tm, LANES), lambda i: (i, 0))),
        compiler_params=_cparams(1),
        name="kv_shared",
    )(x, w_pad_bf16, b_forget_pad)


def _out_ln_kernel(tokp_ref, memp_ref, toks_ref, mems_ref, x_ref, w1_ref, w2_ref, g_ref, b_ref, o_ref,
                   *, alpha, prompt_tiles):
    def run(tok_ref, mem_ref):
        mixed = _dot(tok_ref[...].astype(BF16), w1_ref[...]) + _dot(mem_ref[...].astype(BF16), w2_ref[...])
        o_ref[...] = _layer_norm(alpha * x_ref[...] + mixed, g_ref[...], b_ref[...])

    @pl.when(pl.program_id(0) < prompt_tiles)
    def _():
        run(tokp_ref, memp_ref)

    @pl.when(pl.program_id(0) >= prompt_tiles)
    def _():
        run(toks_ref, mems_ref)


def out_proj_ln(tok_p, mem_p, tok_s, mem_s, x, w1_bf16, w2_bf16, g, b, alpha):
    t, d = x.shape
    tm = min(ROW_TILE, t)
    wt, wm = tok_p.shape[1], mem_p.shape[1]
    n_p = tok_p.shape[0] // tm
    n_s = tok_s.shape[0] // tm

    def prompt_map(i):
        return (jnp.minimum(i, n_p - 1), 0)

    def sample_map(i):
        return (jnp.clip(i - n_p, 0, n_s - 1), 0)

    return pl.pallas_call(
        functools.partial(_out_ln_kernel, alpha=alpha, prompt_tiles=n_p),
        out_shape=jax.ShapeDtypeStruct((t, d), F32),
        grid=(t // tm,),
        in_specs=[pl.BlockSpec((tm, wt), prompt_map),
                  pl.BlockSpec((tm, wm), prompt_map),
                  pl.BlockSpec((tm, wt), sample_map),
                  pl.BlockSpec((tm, wm), sample_map),
                  pl.BlockSpec((tm, d), lambda i: (i, 0)),
                  pl.BlockSpec((wt, d), lambda i: (0, 0)),
                  pl.BlockSpec((wm, d), lambda i: (0, 0)),
                  pl.BlockSpec((1, d), lambda i: (0, 0)),
                  pl.BlockSpec((1, d), lambda i: (0, 0))],
        out_specs=pl.BlockSpec((tm, d), lambda i: (i, 0)),
        compiler_params=_cparams(1),
        name="out_proj_ln",
    )(tok_p, mem_p, tok_s, mem_s, x, w1_bf16, w2_bf16, g, b)


def _head_norm_gate(r, g):
    mu = jnp.mean(r, axis=-1, keepdims=True)
    d = r - mu
    var = jnp.mean(d * d, axis=-1, keepdims=True)
    return d * lax.rsqrt(var + HEAD_NORM_EPS) * (g * jax.nn.sigmoid(g))


def _retention_prompt_kernel(q_ref, k_ref, v_ref, g_ref, s0_ref, dm_ref, qd_ref, kd_ref, cd_ref,
                             o_ref, sout_ref, s_scr, *, chunks):
    i = pl.program_id(1)

    @pl.when(i == 0)
    def _():
        s_scr[...] = s0_ref[0]

    for c in range(chunks):
        rows = slice(c * RET_CHUNK, (c + 1) * RET_CHUNK)
        for h in range(RET_HEADS):
            cols = slice(h * RET_D, (h + 1) * RET_D)
            q = q_ref[rows, cols]
            k = k_ref[rows, cols]
            v = v_ref[rows, cols].astype(BF16)
            s_prev = s_scr[h]
            scores = _dot_nt(q.astype(BF16), k.astype(BF16)) * dm_ref[h]
            intra = _dot(scores.astype(BF16), v)
            cross = _dot((q * qd_ref[h]).astype(BF16), s_prev.astype(BF16))
            kd = (k * kd_ref[h]).astype(BF16)
            s_scr[h] = cd_ref[h] * s_prev + _dot_tn(kd, v)
            o_ref[rows, cols] = _head_norm_gate(intra + cross, g_ref[rows, cols])

    @pl.when(i == pl.num_programs(1) - 1)
    def _():
        sout_ref[0] = s_scr[...]


def retention_prompt(h, s0, decays, batch, seq):
    chunks = math.gcd(4, seq // RET_CHUNK)
    rows = chunks * RET_CHUNK
    n_i = seq // rows
    dm, qd, kd, cd = decays
    const = pl.BlockSpec((RET_HEADS, RET_CHUNK, RET_D), lambda b, i: (0, 0, 0))
    state = pl.BlockSpec((1, RET_HEADS, RET_D, RET_D), lambda b, i: (b, 0, 0, 0))

    def col(j):
        return pl.BlockSpec((rows, RET_W), lambda b, i, j=j: (b * n_i + i, j))

    return pl.pallas_call(
        functools.partial(_retention_prompt_kernel, chunks=chunks),
        out_shape=(jax.ShapeDtypeStruct((batch * seq, RET_W), F32),
                   jax.ShapeDtypeStruct((batch, RET_HEADS, RET_D, RET_D), F32)),
        grid=(batch, n_i),
        in_specs=[col(0), col(1), col(2), col(3), state, const, const, const, const],
        out_specs=(pl.BlockSpec((rows, RET_W), lambda b, i: (b * n_i + i, 0)), state),
        scratch_shapes=[pltpu.VMEM((RET_HEADS, RET_D, RET_D), F32)],
        compiler_params=_cparams(2),
        name="retention_prompt",
    )(h, h, h, h, s0, dm, qd, kd, cd)


def _retention_sample_kernel(qt_ref, kt_ref, v_ref, g_ref, s0_ref, sprev_ref, o_ref, sout_ref, r_scr,
                             *, n_req, n_tok, gammas):
    del sprev_ref
    for j in range(n_req):
        for h in range(RET_HEADS):
            cols = slice(h * RET_D, (h + 1) * RET_D)
            s = s0_ref[j, h]
            for t in range(n_tok):
                r = j * n_tok + t
                s = gammas[h] * s + kt_ref[0, h, :, r:r + 1] * v_ref[r:r + 1, cols]
                r_scr[r:r + 1, cols] = jnp.sum(qt_ref[0, h, :, r:r + 1] * s, axis=0, keepdims=True)
            sout_ref[j, h] = s
    for h in range(RET_HEADS):
        cols = slice(h * RET_D, (h + 1) * RET_D)
        o_ref[:, cols] = _head_norm_gate(r_scr[:, cols], g_ref[:, cols])


def retention_sample(h, qt, kt, s_all, s_new_all, layer, row0, n_req_total, n_tok):
    n_req = SAMPLE_BATCH_TILE
    rows = n_req * n_tok
    steps = n_req_total // n_req
    blk0 = row0 // rows
    gammas = tuple(1.0 - 2.0 ** (-5.0 - hh) for hh in range(RET_HEADS))
    state = pl.BlockSpec((n_req, RET_HEADS, RET_D, RET_D), lambda i: (layer * steps + i, 0, 0, 0))
    tr = pl.BlockSpec((1, RET_HEADS, RET_D, rows), lambda i: (i, 0, 0, 0))
    return pl.pallas_call(
        functools.partial(_retention_sample_kernel, n_req=n_req, n_tok=n_tok, gammas=gammas),
        out_shape=(jax.ShapeDtypeStruct((n_req_total * n_tok, RET_W), F32),
                   jax.ShapeDtypeStruct(s_new_all.shape, F32)),
        grid=(steps,),
        in_specs=[tr, tr,
                  pl.BlockSpec((rows, RET_W), lambda i: (blk0 + i, 2)),
                  pl.BlockSpec((rows, RET_W), lambda i: (blk0 + i, 3)),
                  state,
                  pl.BlockSpec(memory_space=pl.ANY)],
        out_specs=(pl.BlockSpec((rows, RET_W), lambda i: (i, 0)), state),
        scratch_shapes=[pltpu.VMEM((rows, RET_W), F32)],
        input_output_aliases={5: 1},
        compiler_params=_cparams(1),
        name="retention_sample",
    )(qt, kt, h, h, s_all, s_new_all)


def _mem_attn_kernel(q_ref, mk_ref, mv_ref, o_ref, *, banks, rows_per_bank):
    q = q_ref[...] * (MEM_HD ** -0.5)
    n_rows = q.shape[0]
    n_cols = max(n_rows, LANES)
    if n_cols > n_rows:
        q = jnp.concatenate([q, jnp.zeros((n_cols - n_rows, MEM_W), F32)], axis=0)
    qt = q.T.astype(BF16)
    out_t = None
    for j in range(banks):
        mk = mk_ref[j].astype(BF16)
        mvt = mv_ref[j].T.astype(BF16)
        heads = []
        for h in range(MEM_HEADS):
            s = _dot(mk[:, h * MEM_HD:(h + 1) * MEM_HD], qt[h * MEM_HD:(h + 1) * MEM_HD, :])
            p = jnp.exp(s - jnp.max(s, axis=0, keepdims=True))
            p = p / jnp.sum(p, axis=0, keepdims=True)
            heads.append(_dot(mvt[h * MEM_HD:(h + 1) * MEM_HD, :], p.astype(BF16)))
        o_t = jnp.concatenate(heads, axis=0)
        if out_t is None:
            out_t = o_t
        else:
            col = lax.broadcasted_iota(jnp.int32, (MEM_W, n_cols), 1)
            out_t = jnp.where(col >= j * rows_per_bank, o_t, out_t)
    o_ref[...] = out_t.T[:n_rows, :]


def mem_attention(h, q_col_block, mk, mv, bank0, *, row_block0, n_steps, rows, banks, steps_per_bank):
    mem_len = mk.shape[1]
    if banks == 1:
        bank_map = lambda i: (bank0 + i // steps_per_bank, 0, 0)
    else:
        bank_map = lambda i: (bank0 // banks + i, 0, 0)
    return pl.pallas_call(
        functools.partial(_mem_attn_kernel, banks=banks, rows_per_bank=rows // banks),
        out_shape=jax.ShapeDtypeStruct((n_steps * rows, MEM_W), F32),
        grid=(n_steps,),
        in_specs=[pl.BlockSpec((rows, MEM_W), lambda i: (row_block0 + i, q_col_block)),
                  pl.BlockSpec((banks, mem_len, MEM_W), bank_map),
                  pl.BlockSpec((banks, mem_len, MEM_W), bank_map)],
        out_specs=pl.BlockSpec((rows, MEM_W), lambda i: (i, 0)),
        compiler_params=_cparams(1),
        name="mem_attention",
    )(h, mk, mv)


def _cumsum_kernel(x_ref, o_ref, carry):
    @pl.when(pl.program_id(1) == 0)
    def _():
        carry[...] = jnp.zeros_like(carry)

    n = x_ref.shape[0]
    row = lax.broadcasted_iota(jnp.int32, (n, n), 0)
    col = lax.broadcasted_iota(jnp.int32, (n, n), 1)
    tri = (row >= col).astype(F32)
    c = jnp.dot(tri, x_ref[...], precision=HIGHEST, preferred_element_type=F32) + carry[...]
    o_ref[...] = c
    carry[...] = c[n - 1:n, :]


def cumsum_rows(x, batch, seq):
    tm = min(ROW_TILE, seq)
    n_i = seq // tm
    return pl.pallas_call(
        _cumsum_kernel,
        out_shape=jax.ShapeDtypeStruct((batch * seq, x.shape[1]), F32),
        grid=(batch, n_i),
        in_specs=[pl.BlockSpec((tm, x.shape[1]), lambda b, i: (b * n_i + i, 0))],
        out_specs=pl.BlockSpec((tm, x.shape[1]), lambda b, i: (b * n_i + i, 0)),
        scratch_shapes=[pltpu.VMEM((1, x.shape[1]), F32)],
        compiler_params=_cparams(2),
        name="cumsum_rows",
    )(x)


BIAS_COLS = 12


def _split3(x):
    hi = x.astype(BF16).astype(F32)
    mid = (x - hi).astype(BF16).astype(F32)
    lo = x - hi - mid
    return hi, mid, lo


def _place(lane, base, parts):
    out = jnp.zeros(lane.shape, F32)
    for idx, part in enumerate(parts):
        out = jnp.where(lane == base + idx, part, out)
    return out


def _fox_prep_kernel(k_ref, v_ref, c_ref, ka_ref, vt_ref):
    p = pl.program_id(0)
    rows = k_ref.shape[0]
    lane = lax.broadcasted_iota(jnp.int32, (rows, LANES), 1)
    c2 = c_ref[...]
    bias = jnp.zeros((rows, LANES), F32)
    for e in range(2):
        ck = jnp.sum(jnp.where(lane == 2 * p + e, c2, 0.0), axis=1, keepdims=True)
        base = e * (BIAS_COLS // 2)
        ones = jnp.where((lane >= base) & (lane < base + 3), 1.0, 0.0)
        bias = bias + ones + _place(lane, base + 3, [-part for part in _split3(ck)])
    ka_ref[0] = jnp.concatenate([k_ref[...], bias], axis=1).astype(BF16)
    vt_ref[0] = v_ref[...].T.astype(BF16)


def fox_prep(kv, c, n_rows):
    pairs = FOX_HEADS // 2
    tm = min(FOX_BLOCK, n_rows)
    return pl.pallas_call(
        _fox_prep_kernel,
        out_shape=(jax.ShapeDtypeStruct((pairs, n_rows, 2 * LANES), BF16),
                   jax.ShapeDtypeStruct((pairs, LANES, n_rows), BF16)),
        grid=(pairs, n_rows // tm),
        in_specs=[pl.BlockSpec((tm, LANES), lambda p, i: (i, p)),
                  pl.BlockSpec((tm, LANES), lambda p, i: (i, pairs + p)),
                  pl.BlockSpec((tm, LANES), lambda p, i: (i, 0))],
        out_specs=(pl.BlockSpec((1, tm, 2 * LANES), lambda p, i: (p, i, 0)),
                   pl.BlockSpec((1, LANES, tm), lambda p, i: (p, 0, i))),
        compiler_params=_cparams(2),
        name="fox_prep",
    )(kv, kv, c)


def _fox_prompt_kernel(q_ref, c_ref, ka_ref, vt_ref, o_ref, acc_a, acc_b, *, blk):
    p = pl.program_id(1)
    i = pl.program_id(2)
    lane = lax.broadcasted_iota(jnp.int32, (blk, LANES), 1)
    q2 = q_ref[...] * (FOX_HD ** -0.5)
    c2 = c_ref[...]
    qts = []
    for e in range(2):
        head_lanes = (lane >= e * FOX_HD) & (lane < (e + 1) * FOX_HD)
        cq = jnp.sum(jnp.where(lane == 2 * p + e, c2, 0.0), axis=1, keepdims=True)
        base = e * (BIAS_COLS // 2)
        ones = jnp.where((lane >= base + 3) & (lane < base + 6), 1.0, 0.0)
        q_aug = jnp.concatenate([jnp.where(head_lanes, q2, 0.0), ones + _place(lane, base, _split3(cq))], axis=1)
        qts.append(q_aug.T.astype(BF16))
    accs = (acc_a, acc_b)
    acc_a[...] = jnp.zeros(acc_a.shape, F32)
    acc_b[...] = jnp.zeros(acc_b.shape, F32)

    def block(j, carry, masked):
        start = pl.multiple_of(j * blk, blk)
        kb = ka_ref[0, pl.ds(start, blk), :]
        vt = vt_ref[0, :, pl.ds(start, blk)]
        new = []
        for e in range(2):
            m_old, l_old = carry[2 * e], carry[2 * e + 1]
            s = _dot(kb, qts[e])
            if masked:
                key = lax.broadcasted_iota(jnp.int32, (blk, blk), 0)
                qry = lax.broadcasted_iota(jnp.int32, (blk, blk), 1)
                s = jnp.where(key <= qry, s, -jnp.inf)
            m_new = jnp.maximum(m_old, jnp.max(s, axis=0, keepdims=True))
            a = jnp.exp(m_old - m_new)
            pe = jnp.exp(s - m_new)
            new += [m_new, a * l_old + jnp.sum(pe, axis=0, keepdims=True)]
            acc = accs[e]
            acc[...] = a * acc[...] + _dot(vt[e * FOX_HD:(e + 1) * FOX_HD, :], pe.astype(BF16))
        return tuple(new)

    init = (jnp.full((1, blk), -jnp.inf, F32), jnp.zeros((1, blk), F32)) * 2
    carry = lax.fori_loop(0, i, lambda j, c: block(j, c, False), init)
    carry = block(i, carry, True)
    out_t = jnp.concatenate([acc_a[...] / carry[1], acc_b[...] / carry[3]], axis=0)
    o_ref[...] = out_t.T


def fox_prompt(h, c, k_aug, v_t, batch, seq):
    blk = min(FOX_BLOCK, seq)
    n_q = seq // blk
    pairs = FOX_HEADS // 2
    return pl.pallas_call(
        functools.partial(_fox_prompt_kernel, blk=blk),
        out_shape=jax.ShapeDtypeStruct((batch * seq, FOX_W), F32),
        grid=(batch, pairs, n_q),
        in_specs=[pl.BlockSpec((blk, LANES), lambda b, p, i: (b * n_q + i, p)),
                  pl.BlockSpec((blk, LANES), lambda b, p, i: (b * n_q + i, 0)),
                  pl.BlockSpec((1, seq, 2 * LANES), lambda b, p, i: (p, b, 0)),
                  pl.BlockSpec((1, LANES, seq), lambda b, p, i: (p, 0, b))],
        out_specs=pl.BlockSpec((blk, LANES), lambda b, p, i: (b * n_q + i, p)),
        scratch_shapes=[pltpu.VMEM((FOX_HD, blk), F32), pltpu.VMEM((FOX_HD, blk), F32)],
        compiler_params=_cparams(3),
        name="fox_prompt",
    )(h, c, k_aug, v_t)


FOX_COLS = 16


GATHER_PAGES = 4


def _expand_matrix(n_tok):
    e_row = lax.broadcasted_iota(jnp.int32, (FOX_HEADS, LANES), 0)
    e_col = lax.broadcasted_iota(jnp.int32, (FOX_HEADS, LANES), 1)
    return ((e_col % FOX_COLS == e_row) & (e_col < n_tok * FOX_COLS)).astype(F32)


def _head_copy(cache_ref, page_idx, head, slabs, slot, which, j, sem):
    return pltpu.make_async_copy(cache_ref.at[page_idx, :, head, :], slabs.at[slot, which, j, head], sem)


def _fox_gather_kernel(pt_ref, ck_ref, cv_ref, *refs, page, n_tok):
    pgs = GATHER_PAGES
    lf_refs = refs[:pgs]
    kc_ref, vc_ref, ce_ref, slabs, sems, carry = refs[pgs:]
    g = pl.program_id(1)
    n_g = pl.num_programs(1)
    step = pl.program_id(0) * n_g + g
    n_steps = pl.num_programs(0) * n_g
    caches = (ck_ref, cv_ref)

    def issue(st, slot):
        for j in range(pgs):
            page_idx = pt_ref[st * pgs + j]
            for head in range(FOX_HEADS):
                for which in range(2):
                    _head_copy(caches[which], page_idx, head, slabs, slot, which, j, sems.at[slot]).start(
                        priority=which)

    @pl.when(step == 0)
    def _():
        issue(step, 0)

    for slot in range(2):
        @pl.when(step % 2 == slot)
        def _(slot=slot):
            @pl.when(step + 1 < n_steps)
            def _():
                issue(step + 1, 1 - slot)

            for _unused in range(pgs * FOX_HEADS * 2):
                _head_copy(ck_ref, 0, 0, slabs, slot, 0, 0, sems.at[slot]).wait()
            outs = (kc_ref, vc_ref)
            for j in range(pgs):
                for which in range(2):
                    for pair in range(FOX_HEADS // 2):
                        both = jnp.concatenate([slabs[slot, which, j, 2 * pair], slabs[slot, which, j, 2 * pair + 1]],
                                               axis=1)
                        outs[which][j * page:(j + 1) * page, pair * LANES:(pair + 1) * LANES] = both.astype(BF16)

    @pl.when(g == 0)
    def _():
        carry[...] = jnp.zeros_like(carry)

    expand = _expand_matrix(n_tok)
    t_row = lax.broadcasted_iota(jnp.int32, (page, page), 0)
    t_col = lax.broadcasted_iota(jnp.int32, (page, page), 1)
    tri = (t_row >= t_col).astype(F32)
    for j in range(pgs):
        c_in = jnp.dot(tri, lf_refs[j][0], precision=HIGHEST, preferred_element_type=F32) + carry[...]
        carry[...] = c_in[page - 1:page, :]
        ce_ref[j * page:(j + 1) * page, :] = jnp.dot(c_in, expand, precision=HIGHEST, preferred_element_type=F32)


def fox_gather(cache_k, cache_v, cache_logf, page_table, n_tok):
    n_req, n_pages = page_table.shape
    page = cache_k.shape[1]
    pgs = GATHER_PAGES
    assert n_pages % pgs == 0
    n_g = n_pages // pgs
    rows = pgs * page

    def lf_spec(j):
        return pl.BlockSpec((1, page, FOX_HEADS),
                            lambda r, g, pt, j=j: (pt[r * n_pages + g * pgs + j], 0, 0))

    def out_spec(width):
        return pl.BlockSpec((rows, width), lambda r, g, pt: (r * n_g + g, 0))

    grid_spec = pltpu.PrefetchScalarGridSpec(
        num_scalar_prefetch=1,
        grid=(n_req, n_g),
        in_specs=[pl.BlockSpec(memory_space=pl.ANY), pl.BlockSpec(memory_space=pl.ANY)]
        + [lf_spec(j) for j in range(pgs)],
        out_specs=(out_spec(FOX_W), out_spec(FOX_W), out_spec(LANES)),
        scratch_shapes=[pltpu.VMEM((2, 2, pgs, FOX_HEADS, page, FOX_HD), F32),
                        pltpu.SemaphoreType.DMA((2,)),
                        pltpu.VMEM((1, FOX_HEADS), F32)],
    )
    total = n_req * n_pages * page
    return pl.pallas_call(
        functools.partial(_fox_gather_kernel, page=page, n_tok=n_tok),
        out_shape=(jax.ShapeDtypeStruct((total, FOX_W), BF16), jax.ShapeDtypeStruct((total, FOX_W), BF16),
                   jax.ShapeDtypeStruct((total, LANES), F32)),
        grid_spec=grid_spec,
        compiler_params=_cparams(2),
        name="fox_gather",
    )(page_table.reshape(-1), cache_k, cache_v, *([cache_logf] * pgs))


def _fox_sample_kernel(q_ref, kvn_ref, lfn_ref, kc_ref, vc_ref, ce_ref, o_ref, s_scr, *, n_tok):
    n_past = kc_ref.shape[0]
    reqs = SUBLANES // n_tok
    which = pl.program_id(0) % reqs

    def pick(block):
        out = block[0:n_tok, :]
        for w in range(1, reqs):
            out = jnp.where(which == w, block[w * n_tok:(w + 1) * n_tok, :], out)
        return out

    q = pick(q_ref[...])[:, :FOX_W] * (FOX_HD ** -0.5)
    kvn = pick(kvn_ref[...])
    lfn = pick(lfn_ref[...])[:, :FOX_HEADS]
    head_of_lane = lax.broadcasted_iota(jnp.int32, (FOX_COLS, FOX_W), 1) // FOX_HD
    head_of_row = lax.broadcasted_iota(jnp.int32, (FOX_COLS, FOX_W), 0)
    q_rows = [jnp.where(head_of_lane == head_of_row, jnp.broadcast_to(q[t:t + 1, :], (FOX_COLS, FOX_W)), 0.0)
              for t in range(n_tok)]
    pad_rows = LANES - n_tok * FOX_COLS
    q_bd = jnp.concatenate(q_rows + [jnp.zeros((pad_rows, FOX_W), F32)], axis=0).astype(BF16)

    s_scr[0:n_past, :] = _dot_nt(kc_ref[...], q_bd) - ce_ref[...]

    lfn_e = jnp.dot(lfn, _expand_matrix(n_tok), precision=HIGHEST, preferred_element_type=F32)
    carry = ce_ref[n_past - 1:n_past, :]
    c_rows = []
    for t in range(n_tok):
        carry = carry + lfn_e[t:t + 1, :]
        c_rows.append(carry)
    c_new = jnp.concatenate(c_rows, axis=0)
    n_row = lax.broadcasted_iota(jnp.int32, (n_tok, LANES), 0)
    n_col = lax.broadcasted_iota(jnp.int32, (n_tok, LANES), 1)
    tok_of_col = n_col // FOX_COLS
    cq = jnp.sum(jnp.where(tok_of_col == n_row, c_new, 0.0), axis=0, keepdims=True)
    s_new = _dot_nt(kvn[:, :FOX_W].astype(BF16), q_bd) - c_new
    s_new = jnp.where(n_row <= tok_of_col, s_new, -jnp.inf)
    s_scr[n_past:n_past + n_tok, :] = s_new
    s_scr[n_past + n_tok:, :] = jnp.full((s_scr.shape[0] - n_past - n_tok, LANES), -jnp.inf, F32)

    s_all = s_scr[...] + cq
    pr = jnp.exp(s_all - jnp.max(s_all, axis=0, keepdims=True))
    s_scr[...] = pr / jnp.sum(pr, axis=0, keepdims=True)

    acc = _dot(s_scr[0:n_past, :].T.astype(BF16), vc_ref[...])
    n_pad = s_scr.shape[0] - n_past
    v_new = jnp.concatenate([kvn[:, FOX_W:], jnp.zeros((n_pad - n_tok, FOX_W), F32)], axis=0).astype(BF16)
    acc = acc + _dot_tn(s_scr[n_past:, :].astype(BF16), v_new)

    a_row = lax.broadcasted_iota(jnp.int32, (LANES, FOX_W), 0)
    a_col = lax.broadcasted_iota(jnp.int32, (LANES, FOX_W), 1)
    acc = jnp.where(a_col // FOX_HD == a_row % FOX_COLS, acc, 0.0)
    out = acc.reshape(LANES // FOX_COLS, FOX_COLS, FOX_W).sum(axis=1)[:n_tok, :]
    for w in range(reqs):
        @pl.when(which == w)
        def _(w=w):
            o_ref[w * n_tok:(w + 1) * n_tok, :] = out


def fox_sample(h, kv, logf, k_req, v_req, c_req, row0, n_req, n_tok):
    past = k_req.shape[0] // n_req
    reqs = SUBLANES // n_tok
    blk0 = row0 // SUBLANES

    def new_spec(width):
        return pl.BlockSpec((SUBLANES, width), lambda r: (blk0 + r // reqs, 0))

    def req_spec(width):
        return pl.BlockSpec((past, width), lambda r: (r, 0))

    return pl.pallas_call(
        functools.partial(_fox_sample_kernel, n_tok=n_tok),
        out_shape=jax.ShapeDtypeStruct((n_req * n_tok, FOX_W), F32),
        grid=(n_req,),
        in_specs=[new_spec(h.shape[1]), new_spec(kv.shape[1]), new_spec(logf.shape[1]),
                  req_spec(FOX_W), req_spec(FOX_W), req_spec(LANES)],
        out_specs=pl.BlockSpec((SUBLANES, FOX_W), lambda r: (r // reqs, 0)),
        scratch_shapes=[pltpu.VMEM((past + SUBLANES, LANES), F32)],
        compiler_params=_cparams(1),
        name="fox_sample",
    )(h, kv, logf, k_req, v_req, c_req)


def _router_kernel(x_ref, w_ref, b_ref, o_ref, cnt_ref, carry):
    @pl.when(pl.program_id(0) == 0)
    def _():
        carry[...] = jnp.zeros_like(carry)

    logits = jnp.dot(x_ref[...], w_ref[...], precision=HIGHEST, preferred_element_type=F32) + b_ref[...]
    lane = lax.broadcasted_iota(jnp.int32, logits.shape, 1)
    big = jnp.int32(LANES)
    neg = -jnp.inf
    gl = jnp.where(lane < N_GROUPS, logits, neg)
    g_max = jnp.max(gl, axis=-1, keepdims=True)
    g_idx = jnp.min(jnp.where(gl == g_max, lane, big), axis=-1, keepdims=True)
    g_w = 1.0 / jnp.sum(jnp.exp(gl - g_max), axis=-1, keepdims=True)
    in_group = (lane >= N_GROUPS) & (lane < N_GROUPS + N_EXPERTS) & (((lane - N_GROUPS) >> 3) == g_idx)
    el = jnp.where(in_group, logits, neg)
    v1 = jnp.max(el, axis=-1, keepdims=True)
    i1 = jnp.min(jnp.where(el == v1, lane, big), axis=-1, keepdims=True)
    el2 = jnp.where(lane == i1, neg, el)
    v2 = jnp.max(el2, axis=-1, keepdims=True)
    i2 = jnp.min(jnp.where(el2 == v2, lane, big), axis=-1, keepdims=True)
    e2 = jnp.exp(v2 - v1)
    w1 = g_w / (1.0 + e2)
    w2 = g_w * e2 / (1.0 + e2)
    tm = logits.shape[0]
    e1 = i1 - N_GROUPS
    e2 = i2 - N_GROUPS
    oh1 = jnp.where(lane == e1, 1.0, 0.0)
    oh2 = jnp.where(lane == e2, 1.0, 0.0)
    row = lax.broadcasted_iota(jnp.int32, (tm, tm), 0)
    col = lax.broadcasted_iota(jnp.int32, (tm, tm), 1)
    before = jnp.where(col < row, 1.0, 0.0).astype(BF16)
    tot1 = jnp.sum(oh1, axis=0, keepdims=True)
    base = carry[...]
    r1 = jnp.sum(oh1 * (base + _dot(before, oh1.astype(BF16))), axis=-1, keepdims=True)
    r2 = jnp.sum(oh2 * (base + tot1 + _dot(before, oh2.astype(BF16))), axis=-1, keepdims=True)
    carry[...] = base + tot1 + jnp.sum(oh2, axis=0, keepdims=True)
    cnt_ref[...] = carry[...]
    out = jnp.zeros(logits.shape, F32)
    for idx, val in enumerate((e1.astype(F32), e2.astype(F32), w1, w2, r1, r2)):
        out = jnp.where(lane == idx, val, out)
    o_ref[...] = out


def router(x, w_pad, b_pad):
    t, d = x.shape
    tm = min(ROW_TILE, t)
    return pl.pallas_call(
        _router_kernel,
        out_shape=(jax.ShapeDtypeStruct((t, LANES), F32), jax.ShapeDtypeStruct((1, LANES), F32)),
        grid=(t // tm,),
        in_specs=[pl.BlockSpec((tm, d), lambda i: (i, 0)),
                  pl.BlockSpec((d, LANES), lambda i: (0, 0)),
                  pl.BlockSpec((1, LANES), lambda i: (0, 0))],
        out_specs=(pl.BlockSpec((tm, LANES), lambda i: (i, 0)), pl.BlockSpec((1, LANES), lambda i: (0, 0))),
        scratch_shapes=[pltpu.VMEM((1, LANES), F32)],
        compiler_params=_cparams(1),
        name="router",
    )(x, w_pad, b_pad)


def _pack_rows(y):
    n = y.shape[1] // 2
    bits = lax.bitcast_convert_type(y.astype(BF16).astype(F32), jnp.uint32)
    return bits[:, :n] | (bits[:, n:] >> 16)


def _unpack_rows(w):
    return (lax.bitcast_convert_type(w & jnp.uint32(0xFFFF0000), F32),
            lax.bitcast_convert_type(w << 16, F32))


def _row_copy(src_ref, src_row, dst_ref, dst_row, sem):
    return pltpu.make_async_copy(src_ref.at[pl.ds(src_row, 1)], dst_ref.at[pl.ds(dst_row, 1)], sem)


def _drain_rows(src_ref, dst_ref, sem, n_rows):
    def drain(r, carry):
        _row_copy(src_ref, 0, dst_ref, 0, sem).wait()
        return carry

    lax.fori_loop(0, n_rows, drain, 0, unroll=8)


def _dispatch_kernel(pos_ref, x_ref, xs_in_ref, xs_ref, buf, sems, *, tm):
    del xs_in_ref
    i = pl.program_id(0)
    n = pl.num_programs(0)
    for slot in range(2):
        @pl.when(i % 2 == slot)
        def _(slot=slot):
            @pl.when(i >= 2)
            def _():
                _drain_rows(buf.at[slot], xs_ref, sems.at[slot], 2 * tm)

            buf[slot] = _pack_rows(x_ref[...])

            def issue(r, carry):
                for k in range(2):
                    _row_copy(buf.at[slot], r, xs_ref, pos_ref[0, 0, k * tm + r], sems.at[slot]).start(priority=k)
                return carry

            lax.fori_loop(0, tm, issue, 0, unroll=8)

            @pl.when(i == n - 1)
            def _():
                _drain_rows(buf.at[slot], xs_ref, sems.at[slot], 2 * tm)

                @pl.when(n >= 2)
                def _():
                    _drain_rows(buf.at[1 - slot], xs_ref, sems.at[1 - slot], 2 * tm)


def moe_dispatch(x, pos3, xs_init):
    t, d = x.shape
    tm = MOE_TILE
    return pl.pallas_call(
        functools.partial(_dispatch_kernel, tm=tm),
        out_shape=jax.ShapeDtypeStruct(xs_init.shape, jnp.uint32),
        grid=(t // tm,),
        in_specs=[pl.BlockSpec((1, 1, 2 * tm), lambda i: (i, 0, 0), memory_space=pltpu.SMEM),
                  pl.BlockSpec((tm, d), lambda i: (i, 0)),
                  pl.BlockSpec(memory_space=pl.ANY)],
        out_specs=pl.BlockSpec(memory_space=pl.ANY),
        scratch_shapes=[pltpu.VMEM((2, tm, d // 2), jnp.uint32), pltpu.SemaphoreType.DMA((2,))],
        input_output_aliases={2: 0},
        compiler_params=_cparams(1),
        name="moe_dispatch",
    )(pos3, x, xs_init)


def _expert_kernel(te_ref, na_ref, x_ref, wu_ref, wd_ref, o_ref, wu_bf, wd_bf):
    i = pl.program_id(0)
    active = i < na_ref[0]
    changed = jnp.logical_or(i == 0, te_ref[i] != te_ref[jnp.maximum(i - 1, 0)])

    @pl.when(jnp.logical_and(active, changed))
    def _():
        wu_bf[...] = wu_ref[0].astype(BF16)
        wd_bf[...] = wd_ref[0].astype(BF16)

    @pl.when(active)
    def _():
        half = wu_bf.shape[0] // 2
        xa, xb = _unpack_rows(x_ref[...])
        up = _dot(xa.astype(BF16), wu_bf[:half, :]) + _dot(xb.astype(BF16), wu_bf[half:, :])
        hid = jax.nn.silu(up[:, :D_EXPERT]) * up[:, D_EXPERT:]
        o_ref[...] = _dot(hid.astype(BF16), wd_bf[...])

    @pl.when(jnp.logical_not(active))
    def _():
        o_ref[...] = jnp.zeros_like(o_ref)


def moe_experts(xs, w_up, w_down, tile_expert, n_active):
    r = xs.shape[0]
    d = w_up.shape[1]
    tm = MOE_TILE
    n_tiles = r // tm

    def row_map(i, te, na):
        return (jnp.minimum(i, na[0] - 1), 0)

    grid_spec = pltpu.PrefetchScalarGridSpec(
        num_scalar_prefetch=2,
        grid=(n_tiles,),
        in_specs=[pl.BlockSpec((tm, d // 2), row_map),
                  pl.BlockSpec((1, d, 2 * D_EXPERT), lambda i, te, na: (te[i], 0, 0)),
                  pl.BlockSpec((1, D_EXPERT, d), lambda i, te, na: (te[i], 0, 0))],
        out_specs=pl.BlockSpec((tm, d), lambda i, te, na: (i, 0)),
        scratch_shapes=[pltpu.VMEM((d, 2 * D_EXPERT), BF16), pltpu.VMEM((D_EXPERT, d), BF16)],
    )
    return pl.pallas_call(
        _expert_kernel,
        out_shape=jax.ShapeDtypeStruct((r, d), F32),
        grid_spec=grid_spec,
        compiler_params=_cparams(1),
        name="moe_experts",
    )(tile_expert, n_active, xs, w_up, w_down)


def _combine_kernel(pos_ref, posn_ref, x_ref, rt_ref, g_ref, b_ref, ys_ref, o_ref, buf, sems, *, tm, alpha):
    i = pl.program_id(0)
    n = pl.num_programs(0)

    def gather(p_ref, slot):
        def issue(r, carry):
            for k in range(2):
                _row_copy(ys_ref, p_ref[0, 0, k * tm + r], buf.at[slot, k], r, sems.at[slot]).start(priority=k)
            return carry

        lax.fori_loop(0, tm, issue, 0, unroll=8)

    @pl.when(i == 0)
    def _():
        gather(pos_ref, 0)

    for slot in range(2):
        @pl.when(i % 2 == slot)
        def _(slot=slot):
            @pl.when(i + 1 < n)
            def _():
                gather(posn_ref, 1 - slot)

            _drain_rows(ys_ref, buf.at[slot, 0], sems.at[slot], 2 * tm)
            rt = rt_ref[...]
            ffn = rt[:, 2:3] * buf[slot, 0] + rt[:, 3:4] * buf[slot, 1]
            o_ref[...] = _layer_norm(alpha * x_ref[...] + ffn, g_ref[...], b_ref[...])


def moe_combine(x, routing, ys, pos3, g, b, alpha):
    t, d = x.shape
    tm = MOE_TILE
    n = t // tm
    return pl.pallas_call(
        functools.partial(_combine_kernel, tm=tm, alpha=alpha),
        out_shape=jax.ShapeDtypeStruct((t, d), F32),
        grid=(n,),
        in_specs=[pl.BlockSpec((1, 1, 2 * tm), lambda i: (i, 0, 0), memory_space=pltpu.SMEM),
                  pl.BlockSpec((1, 1, 2 * tm), lambda i: (jnp.minimum(i + 1, n - 1), 0, 0),
                               memory_space=pltpu.SMEM),
                  pl.BlockSpec((tm, d), lambda i: (i, 0)),
                  pl.BlockSpec((tm, LANES), lambda i: (i, 0)),
                  pl.BlockSpec((1, d), lambda i: (0, 0)),
                  pl.BlockSpec((1, d), lambda i: (0, 0)),
                  pl.BlockSpec(memory_space=pl.ANY)],
        out_specs=pl.BlockSpec((tm, d), lambda i: (i, 0)),
        scratch_shapes=[pltpu.VMEM((2, 2, tm, d), F32), pltpu.SemaphoreType.DMA((2,))],
        compiler_params=_cparams(1),
        name="moe_combine",
    )(pos3, pos3, x, routing, g, b, ys)


def moe_plan(routing, counts, n_tiles, expert0):
    t = routing.shape[0]
    tm = MOE_TILE
    experts = jnp.arange(N_EXPERTS, dtype=jnp.int32)
    cnt = counts[0, :N_EXPERTS].astype(jnp.int32)
    tiles_e = (cnt + tm - 1) // tm
    tile_end = jnp.cumsum(tiles_e)
    offs = (tile_end - tiles_e) * tm
    ids = routing[:, 0:2].astype(jnp.int32)
    rank = routing[:, 4:6].astype(jnp.int32)
    pos = jnp.sum(jnp.where(ids[:, :, None] == experts, offs, 0), axis=-1) + rank
    n_active = tile_end[-1:]
    tile_ids = jnp.minimum(jnp.arange(n_tiles, dtype=jnp.int32), n_active[0] - 1)
    tile_expert = jnp.minimum(jnp.sum((tile_ids[:, None] >= tile_end[None, :]).astype(jnp.int32), axis=1),
                              N_EXPERTS - 1) + expert0
    pos3 = pos.reshape(t // tm, tm, 2).transpose(0, 2, 1).reshape(t // tm, 1, 2 * tm)
    return pos3, tile_expert, n_active


def hierarchical_moe_ln(x, w_router, b_router, w_up, w_down, expert0, g, b, alpha):
    t, d = x.shape
    n_tiles = (2 * t) // MOE_TILE + N_EXPERTS
    routing, counts = router(x, w_router, b_router)
    pos3, tile_expert, n_active = moe_plan(routing, counts, n_tiles, expert0)
    xs = moe_dispatch(x, pos3, jnp.zeros((n_tiles * MOE_TILE, d // 2), jnp.uint32))
    ys = moe_experts(xs, w_up, w_down, tile_expert, n_active)
    return moe_combine(x, routing, ys, pos3, g, b, alpha)


def _rotary_tables(pos):
    half = RET_D // 2
    inv_freq = ROPE_BASE ** (-jnp.arange(half, dtype=F32) / half)
    ang = pos.astype(F32)[:, None] * inv_freq[None, :]
    cos, sin = jnp.cos(ang), jnp.sin(ang)
    return jnp.concatenate([cos, cos], axis=1), jnp.concatenate([-sin, sin], axis=1)


def _retention_decays():
    log_gamma = jnp.log(1.0 - 2.0 ** (-5.0 - jnp.arange(RET_HEADS, dtype=F32)))
    c = RET_CHUNK
    i = jnp.arange(c, dtype=F32)
    diff = i[:, None] - i[None, :]
    dm = jnp.where(diff >= 0, jnp.exp(log_gamma[:, None, None] * jnp.maximum(diff, 0.0)[None]), 0.0)
    shape = (RET_HEADS, c, RET_D)
    qd = jnp.broadcast_to(jnp.exp(log_gamma[:, None] * (i + 1.0)[None, :])[:, :, None], shape)
    kd = jnp.broadcast_to(jnp.exp(log_gamma[:, None] * (c - 1 - i)[None, :])[:, :, None], shape)
    cd = jnp.broadcast_to(jnp.exp(log_gamma * c)[:, None, None], shape)
    return dm, qd, kd, cd


def kernel(x_prompt, x_sample, mem_prompt, state_ret, cache_fox_k, cache_fox_v, cache_fox_logf, cache_mem_k, cache_mem_v, page_table, w_in_a, w_in_b, w_kv_shared, b_forget, w_o, w_mem_kv, ln1_g, ln1_b, ln2_g, ln2_b, w_group, b_group, w_route, b_route, w_up, w_down):
    bp, seq, d = x_prompt.shape
    bs, n_tok, _ = x_sample.shape
    depth = w_o.shape[0]
    n_a = w_in_a.shape[0]
    mem_len = mem_prompt.shape[1]
    n_pages, page = page_table.shape[1], cache_fox_k.shape[1]
    past = n_pages * page
    tp, ts = bp * seq, bs * n_tok
    alpha = (2 * depth) ** 0.25
    assert w_in_a.shape[2] == 4 * RET_W + MEM_W and w_in_b.shape[2] == FOX_W + MEM_W
    assert w_kv_shared.shape[1] == 2 * FOX_W + FOX_HEADS and w_up.shape[1:3] == (N_GROUPS, EXPERTS_PER_GROUP)
    assert seq % RET_CHUNK == 0 and SUBLANES % n_tok == 0 and tp % ROW_TILE == 0 and ts % ROW_TILE == 0

    x = jnp.concatenate([x_prompt.reshape(tp, d), x_sample.reshape(ts, d)], axis=0)
    t_all = tp + ts

    w_mem_all = jnp.transpose(w_mem_kv, (1, 0, 2)).reshape(d, depth * 2 * MEM_W).astype(BF16)
    mem_kv = matmul(mem_prompt.reshape(bp * mem_len, d), w_mem_all)
    mem_kv5 = mem_kv.reshape(bp, mem_len, depth, 2, MEM_HEADS, MEM_HD)
    mem_k_prompt = jnp.transpose(mem_kv5[:, :, :, 0], (2, 0, 1, 3, 4))
    mem_v_prompt = jnp.transpose(mem_kv5[:, :, :, 1], (2, 0, 1, 3, 4))
    mem_kv_banks = mem_kv.reshape(bp, mem_len, depth * 2 * MEM_W)
    cache_mk = cache_mem_k.reshape(depth * bs, mem_len, MEM_W)
    cache_mv = cache_mem_v.reshape(depth * bs, mem_len, MEM_W)

    pos_all = jnp.concatenate([jnp.tile(jnp.arange(seq), bp), jnp.tile(past + jnp.arange(n_tok), bs)])
    cos2, sin2 = _rotary_tables(pos_all)
    decays = _retention_decays()
    zero_state = jnp.zeros((bp, RET_HEADS, RET_D, RET_D), F32)

    w_router = jnp.concatenate(
        [w_group, jnp.transpose(w_route, (0, 2, 1, 3)).reshape(depth, d, N_EXPERTS),
         jnp.zeros((depth, d, LANES - N_GROUPS - N_EXPERTS), F32)], axis=2)
    b_router = jnp.concatenate(
        [b_group, b_route.reshape(depth, N_EXPERTS), jnp.zeros((depth, LANES - N_GROUPS - N_EXPERTS), F32)],
        axis=1)[:, None, :]
    w_up_e = w_up.reshape(depth * N_EXPERTS, d, 2 * D_EXPERT)
    w_down_e = w_down.reshape(depth * N_EXPERTS, D_EXPERT, d)

    sample_rows = SAMPLE_BATCH_TILE * n_tok
    ret_prompt = []
    state_all = state_ret.reshape(n_a * bs, RET_HEADS, RET_D, RET_D)
    ret_sample = jnp.zeros(state_all.shape, F32)
    kv = logf = None
    for l in range(depth):
        if l < n_a:
            h = proj_a(x, w_in_a[l].astype(BF16), cos2, sin2)
            tok_p, s_p = retention_prompt(h, zero_state, decays, bp, seq)
            hs = h[tp:, :2 * RET_W].reshape(bs // SAMPLE_BATCH_TILE, sample_rows, 2, RET_HEADS, RET_D)
            hs = jnp.transpose(hs, (2, 0, 3, 4, 1))
            tok_s, ret_sample = retention_sample(h, hs[0], hs[1], state_all, ret_sample, l, tp, bs, n_tok)
            ret_prompt.append(s_p)
            mq_block = (4 * RET_W) // MEM_W
        else:
            if l == n_a:
                w_kv_pad = jnp.concatenate(
                    [w_kv_shared, jnp.zeros((d, LANES - FOX_HEADS), F32)], axis=1).astype(BF16)
                bf_pad = jnp.concatenate([b_forget, jnp.zeros((LANES - FOX_HEADS,), F32)])[None, :]
                kv, logf = kv_shared(x, w_kv_pad, bf_pad)
                c_prompt = cumsum_rows(logf, bp, seq)
                k_aug, v_t = fox_prep(kv, c_prompt, tp)
                k_req, v_req, c_req = fox_gather(cache_fox_k, cache_fox_v, cache_fox_logf, page_table, n_tok)
            h = matmul(x, w_in_b[l - n_a].astype(BF16))
            tok_p = fox_prompt(h, c_prompt, k_aug, v_t, bp, seq)
            tok_s = fox_sample(h, kv, logf, k_req, v_req, c_req, tp, bs, n_tok)
            mq_block = FOX_W // MEM_W
        mem_p = mem_attention(h, mq_block, mem_kv_banks[:, :, l * 2 * MEM_W:(l * 2 + 1) * MEM_W],
                              mem_kv_banks[:, :, (l * 2 + 1) * MEM_W:(l * 2 + 2) * MEM_W], 0,
                              row_block0=0, n_steps=tp // ROW_TILE, rows=ROW_TILE, banks=1,
                              steps_per_bank=seq // ROW_TILE)
        mem_s = mem_attention(h, mq_block, cache_mk, cache_mv, l * bs,
                              row_block0=tp // sample_rows, n_steps=bs // SAMPLE_BATCH_TILE,
                              rows=sample_rows, banks=SAMPLE_BATCH_TILE, steps_per_bank=1)
        wt = tok_p.shape[1]
        x = out_proj_ln(tok_p, mem_p, tok_s, mem_s, x, w_o[l, :wt].astype(BF16), w_o[l, wt:].astype(BF16),
                        ln1_g[l][None, :], ln1_b[l][None, :], alpha)
        x = hierarchical_moe_ln(x, w_router[l], b_router[l], w_up_e, w_down_e, l * N_EXPERTS,
                                ln2_g[l][None, :], ln2_b[l][None, :], alpha)

    y_prompt = x[:tp].reshape(bp, seq, d)
    y_sample = x[tp:].reshape(bs, n_tok, d)
    fox_k = kv[:, :FOX_W].reshape(t_all, FOX_HEADS, FOX_HD)
    fox_v = kv[:, FOX_W:].reshape(t_all, FOX_HEADS, FOX_HD)
    fox_lf = logf[:, :FOX_HEADS]
    return (y_prompt, y_sample, jnp.stack(ret_prompt), ret_sample.reshape(state_ret.shape),
            fox_k[:tp].reshape(bp, seq, FOX_HEADS, FOX_HD), fox_v[:tp].reshape(bp, seq, FOX_HEADS, FOX_HD),
            fox_lf[:tp].reshape(bp, seq, FOX_HEADS),
            fox_k[tp:].reshape(bs, n_tok, FOX_HEADS, FOX_HD), fox_v[tp:].reshape(bs, n_tok, FOX_HEADS, FOX_HD),
            fox_lf[tp:].reshape(bs, n_tok, FOX_HEADS),
            mem_k_prompt, mem_v_prompt)
```

```python
import functools
import math

import jax
import jax.numpy as jnp
from jax import lax
from jax.experimental import pallas as pl
from jax.experimental.pallas import tpu as pltpu

F32 = jnp.float32
BF16 = jnp.bfloat16
HIGHEST = lax.Precision.HIGHEST

RET_HEADS = 6
RET_D = 128
RET_W = RET_HEADS * RET_D
RET_CHUNK = 128
ROPE_BASE = 10000.0
FOX_HEADS = 12
FOX_HD = 64
FOX_W = FOX_HEADS * FOX_HD
MEM_HEADS = 4
MEM_HD = 64
MEM_W = MEM_HEADS * MEM_HD
N_GROUPS = 4
EXPERTS_PER_GROUP = 8
N_EXPERTS = N_GROUPS * EXPERTS_PER_GROUP
D_EXPERT = 256
LN_EPS = 1e-5
HEAD_NORM_EPS = 1e-6

LANES = 128
SUBLANES = 8
VMEM_LIMIT_BYTES = 48 * 1024 * 1024

ROW_TILE = 512
MOE_TILE = 256
FOX_BLOCK = 512
SAMPLE_BATCH_TILE = 8


def _cparams(n_axes):
    return pltpu.CompilerParams(
        dimension_semantics=("arbitrary",) * n_axes, vmem_limit_bytes=VMEM_LIMIT_BYTES)


def _dot(a, b):
    return jnp.dot(a, b, preferred_element_type=F32)


def _dot_nt(a, b):
    return lax.dot_general(a, b, (((1,), (1,)), ((), ())), preferred_element_type=F32)


def _dot_tn(a, b):
    return lax.dot_general(a, b, (((0,), (0,)), ((), ())), preferred_element_type=F32)


def _layer_norm(y, g, b):
    mu = jnp.mean(y, axis=-1, keepdims=True)
    d = y - mu
    var = jnp.mean(d * d, axis=-1, keepdims=True)
    return d * lax.rsqrt(var + LN_EPS) * g + b


def _matmul_kernel(x_ref, w_ref, o_ref):
    o_ref[...] = _dot(x_ref[...].astype(BF16), w_ref[...])


def matmul(x, w_bf16):
    t, k = x.shape
    n = w_bf16.shape[1]
    tm = min(ROW_TILE, t)
    return pl.pallas_call(
        _matmul_kernel,
        out_shape=jax.ShapeDtypeStruct((t, n), F32),
        grid=(t // tm,),
        in_specs=[pl.BlockSpec((tm, k), lambda i: (i, 0)),
                  pl.BlockSpec((k, n), lambda i: (0, 0))],
        out_specs=pl.BlockSpec((tm, n), lambda i: (i, 0)),
        compiler_params=_cparams(1),
        name="matmul",
    )(x, w_bf16)


def _proj_a_kernel(x_ref, w_ref, cos_ref, sin_ref, o_ref, *, k_scale):
    x = x_ref[...].astype(BF16)
    cos2 = cos_ref[...]
    sin2 = sin_ref[...]
    for c in range(2 * RET_HEADS):
        cols = slice(c * RET_D, (c + 1) * RET_D)
        y = _dot(x, w_ref[:, cols])
        y = y * cos2 + pltpu.roll(y, RET_D // 2, 1) * sin2
        if c >= RET_HEADS:
            y = y * k_scale
        o_ref[:, cols] = y
    o_ref[:, 2 * RET_W:] = _dot(x, w_ref[:, 2 * RET_W:])


def proj_a(x, w_bf16, cos2, sin2):
    t, k = x.shape
    n = w_bf16.shape[1]
    tm = min(ROW_TILE, t)
    return pl.pallas_call(
        functools.partial(_proj_a_kernel, k_scale=RET_D ** -0.5),
        out_shape=jax.ShapeDtypeStruct((t, n), F32),
        grid=(t // tm,),
        in_specs=[pl.BlockSpec((tm, k), lambda i: (i, 0)),
                  pl.BlockSpec((k, n), lambda i: (0, 0)),
                  pl.BlockSpec((tm, RET_D), lambda i: (i, 0)),
                  pl.BlockSpec((tm, RET_D), lambda i: (i, 0))],
        out_specs=pl.BlockSpec((tm, n), lambda i: (i, 0)),
        compiler_params=_cparams(1),
        name="proj_a",
    )(x, w_bf16, cos2, sin2)


def _kv_kernel(x_ref, w_ref, bf_ref, kv_ref, logf_ref):
    x = x_ref[...].astype(BF16)
    kv_ref[...] = _dot(x, w_ref[:, :2 * FOX_W])
    z = _dot(x, w_ref[:, 2 * FOX_W:]) + bf_ref[...]
    logf_ref[...] = jnp.minimum(z, 0.0) - jnp.log(1.0 + jnp.exp(-jnp.abs(z)))


def kv_shared(x, w_pad_bf16, b_forget_pad):
    t, k = x.shape
    n = w_pad_bf16.shape[1]
    tm = min(ROW_TILE, t)
    return pl.pallas_call(
        _kv_kernel,
        out_shape=(jax.ShapeDtypeStruct((t, 2 * FOX_W), F32),
                   jax.ShapeDtypeStruct((t, LANES), F32)),
        grid=(t // tm,),
        in_specs=[pl.BlockSpec((tm, k), lambda i: (i, 0)),
                  pl.BlockSpec((k, n), lambda i: (0, 0)),
                  pl.BlockSpec((1, LANES), lambda i: (0, 0))],
        out_specs=(pl.BlockSpec((tm, 2 * FOX_W), lambda i: (i, 0)),
                   pl.BlockSpec((tm, LANES), lambda i: (i, 0))),
        compiler_params=_cparams(1),
        name="kv_shared",
    )(x, w_pad_bf16, b_forget_pad)


def _out_ln_kernel(tokp_ref, memp_ref, toks_ref, mems_ref, x_ref, w1_ref, w2_ref, g_ref, b_ref, o_ref,
                   *, alpha, prompt_tiles):
    def run(tok_ref, mem_ref):
        mixed = _dot(tok_ref[...].astype(BF16), w1_ref[...]) + _dot(mem_ref[...].astype(BF16), w2_ref[...])
        o_ref[...] = _layer_norm(alpha * x_ref[...] + mixed, g_ref[...], b_ref[...])

    @pl.when(pl.program_id(0) < prompt_tiles)
    def _():
        run(tokp_ref, memp_ref)

    @pl.when(pl.program_id(0) >= prompt_tiles)
    def _():
        run(toks_ref, mems_ref)


def out_proj_ln(tok_p, mem_p, tok_s, mem_s, x, w1_bf16, w2_bf16, g, b, alpha):
    t, d = x.shape
    tm = min(ROW_TILE, t)
    wt, wm = tok_p.shape[1], mem_p.shape[1]
    n_p = tok_p.shape[0] // tm
    n_s = tok_s.shape[0] // tm

    def prompt_map(i):
        return (jnp.minimum(i, n_p - 1), 0)

    def sample_map(i):
        return (jnp.clip(i - n_p, 0, n_s - 1), 0)

    return pl.pallas_call(
        functools.partial(_out_ln_kernel, alpha=alpha, prompt_tiles=n_p),
        out_shape=jax.ShapeDtypeStruct((t, d), F32),
        grid=(t // tm,),
        in_specs=[pl.BlockSpec((tm, wt), prompt_map),
                  pl.BlockSpec((tm, wm), prompt_map),
                  pl.BlockSpec((tm, wt), sample_map),
                  pl.BlockSpec((tm, wm), sample_map),
                  pl.BlockSpec((tm, d), lambda i: (i, 0)),
                  pl.BlockSpec((wt, d), lambda i: (0, 0)),
                  pl.BlockSpec((wm, d), lambda i: (0, 0)),
                  pl.BlockSpec((1, d), lambda i: (0, 0)),
                  pl.BlockSpec((1, d), lambda i: (0, 0))],
        out_specs=pl.BlockSpec((tm, d), lambda i: (i, 0)),
        compiler_params=_cparams(1),
        name="out_proj_ln",
    )(tok_p, mem_p, tok_s, mem_s, x, w1_bf16, w2_bf16, g, b)


def _head_norm_gate(r, g):
    mu = jnp.mean(r, axis=-1, keepdims=True)
    d = r - mu
    var = jnp.mean(d * d, axis=-1, keepdims=True)
    return d * lax.rsqrt(var + HEAD_NORM_EPS) * (g * jax.nn.sigmoid(g))


def _retention_prompt_kernel(q_ref, k_ref, v_ref, g_ref, s0_ref, dm_ref, qd_ref, kd_ref, cd_ref,
                             o_ref, sout_ref, s_scr, *, chunks):
    i = pl.program_id(1)

    @pl.when(i == 0)
    def _():
        s_scr[...] = s0_ref[0]

    for c in range(chunks):
        rows = slice(c * RET_CHUNK, (c + 1) * RET_CHUNK)
        for h in range(RET_HEADS):
            cols = slice(h * RET_D, (h + 1) * RET_D)
            q = q_ref[rows, cols]
            k = k_ref[rows, cols]
            v = v_ref[rows, cols].astype(BF16)
            s_prev = s_scr[h]
            scores = _dot_nt(q.astype(BF16), k.astype(BF16)) * dm_ref[h]
            intra = _dot(scores.astype(BF16), v)
            cross = _dot((q * qd_ref[h]).astype(BF16), s_prev.astype(BF16))
            kd = (k * kd_ref[h]).astype(BF16)
            s_scr[h] = cd_ref[h] * s_prev + _dot_tn(kd, v)
            o_ref[rows, cols] = _head_norm_gate(intra + cross, g_ref[rows, cols])

    @pl.when(i == pl.num_programs(1) - 1)
    def _():
        sout_ref[0] = s_scr[...]


def retention_prompt(h, s0, decays, batch, seq):
    chunks = math.gcd(4, seq // RET_CHUNK)
    rows = chunks * RET_CHUNK
    n_i = seq // rows
    dm, qd, kd, cd = decays
    const = pl.BlockSpec((RET_HEADS, RET_CHUNK, RET_D), lambda b, i: (0, 0, 0))
    state = pl.BlockSpec((1, RET_HEADS, RET_D, RET_D), lambda b, i: (b, 0, 0, 0))

    def col(j):
        return pl.BlockSpec((rows, RET_W), lambda b, i, j=j: (b * n_i + i, j))

    return pl.pallas_call(
        functools.partial(_retention_prompt_kernel, chunks=chunks),
        out_shape=(jax.ShapeDtypeStruct((batch * seq, RET_W), F32),
                   jax.ShapeDtypeStruct((batch, RET_HEADS, RET_D, RET_D), F32)),
        grid=(batch, n_i),
        in_specs=[col(0), col(1), col(2), col(3), state, const, const, const, const],
        out_specs=(pl.BlockSpec((rows, RET_W), lambda b, i: (b * n_i + i, 0)), state),
        scratch_shapes=[pltpu.VMEM((RET_HEADS, RET_D, RET_D), F32)],
        compiler_params=_cparams(2),
        name="retention_prompt",
    )(h, h, h, h, s0, dm, qd, kd, cd)


def _retention_sample_kernel(qt_ref, kt_ref, v_ref, g_ref, s0_ref, sprev_ref, o_ref, sout_ref, r_scr,
                             *, n_req, n_tok, gammas):
    del sprev_ref
    for j in range(n_req):
        for h in range(RET_HEADS):
            cols = slice(h * RET_D, (h + 1) * RET_D)
            s = s0_ref[j, h]
            for t in range(n_tok):
                r = j * n_tok + t
                s = gammas[h] * s + kt_ref[0, h, :, r:r + 1] * v_ref[r:r + 1, cols]
                r_scr[r:r + 1, cols] = jnp.sum(qt_ref[0, h, :, r:r + 1] * s, axis=0, keepdims=True)
            sout_ref[j, h] = s
    for h in range(RET_HEADS):
        cols = slice(h * RET_D, (h + 1) * RET_D)
        o_ref[:, cols] = _head_norm_gate(r_scr[:, cols], g_ref[:, cols])


def retention_sample(h, qt, kt, s_all, s_new_all, layer, row0, n_req_total, n_tok):
    n_req = SAMPLE_BATCH_TILE
    rows = n_req * n_tok
    steps = n_req_total // n_req
    blk0 = row0 // rows
    gammas = tuple(1.0 - 2.0 ** (-5.0 - hh) for hh in range(RET_HEADS))
    state = pl.BlockSpec((n_req, RET_HEADS, RET_D, RET_D), lambda i: (layer * steps + i, 0, 0, 0))
    tr = pl.BlockSpec((1, RET_HEADS, RET_D, rows), lambda i: (i, 0, 0, 0))
    return pl.pallas_call(
        functools.partial(_retention_sample_kernel, n_req=n_req, n_tok=n_tok, gammas=gammas),
        out_shape=(jax.ShapeDtypeStruct((n_req_total * n_tok, RET_W), F32),
                   jax.ShapeDtypeStruct(s_new_all.shape, F32)),
        grid=(steps,),
        in_specs=[tr, tr,
                  pl.BlockSpec((rows, RET_W), lambda i: (blk0 + i, 2)),
                  pl.BlockSpec((rows, RET_W), lambda i: (blk0 + i, 3)),
                  state,
                  pl.BlockSpec(memory_space=pl.ANY)],
        out_specs=(pl.BlockSpec((rows, RET_W), lambda i: (i, 0)), state),
        scratch_shapes=[pltpu.VMEM((rows, RET_W), F32)],
        input_output_aliases={5: 1},
        compiler_params=_cparams(1),
        name="retention_sample",
    )(qt, kt, h, h, s_all, s_new_all)


def _mem_attn_kernel(q_ref, mk_ref, mv_ref, o_ref, *, banks, rows_per_bank):
    q = q_ref[...] * (MEM_HD ** -0.5)
    n_rows = q.shape[0]
    n_cols = max(n_rows, LANES)
    if n_cols > n_rows:
        q = jnp.concatenate([q, jnp.zeros((n_cols - n_rows, MEM_W), F32)], axis=0)
    qt = q.T.astype(BF16)
    out_t = None
    for j in range(banks):
        heads = []
        for h in range(MEM_HEADS):
            mk = mk_ref[j, h].T.astype(BF16)
            s = _dot(mk, qt[h * MEM_HD:(h + 1) * MEM_HD, :])
            p = jnp.exp(s - jnp.max(s, axis=0, keepdims=True))
            p = p / jnp.sum(p, axis=0, keepdims=True)
            heads.append(_dot(mv_ref[j, h].astype(BF16), p.astype(BF16)))
        o_t = jnp.concatenate(heads, axis=0)
        if out_t is None:
            out_t = o_t
        else:
            col = lax.broadcasted_iota(jnp.int32, (MEM_W, n_cols), 1)
            out_t = jnp.where(col >= j * rows_per_bank, o_t, out_t)
    o_ref[...] = out_t.T[:n_rows, :]


def mem_attention(h, q_col_block, mkt, mvt, k_block, v_block, *, row_block0, n_steps, rows, banks):
    mem_len = mkt.shape[3]
    bank_shape = (banks, MEM_HEADS, MEM_HD, mem_len)
    return pl.pallas_call(
        functools.partial(_mem_attn_kernel, banks=banks, rows_per_bank=rows // banks),
        out_shape=jax.ShapeDtypeStruct((n_steps * rows, MEM_W), F32),
        grid=(n_steps,),
        in_specs=[pl.BlockSpec((rows, MEM_W), lambda i: (row_block0 + i, q_col_block)),
                  pl.BlockSpec(bank_shape, lambda i: (k_block(i), 0, 0, 0)),
                  pl.BlockSpec(bank_shape, lambda i: (v_block(i), 0, 0, 0))],
        out_specs=pl.BlockSpec((rows, MEM_W), lambda i: (i, 0)),
        compiler_params=_cparams(1),
        name="mem_attention",
    )(h, mkt, mvt)


def _mem_kv_kernel(x_ref, wt_ref, o_ref):
    o_ref[0] = _dot_nt(wt_ref[...], x_ref[...].astype(BF16))


def mem_kv_transposed(mem, w_t_bf16):
    b, m, d = mem.shape
    f = w_t_bf16.shape[0]
    return pl.pallas_call(
        _mem_kv_kernel,
        out_shape=jax.ShapeDtypeStruct((b, f, m), F32),
        grid=(b,),
        in_specs=[pl.BlockSpec((m, d), lambda i: (i, 0)),
                  pl.BlockSpec((f, d), lambda i: (0, 0))],
        out_specs=pl.BlockSpec((1, f, m), lambda i: (i, 0, 0)),
        compiler_params=_cparams(1),
        name="mem_kv",
    )(mem.reshape(b * m, d), w_t_bf16)


def _cumsum_kernel(x_ref, o_ref, carry):
    @pl.when(pl.program_id(1) == 0)
    def _():
        carry[...] = jnp.zeros_like(carry)

    n = x_ref.shape[0]
    row = lax.broadcasted_iota(jnp.int32, (n, n), 0)
    col = lax.broadcasted_iota(jnp.int32, (n, n), 1)
    tri = (row >= col).astype(F32)
    c = jnp.dot(tri, x_ref[...], precision=HIGHEST, preferred_element_type=F32) + carry[...]
    o_ref[...] = c
    carry[...] = c[n - 1:n, :]


def cumsum_rows(x, batch, seq):
    tm = min(ROW_TILE, seq)
    n_i = seq // tm
    return pl.pallas_call(
        _cumsum_kernel,
        out_shape=jax.ShapeDtypeStruct((batch * seq, x.shape[1]), F32),
        grid=(batch, n_i),
        in_specs=[pl.BlockSpec((tm, x.shape[1]), lambda b, i: (b * n_i + i, 0))],
        out_specs=pl.BlockSpec((tm, x.shape[1]), lambda b, i: (b * n_i + i, 0)),
        scratch_shapes=[pltpu.VMEM((1, x.shape[1]), F32)],
        compiler_params=_cparams(2),
        name="cumsum_rows",
    )(x)


BIAS_COLS = 12


def _split3(x):
    hi = x.astype(BF16).astype(F32)
    mid = (x - hi).astype(BF16).astype(F32)
    lo = x - hi - mid
    return hi, mid, lo


def _place(lane, base, parts):
    out = jnp.zeros(lane.shape, F32)
    for idx, part in enumerate(parts):
        out = jnp.where(lane == base + idx, part, out)
    return out


def _fox_prep_kernel(k_ref, v_ref, c_ref, ka_ref, vt_ref, ktf_ref, vtf_ref):
    p = pl.program_id(0)
    rows = k_ref.shape[0]
    lane = lax.broadcasted_iota(jnp.int32, (rows, LANES), 1)
    c2 = c_ref[...]
    bias = jnp.zeros((rows, LANES), F32)
    for e in range(2):
        ck = jnp.sum(jnp.where(lane == 2 * p + e, c2, 0.0), axis=1, keepdims=True)
        base = e * (BIAS_COLS // 2)
        ones = jnp.where((lane >= base) & (lane < base + 3), 1.0, 0.0)
        bias = bias + ones + _place(lane, base + 3, [-part for part in _split3(ck)])
    ka_ref[0] = jnp.concatenate([k_ref[...], bias], axis=1).astype(BF16)
    v_t = v_ref[...].T
    vt_ref[0] = v_t.astype(BF16)
    vtf_ref[0] = v_t
    ktf_ref[0] = k_ref[...].T


def fox_prep(kv, c, batch, seq):
    pairs = FOX_HEADS // 2
    n_rows = batch * seq
    tm = min(FOX_BLOCK, seq)
    per_b = seq // tm
    t_spec = pl.BlockSpec((1, LANES, tm), lambda p, i: (i // per_b, p, i % per_b))
    return pl.pallas_call(
        _fox_prep_kernel,
        out_shape=(jax.ShapeDtypeStruct((pairs, n_rows, 2 * LANES), BF16),
                   jax.ShapeDtypeStruct((pairs, LANES, n_rows), BF16),
                   jax.ShapeDtypeStruct((batch, FOX_W, seq), F32),
                   jax.ShapeDtypeStruct((batch, FOX_W, seq), F32)),
        grid=(pairs, n_rows // tm),
        in_specs=[pl.BlockSpec((tm, LANES), lambda p, i: (i, p)),
                  pl.BlockSpec((tm, LANES), lambda p, i: (i, pairs + p)),
                  pl.BlockSpec((tm, LANES), lambda p, i: (i, 0))],
        out_specs=(pl.BlockSpec((1, tm, 2 * LANES), lambda p, i: (p, i, 0)),
                   pl.BlockSpec((1, LANES, tm), lambda p, i: (p, 0, i)),
                   t_spec, t_spec),
        compiler_params=_cparams(2),
        name="fox_prep",
    )(kv, kv, c)


def _fox_prompt_kernel(q_ref, c_ref, ka_ref, vt_ref, o_ref, acc_a, acc_b, *, blk):
    p = pl.program_id(1)
    i = pl.program_id(2)
    lane = lax.broadcasted_iota(jnp.int32, (blk, LANES), 1)
    q2 = q_ref[...] * (FOX_HD ** -0.5)
    c2 = c_ref[...]
    qts = []
    for e in range(2):
        head_lanes = (lane >= e * FOX_HD) & (lane < (e + 1) * FOX_HD)
        cq = jnp.sum(jnp.where(lane == 2 * p + e, c2, 0.0), axis=1, keepdims=True)
        base = e * (BIAS_COLS // 2)
        ones = jnp.where((lane >= base + 3) & (lane < base + 6), 1.0, 0.0)
        q_aug = jnp.concatenate([jnp.where(head_lanes, q2, 0.0), ones + _place(lane, base, _split3(cq))], axis=1)
        qts.append(q_aug.T.astype(BF16))
    accs = (acc_a, acc_b)
    acc_a[...] = jnp.zeros(acc_a.shape, F32)
    acc_b[...] = jnp.zeros(acc_b.shape, F32)

    def block(j, carry, masked):
        start = pl.multiple_of(j * blk, blk)
        kb = ka_ref[0, pl.ds(start, blk), :]
        vt = vt_ref[0, :, pl.ds(start, blk)]
        new = []
        for e in range(2):
            m_old, l_old = carry[2 * e], carry[2 * e + 1]
            s = _dot(kb, qts[e])
            if masked:
                key = lax.broadcasted_iota(jnp.int32, (blk, blk), 0)
                qry = lax.broadcasted_iota(jnp.int32, (blk, blk), 1)
                s = jnp.where(key <= qry, s, -jnp.inf)
            m_new = jnp.maximum(m_old, jnp.max(s, axis=0, keepdims=True))
            a = jnp.exp(m_old - m_new)
            pe = jnp.exp(s - m_new)
            new += [m_new, a * l_old + jnp.sum(pe, axis=0, keepdims=True)]
            acc = accs[e]
            acc[...] = a * acc[...] + _dot(vt[e * FOX_HD:(e + 1) * FOX_HD, :], pe.astype(BF16))
        return tuple(new)

    init = (jnp.full((1, blk), -jnp.inf, F32), jnp.zeros((1, blk), F32)) * 2
    carry = lax.fori_loop(0, i, lambda j, c: block(j, c, False), init)
    carry = block(i, carry, True)
    out_t = jnp.concatenate([acc_a[...] / carry[1], acc_b[...] / carry[3]], axis=0)
    o_ref[...] = out_t.T


def fox_prompt(h, c, k_aug, v_t, batch, seq):
    blk = min(FOX_BLOCK, seq)
    n_q = seq // blk
    pairs = FOX_HEADS // 2
    return pl.pallas_call(
        functools.partial(_fox_prompt_kernel, blk=blk),
        out_shape=jax.ShapeDtypeStruct((batch * seq, FOX_W), F32),
        grid=(batch, pairs, n_q),
        in_specs=[pl.BlockSpec((blk, LANES), lambda b, p, i: (b * n_q + i, p)),
                  pl.BlockSpec((blk, LANES), lambda b, p, i: (b * n_q + i, 0)),
                  pl.BlockSpec((1, seq, 2 * LANES), lambda b, p, i: (p, b, 0)),
                  pl.BlockSpec((1, LANES, seq), lambda b, p, i: (p, 0, b))],
        out_specs=pl.BlockSpec((blk, LANES), lambda b, p, i: (b * n_q + i, p)),
        scratch_shapes=[pltpu.VMEM((FOX_HD, blk), F32), pltpu.VMEM((FOX_HD, blk), F32)],
        compiler_params=_cparams(3),
        name="fox_prompt",
    )(h, c, k_aug, v_t)


FOX_COLS = 16


def _fox_sample_kernel(pt_ref, q_ref, kvn_ref, lfn_ref, *refs, n_pages, page, n_tok):
    del pt_ref
    kt_refs = refs[:n_pages]
    vt_refs = refs[n_pages:2 * n_pages]
    lf_refs = refs[2 * n_pages:3 * n_pages]
    o_ref = refs[3 * n_pages]
    n_rows = n_tok * FOX_COLS
    reqs = SUBLANES // n_tok
    which = pl.program_id(0) % reqs

    def pick(block):
        out = block[0:n_tok, :]
        for w in range(1, reqs):
            out = jnp.where(which == w, block[w * n_tok:(w + 1) * n_tok, :], out)
        return out

    q = pick(q_ref[...])[:, :FOX_W] * (FOX_HD ** -0.5)
    kvn = pick(kvn_ref[...])
    lfn = pick(lfn_ref[...])
    head_of_lane = lax.broadcasted_iota(jnp.int32, (FOX_COLS, FOX_W), 1) // FOX_HD
    head_of_row = lax.broadcasted_iota(jnp.int32, (FOX_COLS, FOX_W), 0)
    q_rows = [jnp.where(head_of_lane == head_of_row, jnp.broadcast_to(q[t:t + 1, :], (FOX_COLS, FOX_W)), 0.0)
              for t in range(n_tok)]
    q_bd = jnp.concatenate(q_rows, axis=0).astype(BF16)

    kt = jnp.concatenate([r[0].reshape(FOX_W, page) for r in kt_refs], axis=1).astype(BF16)
    s = _dot(q_bd, kt)

    t_row = lax.broadcasted_iota(jnp.int32, (page, page), 0)
    t_col = lax.broadcasted_iota(jnp.int32, (page, page), 1)
    upto = (t_row <= t_col).astype(F32)
    carry = jnp.zeros((FOX_COLS, 1), F32)
    c_pages = []
    for pg in range(n_pages):
        c_pg = jnp.dot(lf_refs[pg][0], upto, precision=HIGHEST, preferred_element_type=F32) + carry
        carry = c_pg[:, page - 1:page]
        c_pages.append(c_pg)
    c_past = jnp.concatenate(c_pages, axis=1)

    lfn_t = jnp.concatenate([lfn, jnp.zeros((SUBLANES - n_tok, LANES), F32)], axis=0).T[:FOX_COLS, :]
    c_new = []
    for t in range(n_tok):
        carry = carry + lfn_t[:, t:t + 1]
        c_new.append(carry)
    cq = jnp.concatenate(c_new, axis=0)
    s = s + (cq - jnp.concatenate([c_past] * n_tok, axis=0))

    s_new = _dot_nt(q_bd, kvn[:, :FOX_W].astype(BF16))
    ck_new = jnp.concatenate([jnp.concatenate([c] * n_tok, axis=0) for c in c_new], axis=1)
    tok_of_row = lax.broadcasted_iota(jnp.int32, (n_rows, n_tok), 0) // FOX_COLS
    key_tok = lax.broadcasted_iota(jnp.int32, (n_rows, n_tok), 1)
    s_new = jnp.where(key_tok <= tok_of_row, s_new + (cq - ck_new), -jnp.inf)

    m = jnp.maximum(jnp.max(s, axis=1, keepdims=True), jnp.max(s_new, axis=1, keepdims=True))
    p = jnp.exp(s - m)
    p_new = jnp.exp(s_new - m)
    inv = 1.0 / (jnp.sum(p, axis=1, keepdims=True) + jnp.sum(p_new, axis=1, keepdims=True))
    vt = jnp.concatenate([r[0].reshape(FOX_W, page) for r in vt_refs], axis=1).astype(BF16)
    acc = _dot_nt((p * inv).astype(BF16), vt)
    p_new = p_new * inv
    v_new = kvn[:, FOX_W:]
    for t in range(n_tok):
        acc = acc + p_new[:, t:t + 1] * v_new[t:t + 1, :]

    a_row = lax.broadcasted_iota(jnp.int32, (n_rows, FOX_W), 0)
    a_col = lax.broadcasted_iota(jnp.int32, (n_rows, FOX_W), 1)
    acc = jnp.where(a_col // FOX_HD == a_row % FOX_COLS, acc, 0.0)
    out = acc.reshape(n_tok, FOX_COLS, FOX_W).sum(axis=1)
    for w in range(reqs):
        @pl.when(which == w)
        def _(w=w):
            o_ref[w * n_tok:(w + 1) * n_tok, :] = out


def fox_sample(h, kv, logf, cache_kt, cache_vt, cache_lf, page_table, row0, n_req, n_tok):
    n_pages = page_table.shape[1]
    page = cache_kt.shape[3]
    reqs = SUBLANES // n_tok
    blk0 = row0 // SUBLANES

    def new_spec(width):
        return pl.BlockSpec((SUBLANES, width), lambda r, pt: (blk0 + r // reqs, 0))

    def kv_spec(pg):
        return pl.BlockSpec((1, FOX_HEADS, FOX_HD, page), lambda r, pt, pg=pg: (pt[r * n_pages + pg], 0, 0, 0))

    def lf_spec(pg):
        return pl.BlockSpec((1, FOX_COLS, page), lambda r, pt, pg=pg: (pt[r * n_pages + pg], 0, 0))

    grid_spec = pltpu.PrefetchScalarGridSpec(
        num_scalar_prefetch=1,
        grid=(n_req,),
        in_specs=[new_spec(h.shape[1]), new_spec(kv.shape[1]), new_spec(logf.shape[1])]
        + [kv_spec(pg) for pg in range(n_pages)] + [kv_spec(pg) for pg in range(n_pages)]
        + [lf_spec(pg) for pg in range(n_pages)],
        out_specs=pl.BlockSpec((SUBLANES, FOX_W), lambda r, pt: (r // reqs, 0)),
    )
    return pl.pallas_call(
        functools.partial(_fox_sample_kernel, n_pages=n_pages, page=page, n_tok=n_tok),
        out_shape=jax.ShapeDtypeStruct((n_req * n_tok, FOX_W), F32),
        grid_spec=grid_spec,
        compiler_params=_cparams(1),
        name="fox_sample",
    )(page_table.reshape(-1), h, kv, logf, *([cache_kt] * n_pages), *([cache_vt] * n_pages),
      *([cache_lf] * n_pages))


def _router_kernel(x_ref, w_ref, b_ref, o_ref, cnt_ref, carry):
    @pl.when(pl.program_id(0) == 0)
    def _():
        carry[...] = jnp.zeros_like(carry)

    logits = jnp.dot(x_ref[...], w_ref[...], precision=HIGHEST, preferred_element_type=F32) + b_ref[...]
    lane = lax.broadcasted_iota(jnp.int32, logits.shape, 1)
    big = jnp.int32(LANES)
    neg = -jnp.inf
    gl = jnp.where(lane < N_GROUPS, logits, neg)
    g_max = jnp.max(gl, axis=-1, keepdims=True)
    g_idx = jnp.min(jnp.where(gl == g_max, lane, big), axis=-1, keepdims=True)
    g_w = 1.0 / jnp.sum(jnp.exp(gl - g_max), axis=-1, keepdims=True)
    in_group = (lane >= N_GROUPS) & (lane < N_GROUPS + N_EXPERTS) & (((lane - N_GROUPS) >> 3) == g_idx)
    el = jnp.where(in_group, logits, neg)
    v1 = jnp.max(el, axis=-1, keepdims=True)
    i1 = jnp.min(jnp.where(el == v1, lane, big), axis=-1, keepdims=True)
    el2 = jnp.where(lane == i1, neg, el)
    v2 = jnp.max(el2, axis=-1, keepdims=True)
    i2 = jnp.min(jnp.where(el2 == v2, lane, big), axis=-1, keepdims=True)
    e2 = jnp.exp(v2 - v1)
    w1 = g_w / (1.0 + e2)
    w2 = g_w * e2 / (1.0 + e2)
    tm = logits.shape[0]
    e1 = i1 - N_GROUPS
    e2 = i2 - N_GROUPS
    oh1 = jnp.where(lane == e1, 1.0, 0.0)
    oh2 = jnp.where(lane == e2, 1.0, 0.0)
    row = lax.broadcasted_iota(jnp.int32, (tm, tm), 0)
    col = lax.broadcasted_iota(jnp.int32, (tm, tm), 1)
    before = jnp.where(col < row, 1.0, 0.0).astype(BF16)
    tot1 = jnp.sum(oh1, axis=0, keepdims=True)
    base = carry[...]
    r1 = jnp.sum(oh1 * (base + _dot(before, oh1.astype(BF16))), axis=-1, keepdims=True)
    r2 = jnp.sum(oh2 * (base + tot1 + _dot(before, oh2.astype(BF16))), axis=-1, keepdims=True)
    carry[...] = base + tot1 + jnp.sum(oh2, axis=0, keepdims=True)
    cnt_ref[...] = carry[...]
    out = jnp.zeros(logits.shape, F32)
    for idx, val in enumerate((e1.astype(F32), e2.astype(F32), w1, w2, r1, r2)):
        out = jnp.where(lane == idx, val, out)
    o_ref[...] = out


def router(x, w_pad, b_pad):
    t, d = x.shape
    tm = min(ROW_TILE, t)
    return pl.pallas_call(
        _router_kernel,
        out_shape=(jax.ShapeDtypeStruct((t, LANES), F32), jax.ShapeDtypeStruct((1, LANES), F32)),
        grid=(t // tm,),
        in_specs=[pl.BlockSpec((tm, d), lambda i: (i, 0)),
                  pl.BlockSpec((d, LANES), lambda i: (0, 0)),
                  pl.BlockSpec((1, LANES), lambda i: (0, 0))],
        out_specs=(pl.BlockSpec((tm, LANES), lambda i: (i, 0)), pl.BlockSpec((1, LANES), lambda i: (0, 0))),
        scratch_shapes=[pltpu.VMEM((1, LANES), F32)],
        compiler_params=_cparams(1),
        name="router",
    )(x, w_pad, b_pad)


def _pack_rows(y):
    n = y.shape[1] // 2
    bits = lax.bitcast_convert_type(y.astype(BF16).astype(F32), jnp.uint32)
    return bits[:, :n] | (bits[:, n:] >> 16)


def _unpack_rows(w):
    return (lax.bitcast_convert_type(w & jnp.uint32(0xFFFF0000), F32),
            lax.bitcast_convert_type(w << 16, F32))


def _row_copy(src_ref, src_row, dst_ref, dst_row, sem):
    return pltpu.make_async_copy(src_ref.at[pl.ds(src_row, 1)], dst_ref.at[pl.ds(dst_row, 1)], sem)


def _drain_rows(src_ref, dst_ref, sem, n_rows):
    def drain(r, carry):
        _row_copy(src_ref, 0, dst_ref, 0, sem).wait()
        return carry

    lax.fori_loop(0, n_rows, drain, 0, unroll=8)


def _dispatch_kernel(pos_ref, x_ref, xs_in_ref, xs_ref, buf, sems, *, tm):
    del xs_in_ref
    i = pl.program_id(0)
    n = pl.num_programs(0)
    for slot in range(2):
        @pl.when(i % 2 == slot)
        def _(slot=slot):
            @pl.when(i >= 2)
            def _():
                _drain_rows(buf.at[slot], xs_ref, sems.at[slot], 2 * tm)

            buf[slot] = _pack_rows(x_ref[...])

            def issue(r, carry):
                for k in range(2):
                    _row_copy(buf.at[slot], r, xs_ref, pos_ref[0, 0, k * tm + r], sems.at[slot]).start(priority=k)
                return carry

            lax.fori_loop(0, tm, issue, 0, unroll=8)

            @pl.when(i == n - 1)
            def _():
                _drain_rows(buf.at[slot], xs_ref, sems.at[slot], 2 * tm)

                @pl.when(n >= 2)
                def _():
                    _drain_rows(buf.at[1 - slot], xs_ref, sems.at[1 - slot], 2 * tm)


def moe_dispatch(x, pos3, xs_init):
    t, d = x.shape
    tm = MOE_TILE
    return pl.pallas_call(
        functools.partial(_dispatch_kernel, tm=tm),
        out_shape=jax.ShapeDtypeStruct(xs_init.shape, jnp.uint32),
        grid=(t // tm,),
        in_specs=[pl.BlockSpec((1, 1, 2 * tm), lambda i: (i, 0, 0), memory_space=pltpu.SMEM),
                  pl.BlockSpec((tm, d), lambda i: (i, 0)),
                  pl.BlockSpec(memory_space=pl.ANY)],
        out_specs=pl.BlockSpec(memory_space=pl.ANY),
        scratch_shapes=[pltpu.VMEM((2, tm, d // 2), jnp.uint32), pltpu.SemaphoreType.DMA((2,))],
        input_output_aliases={2: 0},
        compiler_params=_cparams(1),
        name="moe_dispatch",
    )(pos3, x, xs_init)


def _expert_kernel(te_ref, na_ref, x_ref, wu_ref, wd_ref, o_ref, wu_bf, wd_bf):
    i = pl.program_id(0)
    active = i < na_ref[0]
    changed = jnp.logical_or(i == 0, te_ref[i] != te_ref[jnp.maximum(i - 1, 0)])

    @pl.when(jnp.logical_and(active, changed))
    def _():
        wu_bf[...] = wu_ref[0].astype(BF16)
        wd_bf[...] = wd_ref[0].astype(BF16)

    @pl.when(active)
    def _():
        half = wu_bf.shape[0] // 2
        xa, xb = _unpack_rows(x_ref[...])
        up = _dot(xa.astype(BF16), wu_bf[:half, :]) + _dot(xb.astype(BF16), wu_bf[half:, :])
        hid = jax.nn.silu(up[:, :D_EXPERT]) * up[:, D_EXPERT:]
        o_ref[...] = _dot(hid.astype(BF16), wd_bf[...])

    @pl.when(jnp.logical_not(active))
    def _():
        o_ref[...] = jnp.zeros_like(o_ref)


def moe_experts(xs, w_up, w_down, tile_expert, n_active):
    r = xs.shape[0]
    d = w_up.shape[1]
    tm = MOE_TILE
    n_tiles = r // tm

    def row_map(i, te, na):
        return (jnp.minimum(i, na[0] - 1), 0)

    grid_spec = pltpu.PrefetchScalarGridSpec(
        num_scalar_prefetch=2,
        grid=(n_tiles,),
        in_specs=[pl.BlockSpec((tm, d // 2), row_map),
                  pl.BlockSpec((1, d, 2 * D_EXPERT), lambda i, te, na: (te[i], 0, 0)),
                  pl.BlockSpec((1, D_EXPERT, d), lambda i, te, na: (te[i], 0, 0))],
        out_specs=pl.BlockSpec((tm, d), lambda i, te, na: (i, 0)),
        scratch_shapes=[pltpu.VMEM((d, 2 * D_EXPERT), BF16), pltpu.VMEM((D_EXPERT, d), BF16)],
    )
    return pl.pallas_call(
        _expert_kernel,
        out_shape=jax.ShapeDtypeStruct((r, d), F32),
        grid_spec=grid_spec,
        compiler_params=_cparams(1),
        name="moe_experts",
    )(tile_expert, n_active, xs, w_up, w_down)


def _combine_kernel(pos_ref, posn_ref, x_ref, rt_ref, g_ref, b_ref, ys_ref, o_ref, buf, sems, *, tm, alpha):
    i = pl.program_id(0)
    n = pl.num_programs(0)

    def gather(p_ref, slot):
        def issue(r, carry):
            for k in range(2):
                _row_copy(ys_ref, p_ref[0, 0, k * tm + r], buf.at[slot, k], r, sems.at[slot]).start(priority=k)
            return carry

        lax.fori_loop(0, tm, issue, 0, unroll=8)

    @pl.when(i == 0)
    def _():
        gather(pos_ref, 0)

    for slot in range(2):
        @pl.when(i % 2 == slot)
        def _(slot=slot):
            @pl.when(i + 1 < n)
            def _():
                gather(posn_ref, 1 - slot)

            _drain_rows(ys_ref, buf.at[slot, 0], sems.at[slot], 2 * tm)
            rt = rt_ref[...]
            ffn = rt[:, 2:3] * buf[slot, 0] + rt[:, 3:4] * buf[slot, 1]
            o_ref[...] = _layer_norm(alpha * x_ref[...] + ffn, g_ref[...], b_ref[...])


def moe_combine(x, routing, ys, pos3, g, b, alpha):
    t, d = x.shape
    tm = MOE_TILE
    n = t // tm
    return pl.pallas_call(
        functools.partial(_combine_kernel, tm=tm, alpha=alpha),
        out_shape=jax.ShapeDtypeStruct((t, d), F32),
        grid=(n,),
        in_specs=[pl.BlockSpec((1, 1, 2 * tm), lambda i: (i, 0, 0), memory_space=pltpu.SMEM),
                  pl.BlockSpec((1, 1, 2 * tm), lambda i: (jnp.minimum(i + 1, n - 1), 0, 0),
                               memory_space=pltpu.SMEM),
                  pl.BlockSpec((tm, d), lambda i: (i, 0)),
                  pl.BlockSpec((tm, LANES), lambda i: (i, 0)),
                  pl.BlockSpec((1, d), lambda i: (0, 0)),
                  pl.BlockSpec((1, d), lambda i: (0, 0)),
                  pl.BlockSpec(memory_space=pl.ANY)],
        out_specs=pl.BlockSpec((tm, d), lambda i: (i, 0)),
        scratch_shapes=[pltpu.VMEM((2, 2, tm, d), F32), pltpu.SemaphoreType.DMA((2,))],
        compiler_params=_cparams(1),
        name="moe_combine",
    )(pos3, pos3, x, routing, g, b, ys)


def moe_plan(routing, counts, n_tiles, expert0):
    t = routing.shape[0]
    tm = MOE_TILE
    experts = jnp.arange(N_EXPERTS, dtype=jnp.int32)
    cnt = counts[0, :N_EXPERTS].astype(jnp.int32)
    tiles_e = (cnt + tm - 1) // tm
    tile_end = jnp.cumsum(tiles_e)
    offs = (tile_end - tiles_e) * tm
    ids = routing[:, 0:2].astype(jnp.int32)
    rank = routing[:, 4:6].astype(jnp.int32)
    pos = jnp.sum(jnp.where(ids[:, :, None] == experts, offs, 0), axis=-1) + rank
    n_active = tile_end[-1:]
    tile_ids = jnp.minimum(jnp.arange(n_tiles, dtype=jnp.int32), n_active[0] - 1)
    tile_expert = jnp.minimum(jnp.sum((tile_ids[:, None] >= tile_end[None, :]).astype(jnp.int32), axis=1),
                              N_EXPERTS - 1) + expert0
    pos3 = pos.reshape(t // tm, tm, 2).transpose(0, 2, 1).reshape(t // tm, 1, 2 * tm)
    return pos3, tile_expert, n_active


def hierarchical_moe_ln(x, w_router, b_router, w_up, w_down, expert0, g, b, alpha):
    t, d = x.shape
    n_tiles = (2 * t) // MOE_TILE + N_EXPERTS
    routing, counts = router(x, w_router, b_router)
    pos3, tile_expert, n_active = moe_plan(routing, counts, n_tiles, expert0)
    xs = moe_dispatch(x, pos3, jnp.zeros((n_tiles * MOE_TILE, d // 2), jnp.uint32))
    ys = moe_experts(xs, w_up, w_down, tile_expert, n_active)
    return moe_combine(x, routing, ys, pos3, g, b, alpha)


def _rotary_tables(pos):
    half = RET_D // 2
    inv_freq = ROPE_BASE ** (-jnp.arange(half, dtype=F32) / half)
    ang = pos.astype(F32)[:, None] * inv_freq[None, :]
    cos, sin = jnp.cos(ang), jnp.sin(ang)
    return jnp.concatenate([cos, cos], axis=1), jnp.concatenate([-sin, sin], axis=1)


def _retention_decays():
    log_gamma = jnp.log(1.0 - 2.0 ** (-5.0 - jnp.arange(RET_HEADS, dtype=F32)))
    c = RET_CHUNK
    i = jnp.arange(c, dtype=F32)
    diff = i[:, None] - i[None, :]
    dm = jnp.where(diff >= 0, jnp.exp(log_gamma[:, None, None] * jnp.maximum(diff, 0.0)[None]), 0.0)
    shape = (RET_HEADS, c, RET_D)
    qd = jnp.broadcast_to(jnp.exp(log_gamma[:, None] * (i + 1.0)[None, :])[:, :, None], shape)
    kd = jnp.broadcast_to(jnp.exp(log_gamma[:, None] * (c - 1 - i)[None, :])[:, :, None], shape)
    cd = jnp.broadcast_to(jnp.exp(log_gamma * c)[:, None, None], shape)
    return dm, qd, kd, cd


def kernel(x_prompt, x_sample, mem_prompt, state_ret, cache_fox_k, cache_fox_v, cache_fox_logf, cache_mem_k, cache_mem_v, page_table, w_in_a, w_in_b, w_kv_shared, b_forget, w_o, w_mem_kv, ln1_g, ln1_b, ln2_g, ln2_b, w_group, b_group, w_route, b_route, w_up, w_down):
    bp, seq, d = x_prompt.shape
    bs, n_tok, _ = x_sample.shape
    depth = w_o.shape[0]
    n_a = w_in_a.shape[0]
    mem_len = mem_prompt.shape[1]
    n_pages, page = page_table.shape[1], cache_fox_k.shape[1]
    past = n_pages * page
    tp, ts = bp * seq, bs * n_tok
    alpha = (2 * depth) ** 0.25
    assert w_in_a.shape[2] == 4 * RET_W + MEM_W and w_in_b.shape[2] == FOX_W + MEM_W
    assert w_kv_shared.shape[1] == 2 * FOX_W + FOX_HEADS and w_up.shape[1:3] == (N_GROUPS, EXPERTS_PER_GROUP)
    assert seq % RET_CHUNK == 0 and SUBLANES % n_tok == 0 and tp % ROW_TILE == 0 and ts % ROW_TILE == 0

    x = jnp.concatenate([x_prompt.reshape(tp, d), x_sample.reshape(ts, d)], axis=0)
    t_all = tp + ts

    w_mem_t = jnp.transpose(w_mem_kv, (0, 2, 1)).reshape(depth * 2 * MEM_W, d).astype(BF16)
    mem_kv_t = mem_kv_transposed(mem_prompt, w_mem_t)
    mem_kv6 = mem_kv_t.reshape(bp, depth, 2, MEM_HEADS, MEM_HD, mem_len)
    mem_k_prompt = jnp.transpose(mem_kv6[:, :, 0], (1, 0, 4, 2, 3))
    mem_v_prompt = jnp.transpose(mem_kv6[:, :, 1], (1, 0, 4, 2, 3))
    mem_banks_p = mem_kv_t.reshape(bp * depth * 2, MEM_HEADS, MEM_HD, mem_len)
    cache_mkt = jnp.transpose(cache_mem_k, (0, 1, 3, 4, 2)).reshape(depth * bs, MEM_HEADS, MEM_HD, mem_len)
    cache_mvt = jnp.transpose(cache_mem_v, (0, 1, 3, 4, 2)).reshape(depth * bs, MEM_HEADS, MEM_HD, mem_len)

    pos_all = jnp.concatenate([jnp.tile(jnp.arange(seq), bp), jnp.tile(past + jnp.arange(n_tok), bs)])
    cos2, sin2 = _rotary_tables(pos_all)
    decays = _retention_decays()
    zero_state = jnp.zeros((bp, RET_HEADS, RET_D, RET_D), F32)

    w_router = jnp.concatenate(
        [w_group, jnp.transpose(w_route, (0, 2, 1, 3)).reshape(depth, d, N_EXPERTS),
         jnp.zeros((depth, d, LANES - N_GROUPS - N_EXPERTS), F32)], axis=2)
    b_router = jnp.concatenate(
        [b_group, b_route.reshape(depth, N_EXPERTS), jnp.zeros((depth, LANES - N_GROUPS - N_EXPERTS), F32)],
        axis=1)[:, None, :]
    w_up_e = w_up.reshape(depth * N_EXPERTS, d, 2 * D_EXPERT)
    w_down_e = w_down.reshape(depth * N_EXPERTS, D_EXPERT, d)

    sample_rows = SAMPLE_BATCH_TILE * n_tok
    ret_prompt = []
    state_all = state_ret.reshape(n_a * bs, RET_HEADS, RET_D, RET_D)
    ret_sample = jnp.zeros(state_all.shape, F32)
    kv = logf = None
    for l in range(depth):
        if l < n_a:
            h = proj_a(x, w_in_a[l].astype(BF16), cos2, sin2)
            tok_p, s_p = retention_prompt(h, zero_state, decays, bp, seq)
            hs = h[tp:, :2 * RET_W].reshape(bs // SAMPLE_BATCH_TILE, sample_rows, 2, RET_HEADS, RET_D)
            hs = jnp.transpose(hs, (2, 0, 3, 4, 1))
            tok_s, ret_sample = retention_sample(h, hs[0], hs[1], state_all, ret_sample, l, tp, bs, n_tok)
            ret_prompt.append(s_p)
            mq_block = (4 * RET_W) // MEM_W
        else:
            if l == n_a:
                w_kv_pad = jnp.concatenate(
                    [w_kv_shared, jnp.zeros((d, LANES - FOX_HEADS), F32)], axis=1).astype(BF16)
                bf_pad = jnp.concatenate([b_forget, jnp.zeros((LANES - FOX_HEADS,), F32)])[None, :]
                kv, logf = kv_shared(x, w_kv_pad, bf_pad)
                c_prompt = cumsum_rows(logf, bp, seq)
                k_aug, v_t, kt_prompt, vt_prompt = fox_prep(kv, c_prompt, bp, seq)
                cache_kt = jnp.transpose(cache_fox_k, (0, 2, 3, 1))
                cache_vt = jnp.transpose(cache_fox_v, (0, 2, 3, 1))
                cache_lf = jnp.pad(jnp.transpose(cache_fox_logf, (0, 2, 1)),
                                   ((0, 0), (0, FOX_COLS - FOX_HEADS), (0, 0)))
            h = matmul(x, w_in_b[l - n_a].astype(BF16))
            tok_p = fox_prompt(h, c_prompt, k_aug, v_t, bp, seq)
            tok_s = fox_sample(h, kv, logf, cache_kt, cache_vt, cache_lf, page_table, tp, bs, n_tok)
            mq_block = FOX_W // MEM_W
        steps_per_b = seq // ROW_TILE
        mem_p = mem_attention(h, mq_block, mem_banks_p, mem_banks_p,
                              lambda i, l=l: (i // steps_per_b) * (2 * depth) + 2 * l,
                              lambda i, l=l: (i // steps_per_b) * (2 * depth) + 2 * l + 1,
                              row_block0=0, n_steps=tp // ROW_TILE, rows=ROW_TILE, banks=1)
        bank_block = lambda i, l=l: l * (bs // SAMPLE_BATCH_TILE) + i
        mem_s = mem_attention(h, mq_block, cache_mkt, cache_mvt, bank_block, bank_block,
                              row_block0=tp // sample_rows, n_steps=bs // SAMPLE_BATCH_TILE,
                              rows=sample_rows, banks=SAMPLE_BATCH_TILE)
        wt = tok_p.shape[1]
        x = out_proj_ln(tok_p, mem_p, tok_s, mem_s, x, w_o[l, :wt].astype(BF16), w_o[l, wt:].astype(BF16),
                        ln1_g[l][None, :], ln1_b[l][None, :], alpha)
        x = hierarchical_moe_ln(x, w_router[l], b_router[l], w_up_e, w_down_e, l * N_EXPERTS,
                                ln2_g[l][None, :], ln2_b[l][None, :], alpha)

    y_prompt = x[:tp].reshape(bp, seq, d)
    y_sample = x[tp:].reshape(bs, n_tok, d)
    fox_k_s = kv[tp:, :FOX_W]
    fox_v_s = kv[tp:, FOX_W:]
    fox_lf = logf[:, :FOX_HEADS]
    return (y_prompt, y_sample, jnp.stack(ret_prompt), ret_sample.reshape(state_ret.shape),
            jnp.transpose(kt_prompt.reshape(bp, FOX_HEADS, FOX_HD, seq), (0, 3, 1, 2)),
            jnp.transpose(vt_prompt.reshape(bp, FOX_HEADS, FOX_HD, seq), (0, 3, 1, 2)),
            fox_lf[:tp].reshape(bp, seq, FOX_HEADS),
            fox_k_s.reshape(bs, n_tok, FOX_HEADS, FOX_HD), fox_v_s.reshape(bs, n_tok, FOX_HEADS, FOX_HD),
            fox_lf[tp:].reshape(bs, n_tok, FOX_HEADS),
            mem_k_prompt, mem_v_prompt)
```

```python
import functools
import math

import jax
import jax.numpy as jnp
from jax import lax
from jax.experimental import pallas as pl
from jax.experimental.pallas import tpu as pltpu

F32 = jnp.float32
BF16 = jnp.bfloat16
HIGHEST = lax.Precision.HIGHEST

RET_HEADS = 6
RET_D = 128
RET_W = RET_HEADS * RET_D
RET_CHUNK = 128
ROPE_BASE = 10000.0
FOX_HEADS = 12
FOX_HD = 64
FOX_W = FOX_HEADS * FOX_HD
MEM_HEADS = 4
MEM_HD = 64
MEM_W = MEM_HEADS * MEM_HD
N_GROUPS = 4
EXPERTS_PER_GROUP = 8
N_EXPERTS = N_GROUPS * EXPERTS_PER_GROUP
D_EXPERT = 256
LN_EPS = 1e-5
HEAD_NORM_EPS = 1e-6

LANES = 128
SUBLANES = 8
VMEM_LIMIT_BYTES = 48 * 1024 * 1024

ROW_TILE = 512
MOE_TILE = 256
EXPERT_TILE = 512
FOX_BLOCK = 1024
SAMPLE_BATCH_TILE = 8


def _cparams(n_axes):
    return pltpu.CompilerParams(
        dimension_semantics=("arbitrary",) * n_axes, vmem_limit_bytes=VMEM_LIMIT_BYTES)


def _dot(a, b):
    return jnp.dot(a, b, preferred_element_type=F32)


def _dot_nt(a, b):
    return lax.dot_general(a, b, (((1,), (1,)), ((), ())), preferred_element_type=F32)


def _dot_tn(a, b):
    return lax.dot_general(a, b, (((0,), (0,)), ((), ())), preferred_element_type=F32)


def _layer_norm(y, g, b):
    mu = jnp.mean(y, axis=-1, keepdims=True)
    d = y - mu
    var = jnp.mean(d * d, axis=-1, keepdims=True)
    return d * lax.rsqrt(var + LN_EPS) * g + b


def _matmul_kernel(x_ref, w_ref, o_ref):
    o_ref[...] = _dot(x_ref[...].astype(BF16), w_ref[...])


def matmul(x, w_bf16):
    t, k = x.shape
    n = w_bf16.shape[1]
    tm = min(ROW_TILE, t)
    return pl.pallas_call(
        _matmul_kernel,
        out_shape=jax.ShapeDtypeStruct((t, n), F32),
        grid=(t // tm,),
        in_specs=[pl.BlockSpec((tm, k), lambda i: (i, 0)),
                  pl.BlockSpec((k, n), lambda i: (0, 0))],
        out_specs=pl.BlockSpec((tm, n), lambda i: (i, 0)),
        compiler_params=_cparams(1),
        name="matmul",
    )(x, w_bf16)


def _proj_a_kernel(x_ref, w_ref, cos_ref, sin_ref, o_ref, *, k_scale):
    x = x_ref[...].astype(BF16)
    cos2 = cos_ref[...]
    sin2 = sin_ref[...]
    for c in range(2 * RET_HEADS):
        cols = slice(c * RET_D, (c + 1) * RET_D)
        y = _dot(x, w_ref[:, cols])
        y = y * cos2 + pltpu.roll(y, RET_D // 2, 1) * sin2
        if c >= RET_HEADS:
            y = y * k_scale
        o_ref[:, cols] = y
    o_ref[:, 2 * RET_W:] = _dot(x, w_ref[:, 2 * RET_W:])


def proj_a(x, w_bf16, cos2, sin2):
    t, k = x.shape
    n = w_bf16.shape[1]
    tm = min(ROW_TILE, t)
    return pl.pallas_call(
        functools.partial(_proj_a_kernel, k_scale=RET_D ** -0.5),
        out_shape=jax.ShapeDtypeStruct((t, n), F32),
        grid=(t // tm,),
        in_specs=[pl.BlockSpec((tm, k), lambda i: (i, 0)),
                  pl.BlockSpec((k, n), lambda i: (0, 0)),
                  pl.BlockSpec((tm, RET_D), lambda i: (i, 0)),
                  pl.BlockSpec((tm, RET_D), lambda i: (i, 0))],
        out_specs=pl.BlockSpec((tm, n), lambda i: (i, 0)),
        compiler_params=_cparams(1),
        name="proj_a",
    )(x, w_bf16, cos2, sin2)


def _kv_kernel(x_ref, w_ref, bf_ref, kv_ref, logf_ref):
    x = x_ref[...].astype(BF16)
    kv_ref[...] = _dot(x, w_ref[:, :2 * FOX_W])
    z = _dot(x, w_ref[:, 2 * FOX_W:]) + bf_ref[...]
    logf_ref[...] = jnp.minimum(z, 0.0) - jnp.log(1.0 + jnp.exp(-jnp.abs(z)))


def kv_shared(x, w_pad_bf16, b_forget_pad):
    t, k = x.shape
    n = w_pad_bf16.shape[1]
    tm = min(ROW_TILE, t)
    return pl.pallas_call(
        _kv_kernel,
        out_shape=(jax.ShapeDtypeStruct((t, 2 * FOX_W), F32),
                   jax.ShapeDtypeStruct((t, LANES), F32)),
        grid=(t // tm,),
        in_specs=[pl.BlockSpec((tm, k), lambda i: (i, 0)),
                  pl.BlockSpec((k, n), lambda i: (0, 0)),
                  pl.BlockSpec((1, LANES), lambda i: (0, 0))],
        out_specs=(pl.BlockSpec((tm, 2 * FOX_W), lambda i: (i, 0)),
                   pl.BlockSpec((tm, LANES), lambda i: (i, 0))),
        compiler_params=_cparams(1),
        name="kv_shared",
    )(x, w_pad_bf16, b_forget_pad)


def _out_ln_kernel(tokp_ref, memp_ref, toks_ref, mems_ref, x_ref, w1_ref, w2_ref, g_ref, b_ref, o_ref,
                   *, alpha, prompt_tiles):
    def run(tok_ref, mem_ref):
        mixed = _dot(tok_ref[...].astype(BF16), w1_ref[...]) + _dot(mem_ref[...].astype(BF16), w2_ref[...])
        o_ref[...] = _layer_norm(alpha * x_ref[...] + mixed, g_ref[...], b_ref[...])

    @pl.when(pl.program_id(0) < prompt_tiles)
    def _():
        run(tokp_ref, memp_ref)

    @pl.when(pl.program_id(0) >= prompt_tiles)
    def _():
        run(toks_ref, mems_ref)


def out_proj_ln(tok_p, mem_p, tok_s, mem_s, x, w1_bf16, w2_bf16, g, b, alpha):
    t, d = x.shape
    tm = min(ROW_TILE, t)
    wt, wm = tok_p.shape[1], mem_p.shape[1]
    n_p = tok_p.shape[0] // tm
    n_s = tok_s.shape[0] // tm

    def prompt_map(i):
        return (jnp.minimum(i, n_p - 1), 0)

    def sample_map(i):
        return (jnp.clip(i - n_p, 0, n_s - 1), 0)

    return pl.pallas_call(
        functools.partial(_out_ln_kernel, alpha=alpha, prompt_tiles=n_p),
        out_shape=jax.ShapeDtypeStruct((t, d), F32),
        grid=(t // tm,),
        in_specs=[pl.BlockSpec((tm, wt), prompt_map),
                  pl.BlockSpec((tm, wm), prompt_map),
                  pl.BlockSpec((tm, wt), sample_map),
                  pl.BlockSpec((tm, wm), sample_map),
                  pl.BlockSpec((tm, d), lambda i: (i, 0)),
                  pl.BlockSpec((wt, d), lambda i: (0, 0)),
                  pl.BlockSpec((wm, d), lambda i: (0, 0)),
                  pl.BlockSpec((1, d), lambda i: (0, 0)),
                  pl.BlockSpec((1, d), lambda i: (0, 0))],
        out_specs=pl.BlockSpec((tm, d), lambda i: (i, 0)),
        compiler_params=_cparams(1),
        name="out_proj_ln",
    )(tok_p, mem_p, tok_s, mem_s, x, w1_bf16, w2_bf16, g, b)


def _head_norm_gate(r, g):
    mu = jnp.mean(r, axis=-1, keepdims=True)
    d = r - mu
    var = jnp.mean(d * d, axis=-1, keepdims=True)
    return d * lax.rsqrt(var + HEAD_NORM_EPS) * (g * jax.nn.sigmoid(g))


def _retention_prompt_kernel(q_ref, k_ref, v_ref, g_ref, s0_ref, dm_ref, qd_ref, kd_ref, cd_ref,
                             o_ref, sout_ref, s_scr, *, chunks):
    i = pl.program_id(1)

    @pl.when(i == 0)
    def _():
        s_scr[...] = s0_ref[0]

    for c in range(chunks):
        rows = slice(c * RET_CHUNK, (c + 1) * RET_CHUNK)
        for h in range(RET_HEADS):
            cols = slice(h * RET_D, (h + 1) * RET_D)
            q = q_ref[rows, cols]
            k = k_ref[rows, cols]
            v = v_ref[rows, cols].astype(BF16)
            s_prev = s_scr[h]
            scores = _dot_nt(q.astype(BF16), k.astype(BF16)) * dm_ref[h]
            intra = _dot(scores.astype(BF16), v)
            cross = _dot((q * qd_ref[h]).astype(BF16), s_prev.astype(BF16))
            kd = (k * kd_ref[h]).astype(BF16)
            s_scr[h] = cd_ref[h] * s_prev + _dot_tn(kd, v)
            o_ref[rows, cols] = _head_norm_gate(intra + cross, g_ref[rows, cols])

    @pl.when(i == pl.num_programs(1) - 1)
    def _():
        sout_ref[0] = s_scr[...]


def retention_prompt(h, s0, decays, batch, seq):
    chunks = math.gcd(4, seq // RET_CHUNK)
    rows = chunks * RET_CHUNK
    n_i = seq // rows
    dm, qd, kd, cd = decays
    const = pl.BlockSpec((RET_HEADS, RET_CHUNK, RET_D), lambda b, i: (0, 0, 0))
    state = pl.BlockSpec((1, RET_HEADS, RET_D, RET_D), lambda b, i: (b, 0, 0, 0))

    def col(j):
        return pl.BlockSpec((rows, RET_W), lambda b, i, j=j: (b * n_i + i, j))

    return pl.pallas_call(
        functools.partial(_retention_prompt_kernel, chunks=chunks),
        out_shape=(jax.ShapeDtypeStruct((batch * seq, RET_W), F32),
                   jax.ShapeDtypeStruct((batch, RET_HEADS, RET_D, RET_D), F32)),
        grid=(batch, n_i),
        in_specs=[col(0), col(1), col(2), col(3), state, const, const, const, const],
        out_specs=(pl.BlockSpec((rows, RET_W), lambda b, i: (b * n_i + i, 0)), state),
        scratch_shapes=[pltpu.VMEM((RET_HEADS, RET_D, RET_D), F32)],
        compiler_params=_cparams(2),
        name="retention_prompt",
    )(h, h, h, h, s0, dm, qd, kd, cd)


def _retention_sample_kernel(qt_ref, kt_ref, v_ref, g_ref, s0_ref, sprev_ref, o_ref, sout_ref, r_scr,
                             *, n_req, n_tok, gammas):
    del sprev_ref
    for j in range(n_req):
        for h in range(RET_HEADS):
            cols = slice(h * RET_D, (h + 1) * RET_D)
            s = s0_ref[j, h]
            for t in range(n_tok):
                r = j * n_tok + t
                s = gammas[h] * s + kt_ref[0, h, :, r:r + 1] * v_ref[r:r + 1, cols]
                r_scr[r:r + 1, cols] = jnp.sum(qt_ref[0, h, :, r:r + 1] * s, axis=0, keepdims=True)
            sout_ref[j, h] = s
    for h in range(RET_HEADS):
        cols = slice(h * RET_D, (h + 1) * RET_D)
        o_ref[:, cols] = _head_norm_gate(r_scr[:, cols], g_ref[:, cols])


def retention_sample(h, qt, kt, s_all, s_new_all, layer, row0, n_req_total, n_tok):
    n_req = SAMPLE_BATCH_TILE
    rows = n_req * n_tok
    steps = n_req_total // n_req
    blk0 = row0 // rows
    gammas = tuple(1.0 - 2.0 ** (-5.0 - hh) for hh in range(RET_HEADS))
    state = pl.BlockSpec((n_req, RET_HEADS, RET_D, RET_D), lambda i: (layer * steps + i, 0, 0, 0))
    tr = pl.BlockSpec((1, RET_HEADS, RET_D, rows), lambda i: (i, 0, 0, 0))
    return pl.pallas_call(
        functools.partial(_retention_sample_kernel, n_req=n_req, n_tok=n_tok, gammas=gammas),
        out_shape=(jax.ShapeDtypeStruct((n_req_total * n_tok, RET_W), F32),
                   jax.ShapeDtypeStruct(s_new_all.shape, F32)),
        grid=(steps,),
        in_specs=[tr, tr,
                  pl.BlockSpec((rows, RET_W), lambda i: (blk0 + i, 2)),
                  pl.BlockSpec((rows, RET_W), lambda i: (blk0 + i, 3)),
                  state,
                  pl.BlockSpec(memory_space=pl.ANY)],
        out_specs=(pl.BlockSpec((rows, RET_W), lambda i: (i, 0)), state),
        scratch_shapes=[pltpu.VMEM((rows, RET_W), F32)],
        input_output_aliases={5: 1},
        compiler_params=_cparams(1),
        name="retention_sample",
    )(qt, kt, h, h, s_all, s_new_all)


def _mem_attn_few_rows(q_ref, mk_ref, mv_ref, o_ref, *, banks, rows_per_bank):
    q = q_ref[...] * (MEM_HD ** -0.5)
    mem_len = mk_ref.shape[3]
    per_group = SUBLANES // rows_per_bank
    n_bd = MEM_HEADS * SUBLANES
    head_of_lane = lax.broadcasted_iota(jnp.int32, (SUBLANES, MEM_W), 1) // MEM_HD
    bank_of_row = lax.broadcasted_iota(jnp.int32, (SUBLANES, MEM_W), 0) // rows_per_bank
    a_row = lax.broadcasted_iota(jnp.int32, (n_bd, MEM_W), 0)
    a_col = lax.broadcasted_iota(jnp.int32, (n_bd, MEM_W), 1)
    own_head = a_col // MEM_HD == a_row // SUBLANES
    for grp in range(banks // per_group):
        q8 = q[grp * SUBLANES:(grp + 1) * SUBLANES, :]
        out8 = jnp.zeros((SUBLANES, MEM_W), F32)
        for w in range(per_group):
            j = grp * per_group + w
            q_bd = jnp.concatenate(
                [jnp.where((head_of_lane == h) & (bank_of_row == w), q8, 0.0) for h in range(MEM_HEADS)],
                axis=0).astype(BF16)
            s = _dot(q_bd, mk_ref[j].reshape(MEM_W, mem_len).astype(BF16))
            p = jnp.exp(s - jnp.max(s, axis=1, keepdims=True))
            p = p / jnp.sum(p, axis=1, keepdims=True)
            acc = _dot_nt(p.astype(BF16), mv_ref[j].reshape(MEM_W, mem_len).astype(BF16))
            res = jnp.where(own_head, acc, 0.0).reshape(MEM_HEADS, SUBLANES, MEM_W).sum(axis=0)
            out8 = jnp.where(bank_of_row == w, res, out8)
        o_ref[grp * SUBLANES:(grp + 1) * SUBLANES, :] = out8


def _mem_attn_kernel(q_ref, mk_ref, mv_ref, o_ref, *, banks, rows_per_bank):
    if rows_per_bank < SUBLANES:
        _mem_attn_few_rows(q_ref, mk_ref, mv_ref, o_ref, banks=banks, rows_per_bank=rows_per_bank)
        return
    assert banks == 1
    q = q_ref[...] * (MEM_HD ** -0.5)
    n_rows = q.shape[0]
    n_cols = max(n_rows, LANES)
    if n_cols > n_rows:
        q = jnp.concatenate([q, jnp.zeros((n_cols - n_rows, MEM_W), F32)], axis=0)
    qt = q.T.astype(BF16)
    heads = []
    for h in range(MEM_HEADS):
        mk = mk_ref[0, h].T.astype(BF16)
        s = _dot(mk, qt[h * MEM_HD:(h + 1) * MEM_HD, :])
        p = jnp.exp(s - jnp.max(s, axis=0, keepdims=True))
        p = p / jnp.sum(p, axis=0, keepdims=True)
        heads.append(_dot(mv_ref[0, h].astype(BF16), p.astype(BF16)))
    o_ref[...] = jnp.concatenate(heads, axis=0).T[:n_rows, :]


def mem_attention(h, q_col_block, mkt, mvt, k_block, v_block, *, row_block0, n_steps, rows, banks):
    mem_len = mkt.shape[3]
    bank_shape = (banks, MEM_HEADS, MEM_HD, mem_len)
    return pl.pallas_call(
        functools.partial(_mem_attn_kernel, banks=banks, rows_per_bank=rows // banks),
        out_shape=jax.ShapeDtypeStruct((n_steps * rows, MEM_W), F32),
        grid=(n_steps,),
        in_specs=[pl.BlockSpec((rows, MEM_W), lambda i: (row_block0 + i, q_col_block)),
                  pl.BlockSpec(bank_shape, lambda i: (k_block(i), 0, 0, 0)),
                  pl.BlockSpec(bank_shape, lambda i: (v_block(i), 0, 0, 0))],
        out_specs=pl.BlockSpec((rows, MEM_W), lambda i: (i, 0)),
        compiler_params=_cparams(1),
        name="mem_attention",
    )(h, mkt, mvt)


def _mem_kv_kernel(x_ref, wt_ref, o_ref):
    o_ref[0] = _dot_nt(wt_ref[...], x_ref[...].astype(BF16))


def mem_kv_transposed(mem, w_t_bf16):
    b, m, d = mem.shape
    f = w_t_bf16.shape[0]
    return pl.pallas_call(
        _mem_kv_kernel,
        out_shape=jax.ShapeDtypeStruct((b, f, m), F32),
        grid=(b,),
        in_specs=[pl.BlockSpec((m, d), lambda i: (i, 0)),
                  pl.BlockSpec((f, d), lambda i: (0, 0))],
        out_specs=pl.BlockSpec((1, f, m), lambda i: (i, 0, 0)),
        compiler_params=_cparams(1),
        name="mem_kv",
    )(mem.reshape(b * m, d), w_t_bf16)


def _cumsum_kernel(x_ref, o_ref, carry):
    @pl.when(pl.program_id(1) == 0)
    def _():
        carry[...] = jnp.zeros_like(carry)

    n = x_ref.shape[0]
    row = lax.broadcasted_iota(jnp.int32, (n, n), 0)
    col = lax.broadcasted_iota(jnp.int32, (n, n), 1)
    tri = (row >= col).astype(F32)
    c = jnp.dot(tri, x_ref[...], precision=HIGHEST, preferred_element_type=F32) + carry[...]
    o_ref[...] = c
    carry[...] = c[n - 1:n, :]


def cumsum_rows(x, batch, seq):
    tm = min(ROW_TILE, seq)
    n_i = seq // tm
    return pl.pallas_call(
        _cumsum_kernel,
        out_shape=jax.ShapeDtypeStruct((batch * seq, x.shape[1]), F32),
        grid=(batch, n_i),
        in_specs=[pl.BlockSpec((tm, x.shape[1]), lambda b, i: (b * n_i + i, 0))],
        out_specs=pl.BlockSpec((tm, x.shape[1]), lambda b, i: (b * n_i + i, 0)),
        scratch_shapes=[pltpu.VMEM((1, x.shape[1]), F32)],
        compiler_params=_cparams(2),
        name="cumsum_rows",
    )(x)


BIAS_COLS = 12


def _split3(x):
    hi = x.astype(BF16).astype(F32)
    mid = (x - hi).astype(BF16).astype(F32)
    lo = x - hi - mid
    return hi, mid, lo


def _place(lane, base, parts):
    out = jnp.zeros(lane.shape, F32)
    for idx, part in enumerate(parts):
        out = jnp.where(lane == base + idx, part, out)
    return out


def _fox_prep_kernel(k_ref, v_ref, c_ref, ka_ref, vt_ref, ktf_ref, vtf_ref):
    p = pl.program_id(0)
    rows = k_ref.shape[0]
    lane = lax.broadcasted_iota(jnp.int32, (rows, LANES), 1)
    c2 = c_ref[...]
    bias = jnp.zeros((rows, LANES), F32)
    for e in range(2):
        ck = jnp.sum(jnp.where(lane == 2 * p + e, c2, 0.0), axis=1, keepdims=True)
        base = e * (BIAS_COLS // 2)
        ones = jnp.where((lane >= base) & (lane < base + 3), 1.0, 0.0)
        bias = bias + ones + _place(lane, base + 3, [-part for part in _split3(ck)])
    ka_ref[0] = jnp.concatenate([k_ref[...], bias], axis=1).astype(BF16)
    v_t = v_ref[...].T
    vt_ref[0] = v_t.astype(BF16)
    vtf_ref[0] = v_t
    ktf_ref[0] = k_ref[...].T


def fox_prep(kv, c, batch, seq):
    pairs = FOX_HEADS // 2
    n_rows = batch * seq
    tm = min(2 * FOX_BLOCK, seq)
    per_b = seq // tm
    t_spec = pl.BlockSpec((1, LANES, tm), lambda p, i: (i // per_b, p, i % per_b))
    return pl.pallas_call(
        _fox_prep_kernel,
        out_shape=(jax.ShapeDtypeStruct((pairs, n_rows, 2 * LANES), BF16),
                   jax.ShapeDtypeStruct((pairs, LANES, n_rows), BF16),
                   jax.ShapeDtypeStruct((batch, FOX_W, seq), F32),
                   jax.ShapeDtypeStruct((batch, FOX_W, seq), F32)),
        grid=(pairs, n_rows // tm),
        in_specs=[pl.BlockSpec((tm, LANES), lambda p, i: (i, p)),
                  pl.BlockSpec((tm, LANES), lambda p, i: (i, pairs + p)),
                  pl.BlockSpec((tm, LANES), lambda p, i: (i, 0))],
        out_specs=(pl.BlockSpec((1, tm, 2 * LANES), lambda p, i: (p, i, 0)),
                   pl.BlockSpec((1, LANES, tm), lambda p, i: (p, 0, i)),
                   t_spec, t_spec),
        compiler_params=_cparams(2),
        name="fox_prep",
    )(kv, kv, c)


def _fox_prompt_kernel(q_ref, c_ref, ka_ref, vt_ref, o_ref, acc_a, acc_b, *, blk):
    p = pl.program_id(1)
    i = pl.program_id(2)
    lane = lax.broadcasted_iota(jnp.int32, (blk, LANES), 1)
    q2 = q_ref[...] * (FOX_HD ** -0.5)
    c2 = c_ref[...]
    qts = []
    for e in range(2):
        head_lanes = (lane >= e * FOX_HD) & (lane < (e + 1) * FOX_HD)
        cq = jnp.sum(jnp.where(lane == 2 * p + e, c2, 0.0), axis=1, keepdims=True)
        base = e * (BIAS_COLS // 2)
        ones = jnp.where((lane >= base + 3) & (lane < base + 6), 1.0, 0.0)
        q_aug = jnp.concatenate([jnp.where(head_lanes, q2, 0.0), ones + _place(lane, base, _split3(cq))], axis=1)
        qts.append(q_aug.T.astype(BF16))
    accs = (acc_a, acc_b)
    acc_a[...] = jnp.zeros(acc_a.shape, F32)
    acc_b[...] = jnp.zeros(acc_b.shape, F32)

    def block(j, carry, masked):
        start = pl.multiple_of(j * blk, blk)
        kb = ka_ref[0, pl.ds(start, blk), :]
        vt = vt_ref[0, :, pl.ds(start, blk)]
        new = []
        for e in range(2):
            m_old, l_old = carry[2 * e], carry[2 * e + 1]
            s = _dot(kb, qts[e])
            if masked:
                key = lax.broadcasted_iota(jnp.int32, (blk, blk), 0)
                qry = lax.broadcasted_iota(jnp.int32, (blk, blk), 1)
                s = jnp.where(key <= qry, s, -jnp.inf)
            m_new = jnp.maximum(m_old, jnp.max(s, axis=0, keepdims=True))
            a = jnp.exp(m_old - m_new)
            pe = jnp.exp(s - m_new)
            new += [m_new, a * l_old + jnp.sum(pe, axis=0, keepdims=True)]
            acc = accs[e]
            acc[...] = a * acc[...] + _dot(vt[e * FOX_HD:(e + 1) * FOX_HD, :], pe.astype(BF16))
        return tuple(new)

    init = (jnp.full((1, blk), -jnp.inf, F32), jnp.zeros((1, blk), F32)) * 2
    carry = lax.fori_loop(0, i, lambda j, c: block(j, c, False), init)
    carry = block(i, carry, True)
    out_t = jnp.concatenate([acc_a[...] / carry[1], acc_b[...] / carry[3]], axis=0)
    o_ref[...] = out_t.T


def fox_prompt(h, c, k_aug, v_t, batch, seq):
    blk = min(FOX_BLOCK, seq)
    n_q = seq // blk
    pairs = FOX_HEADS // 2
    return pl.pallas_call(
        functools.partial(_fox_prompt_kernel, blk=blk),
        out_shape=jax.ShapeDtypeStruct((batch * seq, FOX_W), F32),
        grid=(batch, pairs, n_q),
        in_specs=[pl.BlockSpec((blk, LANES), lambda b, p, i: (b * n_q + i, p)),
                  pl.BlockSpec((blk, LANES), lambda b, p, i: (b * n_q + i, 0)),
                  pl.BlockSpec((1, seq, 2 * LANES), lambda b, p, i: (p, b, 0)),
                  pl.BlockSpec((1, LANES, seq), lambda b, p, i: (p, 0, b))],
        out_specs=pl.BlockSpec((blk, LANES), lambda b, p, i: (b * n_q + i, p)),
        scratch_shapes=[pltpu.VMEM((FOX_HD, blk), F32), pltpu.VMEM((FOX_HD, blk), F32)],
        compiler_params=_cparams(3),
        name="fox_prompt",
    )(h, c, k_aug, v_t)


FOX_COLS = 16


def _fox_sample_kernel(pt_ref, q_ref, kvn_ref, lfn_ref, *refs, n_pages, page, n_tok):
    del pt_ref
    kt_refs = refs[:n_pages]
    vt_refs = refs[n_pages:2 * n_pages]
    lf_refs = refs[2 * n_pages:3 * n_pages]
    o_ref = refs[3 * n_pages]
    n_rows = n_tok * FOX_COLS
    reqs = SUBLANES // n_tok
    which = pl.program_id(0) % reqs

    def pick(block):
        out = block[0:n_tok, :]
        for w in range(1, reqs):
            out = jnp.where(which == w, block[w * n_tok:(w + 1) * n_tok, :], out)
        return out

    q = pick(q_ref[...])[:, :FOX_W] * (FOX_HD ** -0.5)
    kvn = pick(kvn_ref[...])
    lfn = pick(lfn_ref[...])
    head_of_lane = lax.broadcasted_iota(jnp.int32, (FOX_COLS, FOX_W), 1) // FOX_HD
    head_of_row = lax.broadcasted_iota(jnp.int32, (FOX_COLS, FOX_W), 0)
    q_rows = [jnp.where(head_of_lane == head_of_row, jnp.broadcast_to(q[t:t + 1, :], (FOX_COLS, FOX_W)), 0.0)
              for t in range(n_tok)]
    q_bd = jnp.concatenate(q_rows, axis=0).astype(BF16)

    kt = jnp.concatenate([r[0].reshape(FOX_W, page) for r in kt_refs], axis=1).astype(BF16)
    s = _dot(q_bd, kt)

    t_row = lax.broadcasted_iota(jnp.int32, (page, page), 0)
    t_col = lax.broadcasted_iota(jnp.int32, (page, page), 1)
    upto = (t_row <= t_col).astype(F32)
    carry = jnp.zeros((FOX_COLS, 1), F32)
    c_pages = []
    for pg in range(n_pages):
        c_pg = jnp.dot(lf_refs[pg][0], upto, precision=HIGHEST, preferred_element_type=F32) + carry
        carry = c_pg[:, page - 1:page]
        c_pages.append(c_pg)
    c_past = jnp.concatenate(c_pages, axis=1)

    lfn_t = jnp.concatenate([lfn, jnp.zeros((SUBLANES - n_tok, LANES), F32)], axis=0).T[:FOX_COLS, :]
    c_new = []
    for t in range(n_tok):
        carry = carry + lfn_t[:, t:t + 1]
        c_new.append(carry)
    cq = jnp.concatenate(c_new, axis=0)
    s = s + (cq - jnp.concatenate([c_past] * n_tok, axis=0))

    s_new = _dot_nt(q_bd, kvn[:, :FOX_W].astype(BF16))
    ck_new = jnp.concatenate([jnp.concatenate([c] * n_tok, axis=0) for c in c_new], axis=1)
    tok_of_row = lax.broadcasted_iota(jnp.int32, (n_rows, n_tok), 0) // FOX_COLS
    key_tok = lax.broadcasted_iota(jnp.int32, (n_rows, n_tok), 1)
    s_new = jnp.where(key_tok <= tok_of_row, s_new + (cq - ck_new), -jnp.inf)

    m = jnp.maximum(jnp.max(s, axis=1, keepdims=True), jnp.max(s_new, axis=1, keepdims=True))
    p = jnp.exp(s - m)
    p_new = jnp.exp(s_new - m)
    inv = 1.0 / (jnp.sum(p, axis=1, keepdims=True) + jnp.sum(p_new, axis=1, keepdims=True))
    vt = jnp.concatenate([r[0].reshape(FOX_W, page) for r in vt_refs], axis=1).astype(BF16)
    acc = _dot_nt((p * inv).astype(BF16), vt)
    p_new = p_new * inv
    v_new = kvn[:, FOX_W:]
    for t in range(n_tok):
        acc = acc + p_new[:, t:t + 1] * v_new[t:t + 1, :]

    a_row = lax.broadcasted_iota(jnp.int32, (n_rows, FOX_W), 0)
    a_col = lax.broadcasted_iota(jnp.int32, (n_rows, FOX_W), 1)
    acc = jnp.where(a_col // FOX_HD == a_row % FOX_COLS, acc, 0.0)
    out = acc.reshape(n_tok, FOX_COLS, FOX_W).sum(axis=1)
    for w in range(reqs):
        @pl.when(which == w)
        def _(w=w):
            o_ref[w * n_tok:(w + 1) * n_tok, :] = out


def fox_sample(h, kv, logf, cache_kt, cache_vt, cache_lf, page_table, row0, n_req, n_tok):
    n_pages = page_table.shape[1]
    page = cache_kt.shape[3]
    reqs = SUBLANES // n_tok
    blk0 = row0 // SUBLANES

    def new_spec(width):
        return pl.BlockSpec((SUBLANES, width), lambda r, pt: (blk0 + r // reqs, 0))

    def kv_spec(pg):
        return pl.BlockSpec((1, FOX_HEADS, FOX_HD, page), lambda r, pt, pg=pg: (pt[r * n_pages + pg], 0, 0, 0))

    def lf_spec(pg):
        return pl.BlockSpec((1, FOX_COLS, page), lambda r, pt, pg=pg: (pt[r * n_pages + pg], 0, 0))

    grid_spec = pltpu.PrefetchScalarGridSpec(
        num_scalar_prefetch=1,
        grid=(n_req,),
        in_specs=[new_spec(h.shape[1]), new_spec(kv.shape[1]), new_spec(logf.shape[1])]
        + [kv_spec(pg) for pg in range(n_pages)] + [kv_spec(pg) for pg in range(n_pages)]
        + [lf_spec(pg) for pg in range(n_pages)],
        out_specs=pl.BlockSpec((SUBLANES, FOX_W), lambda r, pt: (r // reqs, 0)),
    )
    return pl.pallas_call(
        functools.partial(_fox_sample_kernel, n_pages=n_pages, page=page, n_tok=n_tok),
        out_shape=jax.ShapeDtypeStruct((n_req * n_tok, FOX_W), F32),
        grid_spec=grid_spec,
        compiler_params=_cparams(1),
        name="fox_sample",
    )(page_table.reshape(-1), h, kv, logf, *([cache_kt] * n_pages), *([cache_vt] * n_pages),
      *([cache_lf] * n_pages))


def _router_kernel(x_ref, wh_ref, wl_ref, b_ref, o_ref, cnt_ref, carry):
    @pl.when(pl.program_id(0) == 0)
    def _():
        carry[...] = jnp.zeros_like(carry)

    x = x_ref[...]
    x_hi = x.astype(BF16)
    x_lo = (x - x_hi.astype(F32)).astype(BF16)
    logits = (_dot(x_hi, wh_ref[...]) + (_dot(x_hi, wl_ref[...]) + _dot(x_lo, wh_ref[...]))) + b_ref[...]
    lane = lax.broadcasted_iota(jnp.int32, logits.shape, 1)
    big = jnp.int32(LANES)
    neg = -jnp.inf
    gl = jnp.where(lane < N_GROUPS, logits, neg)
    g_max = jnp.max(gl, axis=-1, keepdims=True)
    g_idx = jnp.min(jnp.where(gl == g_max, lane, big), axis=-1, keepdims=True)
    g_w = 1.0 / jnp.sum(jnp.exp(gl - g_max), axis=-1, keepdims=True)
    in_group = (lane >= N_GROUPS) & (lane < N_GROUPS + N_EXPERTS) & (((lane - N_GROUPS) >> 3) == g_idx)
    el = jnp.where(in_group, logits, neg)
    v1 = jnp.max(el, axis=-1, keepdims=True)
    i1 = jnp.min(jnp.where(el == v1, lane, big), axis=-1, keepdims=True)
    el2 = jnp.where(lane == i1, neg, el)
    v2 = jnp.max(el2, axis=-1, keepdims=True)
    i2 = jnp.min(jnp.where(el2 == v2, lane, big), axis=-1, keepdims=True)
    e2 = jnp.exp(v2 - v1)
    w1 = g_w / (1.0 + e2)
    w2 = g_w * e2 / (1.0 + e2)
    tm = logits.shape[0]
    e1 = i1 - N_GROUPS
    e2 = i2 - N_GROUPS
    oh1 = jnp.where(lane == e1, 1.0, 0.0)
    oh2 = jnp.where(lane == e2, 1.0, 0.0)
    row = lax.broadcasted_iota(jnp.int32, (tm, tm), 0)
    col = lax.broadcasted_iota(jnp.int32, (tm, tm), 1)
    before = jnp.where(col < row, 1.0, 0.0).astype(BF16)
    tot1 = jnp.sum(oh1, axis=0, keepdims=True)
    base = carry[...]
    r1 = jnp.sum(oh1 * (base + _dot(before, oh1.astype(BF16))), axis=-1, keepdims=True)
    r2 = jnp.sum(oh2 * (base + tot1 + _dot(before, oh2.astype(BF16))), axis=-1, keepdims=True)
    carry[...] = base + tot1 + jnp.sum(oh2, axis=0, keepdims=True)
    cnt_ref[...] = carry[...]
    out = jnp.zeros(logits.shape, F32)
    for idx, val in enumerate((e1.astype(F32), e2.astype(F32), w1, w2, r1, r2)):
        out = jnp.where(lane == idx, val, out)
    o_ref[...] = out


def router(x, w_pad, b_pad):
    t, d = x.shape
    tm = min(ROW_TILE, t)
    w_hi = w_pad.astype(BF16)
    w_lo = (w_pad - w_hi.astype(F32)).astype(BF16)
    return pl.pallas_call(
        _router_kernel,
        out_shape=(jax.ShapeDtypeStruct((t, LANES), F32), jax.ShapeDtypeStruct((1, LANES), F32)),
        grid=(t // tm,),
        in_specs=[pl.BlockSpec((tm, d), lambda i: (i, 0)),
                  pl.BlockSpec((d, LANES), lambda i: (0, 0)),
                  pl.BlockSpec((d, LANES), lambda i: (0, 0)),
                  pl.BlockSpec((1, LANES), lambda i: (0, 0))],
        out_specs=(pl.BlockSpec((tm, LANES), lambda i: (i, 0)), pl.BlockSpec((1, LANES), lambda i: (0, 0))),
        scratch_shapes=[pltpu.VMEM((1, LANES), F32)],
        compiler_params=_cparams(1),
        name="router",
    )(x, w_hi, w_lo, b_pad)


def _pack_rows(y):
    n = y.shape[1] // 2
    bits = lax.bitcast_convert_type(y.astype(BF16).astype(F32), jnp.uint32)
    return bits[:, :n] | (bits[:, n:] >> 16)


def _unpack_rows(w):
    return (lax.bitcast_convert_type(w & jnp.uint32(0xFFFF0000), F32),
            lax.bitcast_convert_type(w << 16, F32))


def _row_copy(src_ref, src_row, dst_ref, dst_row, sem):
    return pltpu.make_async_copy(src_ref.at[pl.ds(src_row, 1)], dst_ref.at[pl.ds(dst_row, 1)], sem)


def _drain_rows(src_ref, dst_ref, sem, n_rows):
    def drain(r, carry):
        _row_copy(src_ref, 0, dst_ref, 0, sem).wait()
        return carry

    lax.fori_loop(0, n_rows, drain, 0, unroll=8)


def _dispatch_kernel(pos_ref, x_ref, xs_in_ref, xs_ref, buf, sems, *, tm):
    del xs_in_ref
    i = pl.program_id(0)
    n = pl.num_programs(0)
    for slot in range(2):
        @pl.when(i % 2 == slot)
        def _(slot=slot):
            @pl.when(i >= 2)
            def _():
                _drain_rows(buf.at[slot], xs_ref, sems.at[slot], 2 * tm)

            buf[slot] = _pack_rows(x_ref[...])

            def issue(r, carry):
                for k in range(2):
                    _row_copy(buf.at[slot], r, xs_ref, pos_ref[0, 0, k * tm + r], sems.at[slot]).start(priority=k)
                return carry

            lax.fori_loop(0, tm, issue, 0, unroll=8)

            @pl.when(i == n - 1)
            def _():
                _drain_rows(buf.at[slot], xs_ref, sems.at[slot], 2 * tm)

                @pl.when(n >= 2)
                def _():
                    _drain_rows(buf.at[1 - slot], xs_ref, sems.at[1 - slot], 2 * tm)


def moe_dispatch(x, pos3, xs_init):
    t, d = x.shape
    tm = MOE_TILE
    return pl.pallas_call(
        functools.partial(_dispatch_kernel, tm=tm),
        out_shape=jax.ShapeDtypeStruct(xs_init.shape, jnp.uint32),
        grid=(t // tm,),
        in_specs=[pl.BlockSpec((1, 1, 2 * tm), lambda i: (i, 0, 0), memory_space=pltpu.SMEM),
                  pl.BlockSpec((tm, d), lambda i: (i, 0)),
                  pl.BlockSpec(memory_space=pl.ANY)],
        out_specs=pl.BlockSpec(memory_space=pl.ANY),
        scratch_shapes=[pltpu.VMEM((2, tm, d // 2), jnp.uint32), pltpu.SemaphoreType.DMA((2,))],
        input_output_aliases={2: 0},
        compiler_params=_cparams(1),
        name="moe_dispatch",
    )(pos3, x, xs_init)


def _expert_kernel(te_ref, na_ref, x_ref, wu_ref, wd_ref, o_ref, wu_bf, wd_bf):
    i = pl.program_id(0)
    active = i < na_ref[0]
    changed = jnp.logical_or(i == 0, te_ref[i] != te_ref[jnp.maximum(i - 1, 0)])

    @pl.when(jnp.logical_and(active, changed))
    def _():
        wu_bf[...] = wu_ref[0].astype(BF16)
        wd_bf[...] = wd_ref[0].astype(BF16)

    @pl.when(active)
    def _():
        half = wu_bf.shape[0] // 2
        xa, xb = _unpack_rows(x_ref[...])
        up = _dot(xa.astype(BF16), wu_bf[:half, :]) + _dot(xb.astype(BF16), wu_bf[half:, :])
        hid = jax.nn.silu(up[:, :D_EXPERT]) * up[:, D_EXPERT:]
        o_ref[...] = _dot(hid.astype(BF16), wd_bf[...])

    @pl.when(jnp.logical_not(active))
    def _():
        o_ref[...] = jnp.zeros_like(o_ref)


def moe_experts(xs, w_up, w_down, tile_expert, n_active):
    r = xs.shape[0]
    d = w_up.shape[1]
    tm = EXPERT_TILE
    n_tiles = r // tm

    def row_map(i, te, na):
        return (jnp.minimum(i, na[0] - 1), 0)

    grid_spec = pltpu.PrefetchScalarGridSpec(
        num_scalar_prefetch=2,
        grid=(n_tiles,),
        in_specs=[pl.BlockSpec((tm, d // 2), row_map),
                  pl.BlockSpec((1, d, 2 * D_EXPERT), lambda i, te, na: (te[i], 0, 0)),
                  pl.BlockSpec((1, D_EXPERT, d), lambda i, te, na: (te[i], 0, 0))],
        out_specs=pl.BlockSpec((tm, d), lambda i, te, na: (i, 0)),
        scratch_shapes=[pltpu.VMEM((d, 2 * D_EXPERT), BF16), pltpu.VMEM((D_EXPERT, d), BF16)],
    )
    return pl.pallas_call(
        _expert_kernel,
        out_shape=jax.ShapeDtypeStruct((r, d), F32),
        grid_spec=grid_spec,
        compiler_params=_cparams(1),
        name="moe_experts",
    )(tile_expert, n_active, xs, w_up, w_down)


def _combine_kernel(pos_ref, posn_ref, x_ref, rt_ref, g_ref, b_ref, ys_ref, o_ref, buf, sems, *, tm, alpha):
    i = pl.program_id(0)
    n = pl.num_programs(0)

    def gather(p_ref, slot):
        def issue(r, carry):
            for k in range(2):
                _row_copy(ys_ref, p_ref[0, 0, k * tm + r], buf.at[slot, k], r, sems.at[slot]).start(priority=k)
            return carry

        lax.fori_loop(0, tm, issue, 0, unroll=8)

    @pl.when(i == 0)
    def _():
        gather(pos_ref, 0)

    for slot in range(2):
        @pl.when(i % 2 == slot)
        def _(slot=slot):
            @pl.when(i + 1 < n)
            def _():
                gather(posn_ref, 1 - slot)

            _drain_rows(ys_ref, buf.at[slot, 0], sems.at[slot], 2 * tm)
            rt = rt_ref[...]
            ffn = rt[:, 2:3] * buf[slot, 0] + rt[:, 3:4] * buf[slot, 1]
            o_ref[...] = _layer_norm(alpha * x_ref[...] + ffn, g_ref[...], b_ref[...])


def moe_combine(x, routing, ys, pos3, g, b, alpha):
    t, d = x.shape
    tm = MOE_TILE
    n = t // tm
    return pl.pallas_call(
        functools.partial(_combine_kernel, tm=tm, alpha=alpha),
        out_shape=jax.ShapeDtypeStruct((t, d), F32),
        grid=(n,),
        in_specs=[pl.BlockSpec((1, 1, 2 * tm), lambda i: (i, 0, 0), memory_space=pltpu.SMEM),
                  pl.BlockSpec((1, 1, 2 * tm), lambda i: (jnp.minimum(i + 1, n - 1), 0, 0),
                               memory_space=pltpu.SMEM),
                  pl.BlockSpec((tm, d), lambda i: (i, 0)),
                  pl.BlockSpec((tm, LANES), lambda i: (i, 0)),
                  pl.BlockSpec((1, d), lambda i: (0, 0)),
                  pl.BlockSpec((1, d), lambda i: (0, 0)),
                  pl.BlockSpec(memory_space=pl.ANY)],
        out_specs=pl.BlockSpec((tm, d), lambda i: (i, 0)),
        scratch_shapes=[pltpu.VMEM((2, 2, tm, d), F32), pltpu.SemaphoreType.DMA((2,))],
        compiler_params=_cparams(1),
        name="moe_combine",
    )(pos3, pos3, x, routing, g, b, ys)


def moe_plan(routing, counts, n_tiles, expert0):
    t = routing.shape[0]
    tm = MOE_TILE
    experts = jnp.arange(N_EXPERTS, dtype=jnp.int32)
    cnt = counts[0, :N_EXPERTS].astype(jnp.int32)
    tiles_e = (cnt + EXPERT_TILE - 1) // EXPERT_TILE
    tile_end = jnp.cumsum(tiles_e)
    offs = (tile_end - tiles_e) * EXPERT_TILE
    ids = routing[:, 0:2].astype(jnp.int32)
    rank = routing[:, 4:6].astype(jnp.int32)
    pos = jnp.sum(jnp.where(ids[:, :, None] == experts, offs, 0), axis=-1) + rank
    n_active = tile_end[-1:]
    tile_ids = jnp.minimum(jnp.arange(n_tiles, dtype=jnp.int32), n_active[0] - 1)
    tile_expert = jnp.minimum(jnp.sum((tile_ids[:, None] >= tile_end[None, :]).astype(jnp.int32), axis=1),
                              N_EXPERTS - 1) + expert0
    pos3 = pos.reshape(t // tm, tm, 2).transpose(0, 2, 1).reshape(t // tm, 1, 2 * tm)
    return pos3, tile_expert, n_active


def hierarchical_moe_ln(x, w_router, b_router, w_up, w_down, expert0, g, b, alpha):
    t, d = x.shape
    n_tiles = (2 * t) // EXPERT_TILE + N_EXPERTS
    routing, counts = router(x, w_router, b_router)
    pos3, tile_expert, n_active = moe_plan(routing, counts, n_tiles, expert0)
    xs = moe_dispatch(x, pos3, jnp.zeros((n_tiles * EXPERT_TILE, d // 2), jnp.uint32))
    ys = moe_experts(xs, w_up, w_down, tile_expert, n_active)
    return moe_combine(x, routing, ys, pos3, g, b, alpha)


def _rotary_tables(pos):
    half = RET_D // 2
    inv_freq = ROPE_BASE ** (-jnp.arange(half, dtype=F32) / half)
    ang = pos.astype(F32)[:, None] * inv_freq[None, :]
    cos, sin = jnp.cos(ang), jnp.sin(ang)
    return jnp.concatenate([cos, cos], axis=1), jnp.concatenate([-sin, sin], axis=1)


def _retention_decays():
    log_gamma = jnp.log(1.0 - 2.0 ** (-5.0 - jnp.arange(RET_HEADS, dtype=F32)))
    c = RET_CHUNK
    i = jnp.arange(c, dtype=F32)
    diff = i[:, None] - i[None, :]
    dm = jnp.where(diff >= 0, jnp.exp(log_gamma[:, None, None] * jnp.maximum(diff, 0.0)[None]), 0.0)
    shape = (RET_HEADS, c, RET_D)
    qd = jnp.broadcast_to(jnp.exp(log_gamma[:, None] * (i + 1.0)[None, :])[:, :, None], shape)
    kd = jnp.broadcast_to(jnp.exp(log_gamma[:, None] * (c - 1 - i)[None, :])[:, :, None], shape)
    cd = jnp.broadcast_to(jnp.exp(log_gamma * c)[:, None, None], shape)
    return dm, qd, kd, cd


def kernel(x_prompt, x_sample, mem_prompt, state_ret, cache_fox_k, cache_fox_v, cache_fox_logf, cache_mem_k, cache_mem_v, page_table, w_in_a, w_in_b, w_kv_shared, b_forget, w_o, w_mem_kv, ln1_g, ln1_b, ln2_g, ln2_b, w_group, b_group, w_route, b_route, w_up, w_down):
    bp, seq, d = x_prompt.shape
    bs, n_tok, _ = x_sample.shape
    depth = w_o.shape[0]
    n_a = w_in_a.shape[0]
    mem_len = mem_prompt.shape[1]
    n_pages, page = page_table.shape[1], cache_fox_k.shape[1]
    past = n_pages * page
    tp, ts = bp * seq, bs * n_tok
    alpha = (2 * depth) ** 0.25
    assert w_in_a.shape[2] == 4 * RET_W + MEM_W and w_in_b.shape[2] == FOX_W + MEM_W
    assert w_kv_shared.shape[1] == 2 * FOX_W + FOX_HEADS and w_up.shape[1:3] == (N_GROUPS, EXPERTS_PER_GROUP)
    assert seq % RET_CHUNK == 0 and SUBLANES % n_tok == 0 and tp % ROW_TILE == 0 and ts % ROW_TILE == 0

    x = jnp.concatenate([x_prompt.reshape(tp, d), x_sample.reshape(ts, d)], axis=0)
    t_all = tp + ts

    w_mem_t = jnp.transpose(w_mem_kv, (0, 2, 1)).reshape(depth * 2 * MEM_W, d).astype(BF16)
    mem_kv_t = mem_kv_transposed(mem_prompt, w_mem_t)
    mem_kv6 = mem_kv_t.reshape(bp, depth, 2, MEM_HEADS, MEM_HD, mem_len)
    mem_k_prompt = jnp.transpose(mem_kv6[:, :, 0], (1, 0, 4, 2, 3))
    mem_v_prompt = jnp.transpose(mem_kv6[:, :, 1], (1, 0, 4, 2, 3))
    mem_banks_p = mem_kv_t.reshape(bp * depth * 2, MEM_HEADS, MEM_HD, mem_len)
    cache_mkt = jnp.transpose(cache_mem_k, (0, 1, 3, 4, 2)).reshape(depth * bs, MEM_HEADS, MEM_HD, mem_len)
    cache_mvt = jnp.transpose(cache_mem_v, (0, 1, 3, 4, 2)).reshape(depth * bs, MEM_HEADS, MEM_HD, mem_len)

    pos_all = jnp.concatenate([jnp.tile(jnp.arange(seq), bp), jnp.tile(past + jnp.arange(n_tok), bs)])
    cos2, sin2 = _rotary_tables(pos_all)
    decays = _retention_decays()
    zero_state = jnp.zeros((bp, RET_HEADS, RET_D, RET_D), F32)

    w_router = jnp.concatenate(
        [w_group, jnp.transpose(w_route, (0, 2, 1, 3)).reshape(depth, d, N_EXPERTS),
         jnp.zeros((depth, d, LANES - N_GROUPS - N_EXPERTS), F32)], axis=2)
    b_router = jnp.concatenate(
        [b_group, b_route.reshape(depth, N_EXPERTS), jnp.zeros((depth, LANES - N_GROUPS - N_EXPERTS), F32)],
        axis=1)[:, None, :]
    w_up_e = w_up.reshape(depth * N_EXPERTS, d, 2 * D_EXPERT)
    w_down_e = w_down.reshape(depth * N_EXPERTS, D_EXPERT, d)

    sample_rows = SAMPLE_BATCH_TILE * n_tok
    ret_prompt = []
    state_all = state_ret.reshape(n_a * bs, RET_HEADS, RET_D, RET_D)
    ret_sample = jnp.zeros(state_all.shape, F32)
    kv = logf = None
    for l in range(depth):
        if l < n_a:
            h = proj_a(x, w_in_a[l].astype(BF16), cos2, sin2)
            tok_p, s_p = retention_prompt(h, zero_state, decays, bp, seq)
            hs = h[tp:, :2 * RET_W].reshape(bs // SAMPLE_BATCH_TILE, sample_rows, 2, RET_HEADS, RET_D)
            hs = jnp.transpose(hs, (2, 0, 3, 4, 1))
            tok_s, ret_sample = retention_sample(h, hs[0], hs[1], state_all, ret_sample, l, tp, bs, n_tok)
            ret_prompt.append(s_p)
            mq_block = (4 * RET_W) // MEM_W
        else:
            if l == n_a:
                w_kv_pad = jnp.concatenate(
                    [w_kv_shared, jnp.zeros((d, LANES - FOX_HEADS), F32)], axis=1).astype(BF16)
                bf_pad = jnp.concatenate([b_forget, jnp.zeros((LANES - FOX_HEADS,), F32)])[None, :]
                kv, logf = kv_shared(x, w_kv_pad, bf_pad)
                c_prompt = cumsum_rows(logf, bp, seq)
                k_aug, v_t, kt_prompt, vt_prompt = fox_prep(kv, c_prompt, bp, seq)
                cache_kt = jnp.transpose(cache_fox_k, (0, 2, 3, 1))
                cache_vt = jnp.transpose(cache_fox_v, (0, 2, 3, 1))
                cache_lf = jnp.pad(jnp.transpose(cache_fox_logf, (0, 2, 1)),
                                   ((0, 0), (0, FOX_COLS - FOX_HEADS), (0, 0)))
            h = matmul(x, w_in_b[l - n_a].astype(BF16))
            tok_p = fox_prompt(h, c_prompt, k_aug, v_t, bp, seq)
            tok_s = fox_sample(h, kv, logf, cache_kt, cache_vt, cache_lf, page_table, tp, bs, n_tok)
            mq_block = FOX_W // MEM_W
        steps_per_b = seq // ROW_TILE
        mem_p = mem_attention(h, mq_block, mem_banks_p, mem_banks_p,
                              lambda i, l=l: (i // steps_per_b) * (2 * depth) + 2 * l,
                              lambda i, l=l: (i // steps_per_b) * (2 * depth) + 2 * l + 1,
                              row_block0=0, n_steps=tp // ROW_TILE, rows=ROW_TILE, banks=1)
        bank_block = lambda i, l=l: l * (bs // SAMPLE_BATCH_TILE) + i
        mem_s = mem_attention(h, mq_block, cache_mkt, cache_mvt, bank_block, bank_block,
                              row_block0=tp // sample_rows, n_steps=bs // SAMPLE_BATCH_TILE,
                              rows=sample_rows, banks=SAMPLE_BATCH_TILE)
        wt = tok_p.shape[1]
        x = out_proj_ln(tok_p, mem_p, tok_s, mem_s, x, w_o[l, :wt].astype(BF16), w_o[l, wt:].astype(BF16),
                        ln1_g[l][None, :], ln1_b[l][None, :], alpha)
        x = hierarchical_moe_ln(x, w_router[l], b_router[l], w_up_e, w_down_e, l * N_EXPERTS,
                                ln2_g[l][None, :], ln2_b[l][None, :], alpha)

    y_prompt = x[:tp].reshape(bp, seq, d)
    y_sample = x[tp:].reshape(bs, n_tok, d)
    fox_k_s = kv[tp:, :FOX_W]
    fox_v_s = kv[tp:, FOX_W:]
    fox_lf = logf[:, :FOX_HEADS]
    return (y_prompt, y_sample, jnp.stack(ret_prompt), ret_sample.reshape(state_ret.shape),
            jnp.transpose(kt_prompt.reshape(bp, FOX_HEADS, FOX_HD, seq), (0, 3, 1, 2)),
            jnp.transpose(vt_prompt.reshape(bp, FOX_HEADS, FOX_HD, seq), (0, 3, 1, 2)),
            fox_lf[:tp].reshape(bp, seq, FOX_HEADS),
            fox_k_s.reshape(bs, n_tok, FOX_HEADS, FOX_HD), fox_v_s.reshape(bs, n_tok, FOX_HEADS, FOX_HD),
            fox_lf[tp:].reshape(bs, n_tok, FOX_HEADS),
            mem_k_prompt, mem_v_prompt)
```

```python
import functools
import math

import jax
import jax.numpy as jnp
from jax import lax
from jax.experimental import pallas as pl
from jax.experimental.pallas import tpu as pltpu

F32 = jnp.float32
BF16 = jnp.bfloat16
HIGHEST = lax.Precision.HIGHEST

RET_HEADS = 6
RET_D = 128
RET_W = RET_HEADS * RET_D
RET_CHUNK = 128
ROPE_BASE = 10000.0
FOX_HEADS = 12
FOX_HD = 64
FOX_W = FOX_HEADS * FOX_HD
MEM_HEADS = 4
MEM_HD = 64
MEM_W = MEM_HEADS * MEM_HD
N_GROUPS = 4
EXPERTS_PER_GROUP = 8
N_EXPERTS = N_GROUPS * EXPERTS_PER_GROUP
D_EXPERT = 256
LN_EPS = 1e-5
HEAD_NORM_EPS = 1e-6

LANES = 128
SUBLANES = 8
VMEM_LIMIT_BYTES = 48 * 1024 * 1024

ROW_TILE = 512
MOE_TILE = 256
EXPERT_TILE = 512
FOX_BLOCK = 1024
SAMPLE_BATCH_TILE = 8


def _cparams(n_axes):
    return pltpu.CompilerParams(
        dimension_semantics=("arbitrary",) * n_axes, vmem_limit_bytes=VMEM_LIMIT_BYTES)


def _dot(a, b):
    return jnp.dot(a, b, preferred_element_type=F32)


def _dot_nt(a, b):
    return lax.dot_general(a, b, (((1,), (1,)), ((), ())), preferred_element_type=F32)


def _dot_tn(a, b):
    return lax.dot_general(a, b, (((0,), (0,)), ((), ())), preferred_element_type=F32)


def _layer_norm(y, g, b):
    mu = jnp.mean(y, axis=-1, keepdims=True)
    d = y - mu
    var = jnp.mean(d * d, axis=-1, keepdims=True)
    return d * lax.rsqrt(var + LN_EPS) * g + b


def _matmul_kernel(x_ref, w_ref, o_ref):
    o_ref[...] = _dot(x_ref[...].astype(BF16), w_ref[...])


def matmul(x, w_bf16):
    t, k = x.shape
    n = w_bf16.shape[1]
    tm = min(ROW_TILE, t)
    return pl.pallas_call(
        _matmul_kernel,
        out_shape=jax.ShapeDtypeStruct((t, n), F32),
        grid=(t // tm,),
        in_specs=[pl.BlockSpec((tm, k), lambda i: (i, 0)),
                  pl.BlockSpec((k, n), lambda i: (0, 0))],
        out_specs=pl.BlockSpec((tm, n), lambda i: (i, 0)),
        compiler_params=_cparams(1),
        name="matmul",
    )(x, w_bf16)


def _proj_a_kernel(x_ref, w_ref, cos_ref, sin_ref, o_ref, *, k_scale):
    x = x_ref[...].astype(BF16)
    cos2 = cos_ref[...]
    sin2 = sin_ref[...]
    for c in range(2 * RET_HEADS):
        cols = slice(c * RET_D, (c + 1) * RET_D)
        y = _dot(x, w_ref[:, cols])
        y = y * cos2 + pltpu.roll(y, RET_D // 2, 1) * sin2
        if c >= RET_HEADS:
            y = y * k_scale
        o_ref[:, cols] = y
    o_ref[:, 2 * RET_W:] = _dot(x, w_ref[:, 2 * RET_W:])


def proj_a(x, w_bf16, cos2, sin2):
    t, k = x.shape
    n = w_bf16.shape[1]
    tm = min(ROW_TILE, t)
    return pl.pallas_call(
        functools.partial(_proj_a_kernel, k_scale=RET_D ** -0.5),
        out_shape=jax.ShapeDtypeStruct((t, n), F32),
        grid=(t // tm,),
        in_specs=[pl.BlockSpec((tm, k), lambda i: (i, 0)),
                  pl.BlockSpec((k, n), lambda i: (0, 0)),
                  pl.BlockSpec((tm, RET_D), lambda i: (i, 0)),
                  pl.BlockSpec((tm, RET_D), lambda i: (i, 0))],
        out_specs=pl.BlockSpec((tm, n), lambda i: (i, 0)),
        compiler_params=_cparams(1),
        name="proj_a",
    )(x, w_bf16, cos2, sin2)


def _kv_kernel(x_ref, w_ref, bf_ref, kv_ref, logf_ref):
    x = x_ref[...].astype(BF16)
    kv_ref[...] = _dot(x, w_ref[:, :2 * FOX_W])
    z = _dot(x, w_ref[:, 2 * FOX_W:]) + bf_ref[...]
    logf_ref[...] = jnp.minimum(z, 0.0) - jnp.log(1.0 + jnp.exp(-jnp.abs(z)))


def kv_shared(x, w_pad_bf16, b_forget_pad):
    t, k = x.shape
    n = w_pad_bf16.shape[1]
    tm = min(ROW_TILE, t)
    return pl.pallas_call(
        _kv_kernel,
        out_shape=(jax.ShapeDtypeStruct((t, 2 * FOX_W), F32),
                   jax.ShapeDtypeStruct((t, LANES), F32)),
        grid=(t // tm,),
        in_specs=[pl.BlockSpec((tm, k), lambda i: (i, 0)),
                  pl.BlockSpec((k, n), lambda i: (0, 0)),
                  pl.BlockSpec((1, LANES), lambda i: (0, 0))],
        out_specs=(pl.BlockSpec((tm, 2 * FOX_W), lambda i: (i, 0)),
                   pl.BlockSpec((tm, LANES), lambda i: (i, 0))),
        compiler_params=_cparams(1),
        name="kv_shared",
    )(x, w_pad_bf16, b_forget_pad)


def _out_ln_kernel(tokp_ref, memp_ref, toks_ref, mems_ref, x_ref, w1_ref, w2_ref, g_ref, b_ref, o_ref, op_ref,
                   *, alpha, prompt_tiles):
    def run(tok_ref, mem_ref):
        mixed = _dot(tok_ref[...].astype(BF16), w1_ref[...]) + _dot(mem_ref[...].astype(BF16), w2_ref[...])
        y = _layer_norm(alpha * x_ref[...] + mixed, g_ref[...], b_ref[...])
        o_ref[...] = y
        op_ref[...] = _pack_rows(y)

    @pl.when(pl.program_id(0) < prompt_tiles)
    def _():
        run(tokp_ref, memp_ref)

    @pl.when(pl.program_id(0) >= prompt_tiles)
    def _():
        run(toks_ref, mems_ref)


def out_proj_ln(tok_p, mem_p, tok_s, mem_s, x, w1_bf16, w2_bf16, g, b, alpha):
    t, d = x.shape
    tm = min(ROW_TILE, t)
    wt, wm = tok_p.shape[1], mem_p.shape[1]
    n_p = tok_p.shape[0] // tm
    n_s = tok_s.shape[0] // tm

    def prompt_map(i):
        return (jnp.minimum(i, n_p - 1), 0)

    def sample_map(i):
        return (jnp.clip(i - n_p, 0, n_s - 1), 0)

    return pl.pallas_call(
        functools.partial(_out_ln_kernel, alpha=alpha, prompt_tiles=n_p),
        out_shape=(jax.ShapeDtypeStruct((t, d), F32), jax.ShapeDtypeStruct((t, d // 2), jnp.uint32)),
        grid=(t // tm,),
        in_specs=[pl.BlockSpec((tm, wt), prompt_map),
                  pl.BlockSpec((tm, wm), prompt_map),
                  pl.BlockSpec((tm, wt), sample_map),
                  pl.BlockSpec((tm, wm), sample_map),
                  pl.BlockSpec((tm, d), lambda i: (i, 0)),
                  pl.BlockSpec((wt, d), lambda i: (0, 0)),
                  pl.BlockSpec((wm, d), lambda i: (0, 0)),
                  pl.BlockSpec((1, d), lambda i: (0, 0)),
                  pl.BlockSpec((1, d), lambda i: (0, 0))],
        out_specs=(pl.BlockSpec((tm, d), lambda i: (i, 0)), pl.BlockSpec((tm, d // 2), lambda i: (i, 0))),
        compiler_params=_cparams(1),
        name="out_proj_ln",
    )(tok_p, mem_p, tok_s, mem_s, x, w1_bf16, w2_bf16, g, b)


def _head_norm_gate(r, g):
    mu = jnp.mean(r, axis=-1, keepdims=True)
    d = r - mu
    var = jnp.mean(d * d, axis=-1, keepdims=True)
    return d * lax.rsqrt(var + HEAD_NORM_EPS) * (g * jax.nn.sigmoid(g))


def _retention_prompt_kernel(q_ref, k_ref, v_ref, g_ref, s0_ref, dm_ref, qd_ref, kd_ref, cd_ref,
                             o_ref, sout_ref, s_scr, *, chunks):
    i = pl.program_id(1)

    @pl.when(i == 0)
    def _():
        s_scr[...] = s0_ref[0]

    for c in range(chunks):
        rows = slice(c * RET_CHUNK, (c + 1) * RET_CHUNK)
        for h in range(RET_HEADS):
            cols = slice(h * RET_D, (h + 1) * RET_D)
            q = q_ref[rows, cols]
            k = k_ref[rows, cols]
            v = v_ref[rows, cols].astype(BF16)
            s_prev = s_scr[h]
            scores = _dot_nt(q.astype(BF16), k.astype(BF16)) * dm_ref[h]
            intra = _dot(scores.astype(BF16), v)
            cross = _dot((q * qd_ref[h]).astype(BF16), s_prev.astype(BF16))
            kd = (k * kd_ref[h]).astype(BF16)
            s_scr[h] = cd_ref[h] * s_prev + _dot_tn(kd, v)
            o_ref[rows, cols] = _head_norm_gate(intra + cross, g_ref[rows, cols])

    @pl.when(i == pl.num_programs(1) - 1)
    def _():
        sout_ref[0] = s_scr[...]


def retention_prompt(h, s0, decays, batch, seq):
    chunks = math.gcd(4, seq // RET_CHUNK)
    rows = chunks * RET_CHUNK
    n_i = seq // rows
    dm, qd, kd, cd = decays
    const = pl.BlockSpec((RET_HEADS, RET_CHUNK, RET_D), lambda b, i: (0, 0, 0))
    state = pl.BlockSpec((1, RET_HEADS, RET_D, RET_D), lambda b, i: (b, 0, 0, 0))

    def col(j):
        return pl.BlockSpec((rows, RET_W), lambda b, i, j=j: (b * n_i + i, j))

    return pl.pallas_call(
        functools.partial(_retention_prompt_kernel, chunks=chunks),
        out_shape=(jax.ShapeDtypeStruct((batch * seq, RET_W), F32),
                   jax.ShapeDtypeStruct((batch, RET_HEADS, RET_D, RET_D), F32)),
        grid=(batch, n_i),
        in_specs=[col(0), col(1), col(2), col(3), state, const, const, const, const],
        out_specs=(pl.BlockSpec((rows, RET_W), lambda b, i: (b * n_i + i, 0)), state),
        scratch_shapes=[pltpu.VMEM((RET_HEADS, RET_D, RET_D), F32)],
        compiler_params=_cparams(2),
        name="retention_prompt",
    )(h, h, h, h, s0, dm, qd, kd, cd)


def _retention_sample_kernel(qt_ref, kt_ref, v_ref, g_ref, s0_ref, sprev_ref, o_ref, sout_ref, r_scr,
                             *, n_req, n_tok, gammas):
    del sprev_ref
    for j in range(n_req):
        for h in range(RET_HEADS):
            cols = slice(h * RET_D, (h + 1) * RET_D)
            s = s0_ref[j, h]
            for t in range(n_tok):
                r = j * n_tok + t
                s = gammas[h] * s + kt_ref[0, h, :, r:r + 1] * v_ref[r:r + 1, cols]
                r_scr[r:r + 1, cols] = jnp.sum(qt_ref[0, h, :, r:r + 1] * s, axis=0, keepdims=True)
            sout_ref[j, h] = s
    for h in range(RET_HEADS):
        cols = slice(h * RET_D, (h + 1) * RET_D)
        o_ref[:, cols] = _head_norm_gate(r_scr[:, cols], g_ref[:, cols])


def retention_sample(h, qt, kt, s_all, s_new_all, layer, row0, n_req_total, n_tok):
    n_req = SAMPLE_BATCH_TILE
    rows = n_req * n_tok
    steps = n_req_total // n_req
    blk0 = row0 // rows
    gammas = tuple(1.0 - 2.0 ** (-5.0 - hh) for hh in range(RET_HEADS))
    state = pl.BlockSpec((n_req, RET_HEADS, RET_D, RET_D), lambda i: (layer * steps + i, 0, 0, 0))
    tr = pl.BlockSpec((1, RET_HEADS, RET_D, rows), lambda i: (i, 0, 0, 0))
    return pl.pallas_call(
        functools.partial(_retention_sample_kernel, n_req=n_req, n_tok=n_tok, gammas=gammas),
        out_shape=(jax.ShapeDtypeStruct((n_req_total * n_tok, RET_W), F32),
                   jax.ShapeDtypeStruct(s_new_all.shape, F32)),
        grid=(steps,),
        in_specs=[tr, tr,
                  pl.BlockSpec((rows, RET_W), lambda i: (blk0 + i, 2)),
                  pl.BlockSpec((rows, RET_W), lambda i: (blk0 + i, 3)),
                  state,
                  pl.BlockSpec(memory_space=pl.ANY)],
        out_specs=(pl.BlockSpec((rows, RET_W), lambda i: (i, 0)), state),
        scratch_shapes=[pltpu.VMEM((rows, RET_W), F32)],
        input_output_aliases={5: 1},
        compiler_params=_cparams(1),
        name="retention_sample",
    )(qt, kt, h, h, s_all, s_new_all)


def _mem_attn_few_rows(q_ref, mk_ref, mv_ref, o_ref, *, banks, rows_per_bank):
    q = q_ref[...] * (MEM_HD ** -0.5)
    mem_len = mk_ref.shape[3]
    per_group = SUBLANES // rows_per_bank
    n_bd = MEM_HEADS * SUBLANES
    head_of_lane = lax.broadcasted_iota(jnp.int32, (SUBLANES, MEM_W), 1) // MEM_HD
    bank_of_row = lax.broadcasted_iota(jnp.int32, (SUBLANES, MEM_W), 0) // rows_per_bank
    a_row = lax.broadcasted_iota(jnp.int32, (n_bd, MEM_W), 0)
    a_col = lax.broadcasted_iota(jnp.int32, (n_bd, MEM_W), 1)
    own_head = a_col // MEM_HD == a_row // SUBLANES
    for grp in range(banks // per_group):
        q8 = q[grp * SUBLANES:(grp + 1) * SUBLANES, :]
        out8 = jnp.zeros((SUBLANES, MEM_W), F32)
        for w in range(per_group):
            j = grp * per_group + w
            q_bd = jnp.concatenate(
                [jnp.where((head_of_lane == h) & (bank_of_row == w), q8, 0.0) for h in range(MEM_HEADS)],
                axis=0).astype(BF16)
            s = _dot(q_bd, mk_ref[j].reshape(MEM_W, mem_len).astype(BF16))
            p = jnp.exp(s - jnp.max(s, axis=1, keepdims=True))
            p = p / jnp.sum(p, axis=1, keepdims=True)
            acc = _dot_nt(p.astype(BF16), mv_ref[j].reshape(MEM_W, mem_len).astype(BF16))
            res = jnp.where(own_head, acc, 0.0).reshape(MEM_HEADS, SUBLANES, MEM_W).sum(axis=0)
            out8 = jnp.where(bank_of_row == w, res, out8)
        o_ref[grp * SUBLANES:(grp + 1) * SUBLANES, :] = out8


def _mem_attn_kernel(q_ref, mk_ref, mv_ref, o_ref, *, banks, rows_per_bank):
    if rows_per_bank < SUBLANES:
        _mem_attn_few_rows(q_ref, mk_ref, mv_ref, o_ref, banks=banks, rows_per_bank=rows_per_bank)
        return
    assert banks == 1
    q = q_ref[...] * (MEM_HD ** -0.5)
    n_rows = q.shape[0]
    n_cols = max(n_rows, LANES)
    if n_cols > n_rows:
        q = jnp.concatenate([q, jnp.zeros((n_cols - n_rows, MEM_W), F32)], axis=0)
    qt = q.T.astype(BF16)
    heads = []
    for h in range(MEM_HEADS):
        mk = mk_ref[0, h].T.astype(BF16)
        s = _dot(mk, qt[h * MEM_HD:(h + 1) * MEM_HD, :])
        p = jnp.exp(s - jnp.max(s, axis=0, keepdims=True))
        p = p / jnp.sum(p, axis=0, keepdims=True)
        heads.append(_dot(mv_ref[0, h].astype(BF16), p.astype(BF16)))
    o_ref[...] = jnp.concatenate(heads, axis=0).T[:n_rows, :]


def mem_attention(h, q_col_block, mkt, mvt, k_block, v_block, *, row_block0, n_steps, rows, banks):
    mem_len = mkt.shape[3]
    bank_shape = (banks, MEM_HEADS, MEM_HD, mem_len)
    return pl.pallas_call(
        functools.partial(_mem_attn_kernel, banks=banks, rows_per_bank=rows // banks),
        out_shape=jax.ShapeDtypeStruct((n_steps * rows, MEM_W), F32),
        grid=(n_steps,),
        in_specs=[pl.BlockSpec((rows, MEM_W), lambda i: (row_block0 + i, q_col_block)),
                  pl.BlockSpec(bank_shape, lambda i: (k_block(i), 0, 0, 0)),
                  pl.BlockSpec(bank_shape, lambda i: (v_block(i), 0, 0, 0))],
        out_specs=pl.BlockSpec((rows, MEM_W), lambda i: (i, 0)),
        compiler_params=_cparams(1),
        name="mem_attention",
    )(h, mkt, mvt)


def _mem_kv_kernel(x_ref, wt_ref, o_ref):
    o_ref[0] = _dot_nt(wt_ref[...], x_ref[...].astype(BF16))


def mem_kv_transposed(mem, w_t_bf16):
    b, m, d = mem.shape
    f = w_t_bf16.shape[0]
    return pl.pallas_call(
        _mem_kv_kernel,
        out_shape=jax.ShapeDtypeStruct((b, f, m), F32),
        grid=(b,),
        in_specs=[pl.BlockSpec((m, d), lambda i: (i, 0)),
                  pl.BlockSpec((f, d), lambda i: (0, 0))],
        out_specs=pl.BlockSpec((1, f, m), lambda i: (i, 0, 0)),
        compiler_params=_cparams(1),
        name="mem_kv",
    )(mem.reshape(b * m, d), w_t_bf16)


def _cumsum_kernel(x_ref, o_ref, carry):
    @pl.when(pl.program_id(1) == 0)
    def _():
        carry[...] = jnp.zeros_like(carry)

    n = x_ref.shape[0]
    row = lax.broadcasted_iota(jnp.int32, (n, n), 0)
    col = lax.broadcasted_iota(jnp.int32, (n, n), 1)
    tri = (row >= col).astype(F32)
    c = jnp.dot(tri, x_ref[...], precision=HIGHEST, preferred_element_type=F32) + carry[...]
    o_ref[...] = c
    carry[...] = c[n - 1:n, :]


def cumsum_rows(x, batch, seq):
    tm = min(ROW_TILE, seq)
    n_i = seq // tm
    return pl.pallas_call(
        _cumsum_kernel,
        out_shape=jax.ShapeDtypeStruct((batch * seq, x.shape[1]), F32),
        grid=(batch, n_i),
        in_specs=[pl.BlockSpec((tm, x.shape[1]), lambda b, i: (b * n_i + i, 0))],
        out_specs=pl.BlockSpec((tm, x.shape[1]), lambda b, i: (b * n_i + i, 0)),
        scratch_shapes=[pltpu.VMEM((1, x.shape[1]), F32)],
        compiler_params=_cparams(2),
        name="cumsum_rows",
    )(x)


BIAS_COLS = 12
LOG2E = math.log2(math.e)


def _split3(x):
    hi = x.astype(BF16).astype(F32)
    mid = (x - hi).astype(BF16).astype(F32)
    lo = x - hi - mid
    return hi, mid, lo


def _place(lane, base, parts):
    out = jnp.zeros(lane.shape, F32)
    for idx, part in enumerate(parts):
        out = jnp.where(lane == base + idx, part, out)
    return out


def _fox_prep_kernel(k_ref, v_ref, c_ref, ka_ref, vt_ref, ktf_ref, vtf_ref):
    p = pl.program_id(0)
    rows = k_ref.shape[0]
    lane = lax.broadcasted_iota(jnp.int32, (rows, LANES), 1)
    c2 = c_ref[...]
    bias = jnp.zeros((rows, LANES), F32)
    for e in range(2):
        ck = jnp.sum(jnp.where(lane == 2 * p + e, c2, 0.0), axis=1, keepdims=True)
        base = e * (BIAS_COLS // 2)
        ones = jnp.where((lane >= base) & (lane < base + 3), 1.0, 0.0)
        bias = bias + ones + _place(lane, base + 3, [-part for part in _split3(ck * LOG2E)])
    ka_ref[0] = jnp.concatenate([k_ref[...], bias], axis=1).astype(BF16)
    v_t = v_ref[...].T
    vt_ref[0] = v_t.astype(BF16)
    vtf_ref[0] = v_t
    ktf_ref[0] = k_ref[...].T


def fox_prep(kv, c, batch, seq):
    pairs = FOX_HEADS // 2
    n_rows = batch * seq
    tm = min(2 * FOX_BLOCK, seq)
    per_b = seq // tm
    t_spec = pl.BlockSpec((1, LANES, tm), lambda p, i: (i // per_b, p, i % per_b))
    return pl.pallas_call(
        _fox_prep_kernel,
        out_shape=(jax.ShapeDtypeStruct((pairs, n_rows, 2 * LANES), BF16),
                   jax.ShapeDtypeStruct((pairs, LANES, n_rows), BF16),
                   jax.ShapeDtypeStruct((batch, FOX_W, seq), F32),
                   jax.ShapeDtypeStruct((batch, FOX_W, seq), F32)),
        grid=(pairs, n_rows // tm),
        in_specs=[pl.BlockSpec((tm, LANES), lambda p, i: (i, p)),
                  pl.BlockSpec((tm, LANES), lambda p, i: (i, pairs + p)),
                  pl.BlockSpec((tm, LANES), lambda p, i: (i, 0))],
        out_specs=(pl.BlockSpec((1, tm, 2 * LANES), lambda p, i: (p, i, 0)),
                   pl.BlockSpec((1, LANES, tm), lambda p, i: (p, 0, i)),
                   t_spec, t_spec),
        compiler_params=_cparams(2),
        name="fox_prep",
    )(kv, kv, c)


def _fox_prompt_kernel(q_ref, c_ref, ka_ref, vt_ref, o_ref, acc_a, acc_b, *, blk):
    p = pl.program_id(1)
    i = pl.program_id(2)
    lane = lax.broadcasted_iota(jnp.int32, (blk, LANES), 1)
    q2 = q_ref[...] * (FOX_HD ** -0.5 * LOG2E)
    c2 = c_ref[...] * LOG2E
    qts = []
    for e in range(2):
        head_lanes = (lane >= e * FOX_HD) & (lane < (e + 1) * FOX_HD)
        cq = jnp.sum(jnp.where(lane == 2 * p + e, c2, 0.0), axis=1, keepdims=True)
        base = e * (BIAS_COLS // 2)
        ones = jnp.where((lane >= base + 3) & (lane < base + 6), 1.0, 0.0)
        q_aug = jnp.concatenate([jnp.where(head_lanes, q2, 0.0), ones + _place(lane, base, _split3(cq))], axis=1)
        qts.append(q_aug.T.astype(BF16))
    accs = (acc_a, acc_b)
    acc_a[...] = jnp.zeros(acc_a.shape, F32)
    acc_b[...] = jnp.zeros(acc_b.shape, F32)

    def block(j, carry, masked):
        start = pl.multiple_of(j * blk, blk)
        kb = ka_ref[0, pl.ds(start, blk), :]
        vt = vt_ref[0, :, pl.ds(start, blk)]
        new = []
        for e in range(2):
            m_old, l_old = carry[2 * e], carry[2 * e + 1]
            s = _dot(kb, qts[e])
            if masked:
                key = lax.broadcasted_iota(jnp.int32, (blk, blk), 0)
                qry = lax.broadcasted_iota(jnp.int32, (blk, blk), 1)
                s = jnp.where(key <= qry, s, -jnp.inf)
            m_new = jnp.maximum(m_old, jnp.max(s, axis=0, keepdims=True))
            a = jnp.exp2(m_old - m_new)
            pe = jnp.exp2(s - m_new)
            new += [m_new, a * l_old + jnp.sum(pe, axis=0, keepdims=True)]
            acc = accs[e]
            acc[...] = a * acc[...] + _dot(vt[e * FOX_HD:(e + 1) * FOX_HD, :], pe.astype(BF16))
        return tuple(new)

    init = (jnp.full((1, blk), -jnp.inf, F32), jnp.zeros((1, blk), F32)) * 2
    carry = lax.fori_loop(0, i, lambda j, c: block(j, c, False), init)
    carry = block(i, carry, True)
    out_t = jnp.concatenate([acc_a[...] / carry[1], acc_b[...] / carry[3]], axis=0)
    o_ref[...] = out_t.T


def fox_prompt(h, c, k_aug, v_t, batch, seq):
    blk = min(FOX_BLOCK, seq)
    n_q = seq // blk
    pairs = FOX_HEADS // 2
    return pl.pallas_call(
        functools.partial(_fox_prompt_kernel, blk=blk),
        out_shape=jax.ShapeDtypeStruct((batch * seq, FOX_W), F32),
        grid=(batch, pairs, n_q),
        in_specs=[pl.BlockSpec((blk, LANES), lambda b, p, i: (b * n_q + i, p)),
                  pl.BlockSpec((blk, LANES), lambda b, p, i: (b * n_q + i, 0)),
                  pl.BlockSpec((1, seq, 2 * LANES), lambda b, p, i: (p, b, 0)),
                  pl.BlockSpec((1, LANES, seq), lambda b, p, i: (p, 0, b))],
        out_specs=pl.BlockSpec((blk, LANES), lambda b, p, i: (b * n_q + i, p)),
        scratch_shapes=[pltpu.VMEM((FOX_HD, blk), F32), pltpu.VMEM((FOX_HD, blk), F32)],
        compiler_params=_cparams(3),
        name="fox_prompt",
    )(h, c, k_aug, v_t)


FOX_COLS = 16


def _fox_sample_kernel(pt_ref, q_ref, kvn_ref, lfn_ref, *refs, n_pages, page, n_tok):
    del pt_ref
    kt_refs = refs[:n_pages]
    vt_refs = refs[n_pages:2 * n_pages]
    lf_refs = refs[2 * n_pages:3 * n_pages]
    o_ref = refs[3 * n_pages]
    n_rows = n_tok * FOX_COLS
    reqs = SUBLANES // n_tok
    which = pl.program_id(0) % reqs

    def pick(block):
        out = block[0:n_tok, :]
        for w in range(1, reqs):
            out = jnp.where(which == w, block[w * n_tok:(w + 1) * n_tok, :], out)
        return out

    q = pick(q_ref[...])[:, :FOX_W] * (FOX_HD ** -0.5)
    kvn = pick(kvn_ref[...])
    lfn = pick(lfn_ref[...])
    head_of_lane = lax.broadcasted_iota(jnp.int32, (FOX_COLS, FOX_W), 1) // FOX_HD
    head_of_row = lax.broadcasted_iota(jnp.int32, (FOX_COLS, FOX_W), 0)
    q_rows = [jnp.where(head_of_lane == head_of_row, jnp.broadcast_to(q[t:t + 1, :], (FOX_COLS, FOX_W)), 0.0)
              for t in range(n_tok)]
    q_bd = jnp.concatenate(q_rows, axis=0).astype(BF16)

    kt = jnp.concatenate([r[0].reshape(FOX_W, page) for r in kt_refs], axis=1).astype(BF16)
    s = _dot(q_bd, kt)

    t_row = lax.broadcasted_iota(jnp.int32, (page, page), 0)
    t_col = lax.broadcasted_iota(jnp.int32, (page, page), 1)
    upto = (t_row <= t_col).astype(F32)
    carry = jnp.zeros((FOX_COLS, 1), F32)
    c_pages = []
    for pg in range(n_pages):
        c_pg = jnp.dot(lf_refs[pg][0], upto, precision=HIGHEST, preferred_element_type=F32) + carry
        carry = c_pg[:, page - 1:page]
        c_pages.append(c_pg)
    c_past = jnp.concatenate(c_pages, axis=1)

    lfn_t = jnp.concatenate([lfn, jnp.zeros((SUBLANES - n_tok, LANES), F32)], axis=0).T[:FOX_COLS, :]
    c_new = []
    for t in range(n_tok):
        carry = carry + lfn_t[:, t:t + 1]
        c_new.append(carry)
    cq = jnp.concatenate(c_new, axis=0)
    s = s + (cq - jnp.concatenate([c_past] * n_tok, axis=0))

    s_new = _dot_nt(q_bd, kvn[:, :FOX_W].astype(BF16))
    ck_new = jnp.concatenate([jnp.concatenate([c] * n_tok, axis=0) for c in c_new], axis=1)
    tok_of_row = lax.broadcasted_iota(jnp.int32, (n_rows, n_tok), 0) // FOX_COLS
    key_tok = lax.broadcasted_iota(jnp.int32, (n_rows, n_tok), 1)
    s_new = jnp.where(key_tok <= tok_of_row, s_new + (cq - ck_new), -jnp.inf)

    m = jnp.maximum(jnp.max(s, axis=1, keepdims=True), jnp.max(s_new, axis=1, keepdims=True))
    p = jnp.exp(s - m)
    p_new = jnp.exp(s_new - m)
    inv = 1.0 / (jnp.sum(p, axis=1, keepdims=True) + jnp.sum(p_new, axis=1, keepdims=True))
    vt = jnp.concatenate([r[0].reshape(FOX_W, page) for r in vt_refs], axis=1).astype(BF16)
    acc = _dot_nt((p * inv).astype(BF16), vt)
    p_new = p_new * inv
    v_new = kvn[:, FOX_W:]
    for t in range(n_tok):
        acc = acc + p_new[:, t:t + 1] * v_new[t:t + 1, :]

    a_row = lax.broadcasted_iota(jnp.int32, (n_rows, FOX_W), 0)
    a_col = lax.broadcasted_iota(jnp.int32, (n_rows, FOX_W), 1)
    acc = jnp.where(a_col // FOX_HD == a_row % FOX_COLS, acc, 0.0)
    out = acc.reshape(n_tok, FOX_COLS, FOX_W).sum(axis=1)
    for w in range(reqs):
        @pl.when(which == w)
        def _(w=w):
            o_ref[w * n_tok:(w + 1) * n_tok, :] = out


def fox_sample(h, kv, logf, cache_kt, cache_vt, cache_lf, page_table, row0, n_req, n_tok):
    n_pages = page_table.shape[1]
    page = cache_kt.shape[3]
    reqs = SUBLANES // n_tok
    blk0 = row0 // SUBLANES

    def new_spec(width):
        return pl.BlockSpec((SUBLANES, width), lambda r, pt: (blk0 + r // reqs, 0))

    def kv_spec(pg):
        return pl.BlockSpec((1, FOX_HEADS, FOX_HD, page), lambda r, pt, pg=pg: (pt[r * n_pages + pg], 0, 0, 0))

    def lf_spec(pg):
        return pl.BlockSpec((1, FOX_COLS, page), lambda r, pt, pg=pg: (pt[r * n_pages + pg], 0, 0))

    grid_spec = pltpu.PrefetchScalarGridSpec(
        num_scalar_prefetch=1,
        grid=(n_req,),
        in_specs=[new_spec(h.shape[1]), new_spec(kv.shape[1]), new_spec(logf.shape[1])]
        + [kv_spec(pg) for pg in range(n_pages)] + [kv_spec(pg) for pg in range(n_pages)]
        + [lf_spec(pg) for pg in range(n_pages)],
        out_specs=pl.BlockSpec((SUBLANES, FOX_W), lambda r, pt: (r // reqs, 0)),
    )
    return pl.pallas_call(
        functools.partial(_fox_sample_kernel, n_pages=n_pages, page=page, n_tok=n_tok),
        out_shape=jax.ShapeDtypeStruct((n_req * n_tok, FOX_W), F32),
        grid_spec=grid_spec,
        compiler_params=_cparams(1),
        name="fox_sample",
    )(page_table.reshape(-1), h, kv, logf, *([cache_kt] * n_pages), *([cache_vt] * n_pages),
      *([cache_lf] * n_pages))


def _router_kernel(x_ref, wh_ref, wl_ref, b_ref, o_ref, cnt_ref, carry):
    @pl.when(pl.program_id(0) == 0)
    def _():
        carry[...] = jnp.zeros_like(carry)

    x = x_ref[...]
    x_hi = x.astype(BF16)
    x_lo = (x - x_hi.astype(F32)).astype(BF16)
    logits = (_dot(x_hi, wh_ref[...]) + (_dot(x_hi, wl_ref[...]) + _dot(x_lo, wh_ref[...]))) + b_ref[...]
    lane = lax.broadcasted_iota(jnp.int32, logits.shape, 1)
    big = jnp.int32(LANES)
    neg = -jnp.inf
    gl = jnp.where(lane < N_GROUPS, logits, neg)
    g_max = jnp.max(gl, axis=-1, keepdims=True)
    g_idx = jnp.min(jnp.where(gl == g_max, lane, big), axis=-1, keepdims=True)
    g_w = 1.0 / jnp.sum(jnp.exp(gl - g_max), axis=-1, keepdims=True)
    in_group = (lane >= N_GROUPS) & (lane < N_GROUPS + N_EXPERTS) & (((lane - N_GROUPS) >> 3) == g_idx)
    el = jnp.where(in_group, logits, neg)
    v1 = jnp.max(el, axis=-1, keepdims=True)
    i1 = jnp.min(jnp.where(el == v1, lane, big), axis=-1, keepdims=True)
    el2 = jnp.where(lane == i1, neg, el)
    v2 = jnp.max(el2, axis=-1, keepdims=True)
    i2 = jnp.min(jnp.where(el2 == v2, lane, big), axis=-1, keepdims=True)
    e2 = jnp.exp(v2 - v1)
    w1 = g_w / (1.0 + e2)
    w2 = g_w * e2 / (1.0 + e2)
    tm = logits.shape[0]
    e1 = i1 - N_GROUPS
    e2 = i2 - N_GROUPS
    oh1 = jnp.where(lane == e1, 1.0, 0.0)
    oh2 = jnp.where(lane == e2, 1.0, 0.0)
    row = lax.broadcasted_iota(jnp.int32, (tm, tm), 0)
    col = lax.broadcasted_iota(jnp.int32, (tm, tm), 1)
    before = jnp.where(col < row, 1.0, 0.0).astype(BF16)
    tot1 = jnp.sum(oh1, axis=0, keepdims=True)
    base = carry[...]
    r1 = jnp.sum(oh1 * (base + _dot(before, oh1.astype(BF16))), axis=-1, keepdims=True)
    r2 = jnp.sum(oh2 * (base + tot1 + _dot(before, oh2.astype(BF16))), axis=-1, keepdims=True)
    carry[...] = base + tot1 + jnp.sum(oh2, axis=0, keepdims=True)
    cnt_ref[...] = carry[...]
    out = jnp.zeros(logits.shape, F32)
    for idx, val in enumerate((e1.astype(F32), e2.astype(F32), w1, w2, r1, r2)):
        out = jnp.where(lane == idx, val, out)
    o_ref[...] = out


def router(x, w_pad, b_pad):
    t, d = x.shape
    tm = min(ROW_TILE, t)
    w_hi = w_pad.astype(BF16)
    w_lo = (w_pad - w_hi.astype(F32)).astype(BF16)
    return pl.pallas_call(
        _router_kernel,
        out_shape=(jax.ShapeDtypeStruct((t, LANES), F32), jax.ShapeDtypeStruct((1, LANES), F32)),
        grid=(t // tm,),
        in_specs=[pl.BlockSpec((tm, d), lambda i: (i, 0)),
                  pl.BlockSpec((d, LANES), lambda i: (0, 0)),
                  pl.BlockSpec((d, LANES), lambda i: (0, 0)),
                  pl.BlockSpec((1, LANES), lambda i: (0, 0))],
        out_specs=(pl.BlockSpec((tm, LANES), lambda i: (i, 0)), pl.BlockSpec((1, LANES), lambda i: (0, 0))),
        scratch_shapes=[pltpu.VMEM((1, LANES), F32)],
        compiler_params=_cparams(1),
        name="router",
    )(x, w_hi, w_lo, b_pad)


def _pack_rows(y):
    n = y.shape[1] // 2
    bits = lax.bitcast_convert_type(y.astype(BF16).astype(F32), jnp.uint32)
    return bits[:, :n] | (bits[:, n:] >> 16)


def _unpack_rows(w):
    return (lax.bitcast_convert_type(w & jnp.uint32(0xFFFF0000), F32),
            lax.bitcast_convert_type(w << 16, F32))


def _row_copy(src_ref, src_row, dst_ref, dst_row, sem):
    return pltpu.make_async_copy(src_ref.at[pl.ds(src_row, 1)], dst_ref.at[pl.ds(dst_row, 1)], sem)


def _drain_rows(src_ref, dst_ref, sem, n_rows):
    def drain(r, carry):
        _row_copy(src_ref, 0, dst_ref, 0, sem).wait()
        return carry

    lax.fori_loop(0, n_rows, drain, 0, unroll=8)


def _expert_kernel(te_ref, na_ref, tok_ref, tokn_ref, xp_ref, wu_ref, wd_ref, o_ref, xbuf, sems, wu_bf, wd_bf,
                   *, tm):
    i = pl.program_id(0)
    n_active = na_ref[0]
    active = i < n_active

    def gather(t_ref, slot):
        def issue(r, carry):
            for u in range(2):
                _row_copy(xp_ref, t_ref[0, 0, 2 * r + u], xbuf.at[slot], 2 * r + u, sems.at[slot]).start(priority=u)
            return carry

        lax.fori_loop(0, tm // 2, issue, 0, unroll=8)

    @pl.when(i == 0)
    def _():
        gather(tok_ref, 0)

    changed = jnp.logical_or(i == 0, te_ref[i] != te_ref[jnp.maximum(i - 1, 0)])

    @pl.when(jnp.logical_and(active, changed))
    def _():
        wu_bf[...] = wu_ref[0].astype(BF16)
        wd_bf[...] = wd_ref[0].astype(BF16)

    for slot in range(2):
        @pl.when(jnp.logical_and(active, i % 2 == slot))
        def _(slot=slot):
            @pl.when(i + 1 < n_active)
            def _():
                gather(tokn_ref, 1 - slot)

            _drain_rows(xp_ref, xbuf.at[slot], sems.at[slot], tm)
            half = wu_bf.shape[0] // 2
            xa, xb = _unpack_rows(xbuf[slot])
            up = _dot(xa.astype(BF16), wu_bf[:half, :]) + _dot(xb.astype(BF16), wu_bf[half:, :])
            hid = jax.nn.silu(up[:, :D_EXPERT]) * up[:, D_EXPERT:]
            o_ref[...] = _dot(hid.astype(BF16), wd_bf[...])

    @pl.when(jnp.logical_not(active))
    def _():
        o_ref[...] = jnp.zeros_like(o_ref)


def moe_experts(xp, slot_token, w_up, w_down, tile_expert, n_active):
    n_tiles = slot_token.shape[0]
    d = w_up.shape[1]
    tm = EXPERT_TILE

    def cur(i, te, na):
        return (jnp.minimum(i, na[0] - 1), 0, 0)

    def nxt(i, te, na):
        return (jnp.minimum(i + 1, na[0] - 1), 0, 0)

    grid_spec = pltpu.PrefetchScalarGridSpec(
        num_scalar_prefetch=2,
        grid=(n_tiles,),
        in_specs=[pl.BlockSpec((1, 1, tm), cur, memory_space=pltpu.SMEM),
                  pl.BlockSpec((1, 1, tm), nxt, memory_space=pltpu.SMEM),
                  pl.BlockSpec(memory_space=pl.ANY),
                  pl.BlockSpec((1, d, 2 * D_EXPERT), lambda i, te, na: (te[i], 0, 0)),
                  pl.BlockSpec((1, D_EXPERT, d), lambda i, te, na: (te[i], 0, 0))],
        out_specs=pl.BlockSpec((tm, d), lambda i, te, na: (i, 0)),
        scratch_shapes=[pltpu.VMEM((2, tm, d // 2), jnp.uint32), pltpu.SemaphoreType.DMA((2,)),
                        pltpu.VMEM((d, 2 * D_EXPERT), BF16), pltpu.VMEM((D_EXPERT, d), BF16)],
    )
    return pl.pallas_call(
        functools.partial(_expert_kernel, tm=tm),
        out_shape=jax.ShapeDtypeStruct((n_tiles * tm, d), F32),
        grid_spec=grid_spec,
        compiler_params=_cparams(1),
        name="moe_experts",
    )(tile_expert, n_active, slot_token, slot_token, xp, w_up, w_down)


def _combine_kernel(pos_ref, posn_ref, x_ref, rt_ref, g_ref, b_ref, ys_ref, o_ref, buf, sems, *, tm, alpha):
    i = pl.program_id(0)
    n = pl.num_programs(0)

    def gather(p_ref, slot):
        def issue(r, carry):
            for k in range(2):
                _row_copy(ys_ref, p_ref[0, 0, k * tm + r], buf.at[slot, k], r, sems.at[slot]).start(priority=k)
            return carry

        lax.fori_loop(0, tm, issue, 0, unroll=8)

    @pl.when(i == 0)
    def _():
        gather(pos_ref, 0)

    for slot in range(2):
        @pl.when(i % 2 == slot)
        def _(slot=slot):
            @pl.when(i + 1 < n)
            def _():
                gather(posn_ref, 1 - slot)

            _drain_rows(ys_ref, buf.at[slot, 0], sems.at[slot], 2 * tm)
            rt = rt_ref[...]
            ffn = rt[:, 2:3] * buf[slot, 0] + rt[:, 3:4] * buf[slot, 1]
            o_ref[...] = _layer_norm(alpha * x_ref[...] + ffn, g_ref[...], b_ref[...])


def moe_combine(x, routing, ys, pos3, g, b, alpha):
    t, d = x.shape
    tm = MOE_TILE
    n = t // tm
    return pl.pallas_call(
        functools.partial(_combine_kernel, tm=tm, alpha=alpha),
        out_shape=jax.ShapeDtypeStruct((t, d), F32),
        grid=(n,),
        in_specs=[pl.BlockSpec((1, 1, 2 * tm), lambda i: (i, 0, 0), memory_space=pltpu.SMEM),
                  pl.BlockSpec((1, 1, 2 * tm), lambda i: (jnp.minimum(i + 1, n - 1), 0, 0),
                               memory_space=pltpu.SMEM),
                  pl.BlockSpec((tm, d), lambda i: (i, 0)),
                  pl.BlockSpec((tm, LANES), lambda i: (i, 0)),
                  pl.BlockSpec((1, d), lambda i: (0, 0)),
                  pl.BlockSpec((1, d), lambda i: (0, 0)),
                  pl.BlockSpec(memory_space=pl.ANY)],
        out_specs=pl.BlockSpec((tm, d), lambda i: (i, 0)),
        scratch_shapes=[pltpu.VMEM((2, 2, tm, d), F32), pltpu.SemaphoreType.DMA((2,))],
        compiler_params=_cparams(1),
        name="moe_combine",
    )(pos3, pos3, x, routing, g, b, ys)


def moe_plan(routing, counts, n_tiles, expert0):
    t = routing.shape[0]
    tm = MOE_TILE
    experts = jnp.arange(N_EXPERTS, dtype=jnp.int32)
    cnt = counts[0, :N_EXPERTS].astype(jnp.int32)
    tiles_e = (cnt + EXPERT_TILE - 1) // EXPERT_TILE
    tile_end = jnp.cumsum(tiles_e)
    offs = (tile_end - tiles_e) * EXPERT_TILE
    ids = routing[:, 0:2].astype(jnp.int32)
    rank = routing[:, 4:6].astype(jnp.int32)
    pos = jnp.sum(jnp.where(ids[:, :, None] == experts, offs, 0), axis=-1) + rank
    n_active = tile_end[-1:]
    tile_ids = jnp.minimum(jnp.arange(n_tiles, dtype=jnp.int32), n_active[0] - 1)
    tile_expert = jnp.minimum(jnp.sum((tile_ids[:, None] >= tile_end[None, :]).astype(jnp.int32), axis=1),
                              N_EXPERTS - 1) + expert0
    pos3 = pos.reshape(t // tm, tm, 2).transpose(0, 2, 1).reshape(t // tm, 1, 2 * tm)
    tokens = jnp.broadcast_to(jnp.arange(t, dtype=jnp.int32)[:, None], (t, 2))
    slot_token = jnp.zeros((n_tiles * EXPERT_TILE,), jnp.int32).at[pos.reshape(-1)].set(
        tokens.reshape(-1), unique_indices=True)
    return pos3, slot_token.reshape(n_tiles, 1, EXPERT_TILE), tile_expert, n_active


def hierarchical_moe_ln(x, x_packed, w_router, b_router, w_up, w_down, expert0, g, b, alpha):
    t, d = x.shape
    n_tiles = (2 * t) // EXPERT_TILE + N_EXPERTS
    routing, counts = router(x, w_router, b_router)
    pos3, slot_token, tile_expert, n_active = moe_plan(routing, counts, n_tiles, expert0)
    ys = moe_experts(x_packed, slot_token, w_up, w_down, tile_expert, n_active)
    return moe_combine(x, routing, ys, pos3, g, b, alpha)


def _rotary_tables(pos):
    half = RET_D // 2
    inv_freq = ROPE_BASE ** (-jnp.arange(half, dtype=F32) / half)
    ang = pos.astype(F32)[:, None] * inv_freq[None, :]
    cos, sin = jnp.cos(ang), jnp.sin(ang)
    return jnp.concatenate([cos, cos], axis=1), jnp.concatenate([-sin, sin], axis=1)


def _retention_decays():
    log_gamma = jnp.log(1.0 - 2.0 ** (-5.0 - jnp.arange(RET_HEADS, dtype=F32)))
    c = RET_CHUNK
    i = jnp.arange(c, dtype=F32)
    diff = i[:, None] - i[None, :]
    dm = jnp.where(diff >= 0, jnp.exp(log_gamma[:, None, None] * jnp.maximum(diff, 0.0)[None]), 0.0)
    shape = (RET_HEADS, c, RET_D)
    qd = jnp.broadcast_to(jnp.exp(log_gamma[:, None] * (i + 1.0)[None, :])[:, :, None], shape)
    kd = jnp.broadcast_to(jnp.exp(log_gamma[:, None] * (c - 1 - i)[None, :])[:, :, None], shape)
    cd = jnp.broadcast_to(jnp.exp(log_gamma * c)[:, None, None], shape)
    return dm, qd, kd, cd


def kernel(x_prompt, x_sample, mem_prompt, state_ret, cache_fox_k, cache_fox_v, cache_fox_logf, cache_mem_k, cache_mem_v, page_table, w_in_a, w_in_b, w_kv_shared, b_forget, w_o, w_mem_kv, ln1_g, ln1_b, ln2_g, ln2_b, w_group, b_group, w_route, b_route, w_up, w_down):
    bp, seq, d = x_prompt.shape
    bs, n_tok, _ = x_sample.shape
    depth = w_o.shape[0]
    n_a = w_in_a.shape[0]
    mem_len = mem_prompt.shape[1]
    n_pages, page = page_table.shape[1], cache_fox_k.shape[1]
    past = n_pages * page
    tp, ts = bp * seq, bs * n_tok
    alpha = (2 * depth) ** 0.25
    assert w_in_a.shape[2] == 4 * RET_W + MEM_W and w_in_b.shape[2] == FOX_W + MEM_W
    assert w_kv_shared.shape[1] == 2 * FOX_W + FOX_HEADS and w_up.shape[1:3] == (N_GROUPS, EXPERTS_PER_GROUP)
    assert seq % RET_CHUNK == 0 and SUBLANES % n_tok == 0 and tp % ROW_TILE == 0 and ts % ROW_TILE == 0

    x = jnp.concatenate([x_prompt.reshape(tp, d), x_sample.reshape(ts, d)], axis=0)
    t_all = tp + ts

    w_mem_t = jnp.transpose(w_mem_kv, (0, 2, 1)).reshape(depth * 2 * MEM_W, d).astype(BF16)
    mem_kv_t = mem_kv_transposed(mem_prompt, w_mem_t)
    mem_kv6 = mem_kv_t.reshape(bp, depth, 2, MEM_HEADS, MEM_HD, mem_len)
    mem_k_prompt = jnp.transpose(mem_kv6[:, :, 0], (1, 0, 4, 2, 3))
    mem_v_prompt = jnp.transpose(mem_kv6[:, :, 1], (1, 0, 4, 2, 3))
    mem_banks_p = mem_kv_t.reshape(bp * depth * 2, MEM_HEADS, MEM_HD, mem_len)
    cache_mkt = jnp.transpose(cache_mem_k, (0, 1, 3, 4, 2)).reshape(depth * bs, MEM_HEADS, MEM_HD, mem_len)
    cache_mvt = jnp.transpose(cache_mem_v, (0, 1, 3, 4, 2)).reshape(depth * bs, MEM_HEADS, MEM_HD, mem_len)

    pos_all = jnp.concatenate([jnp.tile(jnp.arange(seq), bp), jnp.tile(past + jnp.arange(n_tok), bs)])
    cos2, sin2 = _rotary_tables(pos_all)
    decays = _retention_decays()
    zero_state = jnp.zeros((bp, RET_HEADS, RET_D, RET_D), F32)

    w_router = jnp.concatenate(
        [w_group, jnp.transpose(w_route, (0, 2, 1, 3)).reshape(depth, d, N_EXPERTS),
         jnp.zeros((depth, d, LANES - N_GROUPS - N_EXPERTS), F32)], axis=2)
    b_router = jnp.concatenate(
        [b_group, b_route.reshape(depth, N_EXPERTS), jnp.zeros((depth, LANES - N_GROUPS - N_EXPERTS), F32)],
        axis=1)[:, None, :]
    w_up_e = w_up.reshape(depth * N_EXPERTS, d, 2 * D_EXPERT)
    w_down_e = w_down.reshape(depth * N_EXPERTS, D_EXPERT, d)

    sample_rows = SAMPLE_BATCH_TILE * n_tok
    ret_prompt = []
    state_all = state_ret.reshape(n_a * bs, RET_HEADS, RET_D, RET_D)
    ret_sample = jnp.zeros(state_all.shape, F32)
    kv = logf = None
    for l in range(depth):
        if l < n_a:
            h = proj_a(x, w_in_a[l].astype(BF16), cos2, sin2)
            tok_p, s_p = retention_prompt(h, zero_state, decays, bp, seq)
            hs = h[tp:, :2 * RET_W].reshape(bs // SAMPLE_BATCH_TILE, sample_rows, 2, RET_HEADS, RET_D)
            hs = jnp.transpose(hs, (2, 0, 3, 4, 1))
            tok_s, ret_sample = retention_sample(h, hs[0], hs[1], state_all, ret_sample, l, tp, bs, n_tok)
            ret_prompt.append(s_p)
            mq_block = (4 * RET_W) // MEM_W
        else:
            if l == n_a:
                w_kv_pad = jnp.concatenate(
                    [w_kv_shared, jnp.zeros((d, LANES - FOX_HEADS), F32)], axis=1).astype(BF16)
                bf_pad = jnp.concatenate([b_forget, jnp.zeros((LANES - FOX_HEADS,), F32)])[None, :]
                kv, logf = kv_shared(x, w_kv_pad, bf_pad)
                c_prompt = cumsum_rows(logf, bp, seq)
                k_aug, v_t, kt_prompt, vt_prompt = fox_prep(kv, c_prompt, bp, seq)
                cache_kt = jnp.transpose(cache_fox_k, (0, 2, 3, 1))
                cache_vt = jnp.transpose(cache_fox_v, (0, 2, 3, 1))
                cache_lf = jnp.pad(jnp.transpose(cache_fox_logf, (0, 2, 1)),
                                   ((0, 0), (0, FOX_COLS - FOX_HEADS), (0, 0)))
            h = matmul(x, w_in_b[l - n_a].astype(BF16))
            tok_p = fox_prompt(h, c_prompt, k_aug, v_t, bp, seq)
            tok_s = fox_sample(h, kv, logf, cache_kt, cache_vt, cache_lf, page_table, tp, bs, n_tok)
            mq_block = FOX_W // MEM_W
        steps_per_b = seq // ROW_TILE
        mem_p = mem_attention(h, mq_block, mem_banks_p, mem_banks_p,
                              lambda i, l=l: (i // steps_per_b) * (2 * depth) + 2 * l,
                              lambda i, l=l: (i // steps_per_b) * (2 * depth) + 2 * l + 1,
                              row_block0=0, n_steps=tp // ROW_TILE, rows=ROW_TILE, banks=1)
        bank_block = lambda i, l=l: l * (bs // SAMPLE_BATCH_TILE) + i
        mem_s = mem_attention(h, mq_block, cache_mkt, cache_mvt, bank_block, bank_block,
                              row_block0=tp // sample_rows, n_steps=bs // SAMPLE_BATCH_TILE,
                              rows=sample_rows, banks=SAMPLE_BATCH_TILE)
        wt = tok_p.shape[1]
        x, x_packed = out_proj_ln(tok_p, mem_p, tok_s, mem_s, x, w_o[l, :wt].astype(BF16),
                                  w_o[l, wt:].astype(BF16), ln1_g[l][None, :], ln1_b[l][None, :], alpha)
        x = hierarchical_moe_ln(x, x_packed, w_router[l], b_router[l], w_up_e, w_down_e, l * N_EXPERTS,
                                ln2_g[l][None, :], ln2_b[l][None, :], alpha)

    y_prompt = x[:tp].reshape(bp, seq, d)
    y_sample = x[tp:].reshape(bs, n_tok, d)
    fox_k_s = kv[tp:, :FOX_W]
    fox_v_s = kv[tp:, FOX_W:]
    fox_lf = logf[:, :FOX_HEADS]
    return (y_prompt, y_sample, jnp.stack(ret_prompt), ret_sample.reshape(state_ret.shape),
            jnp.transpose(kt_prompt.reshape(bp, FOX_HEADS, FOX_HD, seq), (0, 3, 1, 2)),
            jnp.transpose(vt_prompt.reshape(bp, FOX_HEADS, FOX_HD, seq), (0, 3, 1, 2)),
            fox_lf[:tp].reshape(bp, seq, FOX_HEADS),
            fox_k_s.reshape(bs, n_tok, FOX_HEADS, FOX_HD), fox_v_s.reshape(bs, n_tok, FOX_HEADS, FOX_HD),
            fox_lf[tp:].reshape(bs, n_tok, FOX_HEADS),
            mem_k_prompt, mem_v_prompt)
```

```python
import functools
import math

import jax
import jax.numpy as jnp
from jax import lax
from jax.experimental import pallas as pl
from jax.experimental.pallas import tpu as pltpu

F32 = jnp.float32
BF16 = jnp.bfloat16
HIGHEST = lax.Precision.HIGHEST

RET_HEADS = 6
RET_D = 128
RET_W = RET_HEADS * RET_D
RET_CHUNK = 128
ROPE_BASE = 10000.0
FOX_HEADS = 12
FOX_HD = 64
FOX_W = FOX_HEADS * FOX_HD
MEM_HEADS = 4
MEM_HD = 64
MEM_W = MEM_HEADS * MEM_HD
N_GROUPS = 4
EXPERTS_PER_GROUP = 8
N_EXPERTS = N_GROUPS * EXPERTS_PER_GROUP
D_EXPERT = 256
LN_EPS = 1e-5
HEAD_NORM_EPS = 1e-6

LANES = 128
SUBLANES = 8
VMEM_LIMIT_BYTES = 48 * 1024 * 1024

ROW_TILE = 512
MOE_TILE = 512
EXPERT_TILE = 512
FOX_BLOCK = 1024
SAMPLE_BATCH_TILE = 8


def _cparams(n_axes):
    return pltpu.CompilerParams(
        dimension_semantics=("arbitrary",) * n_axes, vmem_limit_bytes=VMEM_LIMIT_BYTES)


def _dot(a, b):
    return jnp.dot(a, b, preferred_element_type=F32)


def _dot_nt(a, b):
    return lax.dot_general(a, b, (((1,), (1,)), ((), ())), preferred_element_type=F32)


def _dot_tn(a, b):
    return lax.dot_general(a, b, (((0,), (0,)), ((), ())), preferred_element_type=F32)


def _layer_norm(y, g, b):
    mu = jnp.mean(y, axis=-1, keepdims=True)
    d = y - mu
    var = jnp.mean(d * d, axis=-1, keepdims=True)
    return d * lax.rsqrt(var + LN_EPS) * g + b


def _matmul_kernel(x_ref, w_ref, o_ref):
    o_ref[...] = _dot(x_ref[...].astype(BF16), w_ref[...])


def matmul(x, w_bf16):
    t, k = x.shape
    n = w_bf16.shape[1]
    tm = min(ROW_TILE, t)
    return pl.pallas_call(
        _matmul_kernel,
        out_shape=jax.ShapeDtypeStruct((t, n), F32),
        grid=(t // tm,),
        in_specs=[pl.BlockSpec((tm, k), lambda i: (i, 0)),
                  pl.BlockSpec((k, n), lambda i: (0, 0))],
        out_specs=pl.BlockSpec((tm, n), lambda i: (i, 0)),
        compiler_params=_cparams(1),
        name="matmul",
    )(x, w_bf16)


def _proj_a_kernel(x_ref, w_ref, cos_ref, sin_ref, o_ref, *, k_scale):
    x = x_ref[...].astype(BF16)
    cos2 = cos_ref[...]
    sin2 = sin_ref[...]
    for c in range(2 * RET_HEADS):
        cols = slice(c * RET_D, (c + 1) * RET_D)
        y = _dot(x, w_ref[:, cols])
        y = y * cos2 + pltpu.roll(y, RET_D // 2, 1) * sin2
        if c >= RET_HEADS:
            y = y * k_scale
        o_ref[:, cols] = y
    o_ref[:, 2 * RET_W:] = _dot(x, w_ref[:, 2 * RET_W:])


def proj_a(x, w_bf16, cos2, sin2):
    t, k = x.shape
    n = w_bf16.shape[1]
    tm = min(ROW_TILE, t)
    return pl.pallas_call(
        functools.partial(_proj_a_kernel, k_scale=RET_D ** -0.5),
        out_shape=jax.ShapeDtypeStruct((t, n), F32),
        grid=(t // tm,),
        in_specs=[pl.BlockSpec((tm, k), lambda i: (i, 0)),
                  pl.BlockSpec((k, n), lambda i: (0, 0)),
                  pl.BlockSpec((tm, RET_D), lambda i: (i, 0)),
                  pl.BlockSpec((tm, RET_D), lambda i: (i, 0))],
        out_specs=pl.BlockSpec((tm, n), lambda i: (i, 0)),
        compiler_params=_cparams(1),
        name="proj_a",
    )(x, w_bf16, cos2, sin2)


def _kv_kernel(x_ref, w_ref, bf_ref, kv_ref, logf_ref):
    x = x_ref[...].astype(BF16)
    kv_ref[...] = _dot(x, w_ref[:, :2 * FOX_W])
    z = _dot(x, w_ref[:, 2 * FOX_W:]) + bf_ref[...]
    logf_ref[...] = jnp.minimum(z, 0.0) - jnp.log(1.0 + jnp.exp(-jnp.abs(z)))


def kv_shared(x, w_pad_bf16, b_forget_pad):
    t, k = x.shape
    n = w_pad_bf16.shape[1]
    tm = min(ROW_TILE, t)
    return pl.pallas_call(
        _kv_kernel,
        out_shape=(jax.ShapeDtypeStruct((t, 2 * FOX_W), F32),
                   jax.ShapeDtypeStruct((t, LANES), F32)),
        grid=(t // tm,),
        in_specs=[pl.BlockSpec((tm, k), lambda i: (i, 0)),
                  pl.BlockSpec((k, n), lambda i: (0, 0)),
                  pl.BlockSpec((1, LANES), lambda i: (0, 0))],
        out_specs=(pl.BlockSpec((tm, 2 * FOX_W), lambda i: (i, 0)),
                   pl.BlockSpec((tm, LANES), lambda i: (i, 0))),
        compiler_params=_cparams(1),
        name="kv_shared",
    )(x, w_pad_bf16, b_forget_pad)


def _out_ln_kernel(tokp_ref, memp_ref, toks_ref, mems_ref, x_ref, w1_ref, w2_ref, g_ref, b_ref, o_ref,
                   *, alpha, prompt_tiles):
    def run(tok_ref, mem_ref):
        mixed = _dot(tok_ref[...].astype(BF16), w1_ref[...]) + _dot(mem_ref[...].astype(BF16), w2_ref[...])
        o_ref[...] = _layer_norm(alpha * x_ref[...] + mixed, g_ref[...], b_ref[...])

    @pl.when(pl.program_id(0) < prompt_tiles)
    def _():
        run(tokp_ref, memp_ref)

    @pl.when(pl.program_id(0) >= prompt_tiles)
    def _():
        run(toks_ref, mems_ref)


def out_proj_ln(tok_p, mem_p, tok_s, mem_s, x, w1_bf16, w2_bf16, g, b, alpha):
    t, d = x.shape
    tm = min(ROW_TILE, t)
    wt, wm = tok_p.shape[1], mem_p.shape[1]
    n_p = tok_p.shape[0] // tm
    n_s = tok_s.shape[0] // tm

    def prompt_map(i):
        return (jnp.minimum(i, n_p - 1), 0)

    def sample_map(i):
        return (jnp.clip(i - n_p, 0, n_s - 1), 0)

    return pl.pallas_call(
        functools.partial(_out_ln_kernel, alpha=alpha, prompt_tiles=n_p),
        out_shape=jax.ShapeDtypeStruct((t, d), F32),
        grid=(t // tm,),
        in_specs=[pl.BlockSpec((tm, wt), prompt_map),
                  pl.BlockSpec((tm, wm), prompt_map),
                  pl.BlockSpec((tm, wt), sample_map),
                  pl.BlockSpec((tm, wm), sample_map),
                  pl.BlockSpec((tm, d), lambda i: (i, 0)),
                  pl.BlockSpec((wt, d), lambda i: (0, 0)),
                  pl.BlockSpec((wm, d), lambda i: (0, 0)),
                  pl.BlockSpec((1, d), lambda i: (0, 0)),
                  pl.BlockSpec((1, d), lambda i: (0, 0))],
        out_specs=pl.BlockSpec((tm, d), lambda i: (i, 0)),
        compiler_params=_cparams(1),
        name="out_proj_ln",
    )(tok_p, mem_p, tok_s, mem_s, x, w1_bf16, w2_bf16, g, b)


def _head_norm_gate(r, g):
    mu = jnp.mean(r, axis=-1, keepdims=True)
    d = r - mu
    var = jnp.mean(d * d, axis=-1, keepdims=True)
    return d * lax.rsqrt(var + HEAD_NORM_EPS) * (g * jax.nn.sigmoid(g))


def _retention_prompt_kernel(q_ref, k_ref, v_ref, g_ref, s0_ref, dm_ref, qd_ref, kd_ref, cd_ref,
                             o_ref, sout_ref, s_scr, *, chunks):
    i = pl.program_id(1)

    @pl.when(i == 0)
    def _():
        s_scr[...] = s0_ref[0]

    for c in range(chunks):
        rows = slice(c * RET_CHUNK, (c + 1) * RET_CHUNK)
        for h in range(RET_HEADS):
            cols = slice(h * RET_D, (h + 1) * RET_D)
            q = q_ref[rows, cols]
            k = k_ref[rows, cols]
            v = v_ref[rows, cols].astype(BF16)
            s_prev = s_scr[h]
            scores = _dot_nt(q.astype(BF16), k.astype(BF16)) * dm_ref[h]
            intra = _dot(scores.astype(BF16), v)
            cross = _dot((q * qd_ref[h]).astype(BF16), s_prev.astype(BF16))
            kd = (k * kd_ref[h]).astype(BF16)
            s_scr[h] = cd_ref[h] * s_prev + _dot_tn(kd, v)
            o_ref[rows, cols] = _head_norm_gate(intra + cross, g_ref[rows, cols])

    @pl.when(i == pl.num_programs(1) - 1)
    def _():
        sout_ref[0] = s_scr[...]


def retention_prompt(h, s0, decays, batch, seq):
    chunks = math.gcd(4, seq // RET_CHUNK)
    rows = chunks * RET_CHUNK
    n_i = seq // rows
    dm, qd, kd, cd = decays
    const = pl.BlockSpec((RET_HEADS, RET_CHUNK, RET_D), lambda b, i: (0, 0, 0))
    state = pl.BlockSpec((1, RET_HEADS, RET_D, RET_D), lambda b, i: (b, 0, 0, 0))

    def col(j):
        return pl.BlockSpec((rows, RET_W), lambda b, i, j=j: (b * n_i + i, j))

    return pl.pallas_call(
        functools.partial(_retention_prompt_kernel, chunks=chunks),
        out_shape=(jax.ShapeDtypeStruct((batch * seq, RET_W), F32),
                   jax.ShapeDtypeStruct((batch, RET_HEADS, RET_D, RET_D), F32)),
        grid=(batch, n_i),
        in_specs=[col(0), col(1), col(2), col(3), state, const, const, const, const],
        out_specs=(pl.BlockSpec((rows, RET_W), lambda b, i: (b * n_i + i, 0)), state),
        scratch_shapes=[pltpu.VMEM((RET_HEADS, RET_D, RET_D), F32)],
        compiler_params=_cparams(2),
        name="retention_prompt",
    )(h, h, h, h, s0, dm, qd, kd, cd)


def _retention_sample_kernel(qt_ref, kt_ref, v_ref, g_ref, s0_ref, sprev_ref, o_ref, sout_ref, r_scr,
                             *, n_req, n_tok, gammas):
    del sprev_ref
    for j in range(n_req):
        for h in range(RET_HEADS):
            cols = slice(h * RET_D, (h + 1) * RET_D)
            s = s0_ref[j, h]
            for t in range(n_tok):
                r = j * n_tok + t
                s = gammas[h] * s + kt_ref[0, h, :, r:r + 1] * v_ref[r:r + 1, cols]
                r_scr[r:r + 1, cols] = jnp.sum(qt_ref[0, h, :, r:r + 1] * s, axis=0, keepdims=True)
            sout_ref[j, h] = s
    for h in range(RET_HEADS):
        cols = slice(h * RET_D, (h + 1) * RET_D)
        o_ref[:, cols] = _head_norm_gate(r_scr[:, cols], g_ref[:, cols])


def retention_sample(h, qt, kt, s_all, s_new_all, layer, row0, n_req_total, n_tok):
    n_req = SAMPLE_BATCH_TILE
    rows = n_req * n_tok
    steps = n_req_total // n_req
    blk0 = row0 // rows
    gammas = tuple(1.0 - 2.0 ** (-5.0 - hh) for hh in range(RET_HEADS))
    state = pl.BlockSpec((n_req, RET_HEADS, RET_D, RET_D), lambda i: (layer * steps + i, 0, 0, 0))
    tr = pl.BlockSpec((1, RET_HEADS, RET_D, rows), lambda i: (i, 0, 0, 0))
    return pl.pallas_call(
        functools.partial(_retention_sample_kernel, n_req=n_req, n_tok=n_tok, gammas=gammas),
        out_shape=(jax.ShapeDtypeStruct((n_req_total * n_tok, RET_W), F32),
                   jax.ShapeDtypeStruct(s_new_all.shape, F32)),
        grid=(steps,),
        in_specs=[tr, tr,
                  pl.BlockSpec((rows, RET_W), lambda i: (blk0 + i, 2)),
                  pl.BlockSpec((rows, RET_W), lambda i: (blk0 + i, 3)),
                  state,
                  pl.BlockSpec(memory_space=pl.ANY)],
        out_specs=(pl.BlockSpec((rows, RET_W), lambda i: (i, 0)), state),
        scratch_shapes=[pltpu.VMEM((rows, RET_W), F32)],
        input_output_aliases={5: 1},
        compiler_params=_cparams(1),
        name="retention_sample",
    )(qt, kt, h, h, s_all, s_new_all)


def _mem_attn_few_rows(q_ref, mk_ref, mv_ref, o_ref, *, banks, rows_per_bank):
    q = q_ref[...] * (MEM_HD ** -0.5)
    mem_len = mk_ref.shape[3]
    per_group = SUBLANES // rows_per_bank
    n_bd = MEM_HEADS * SUBLANES
    head_of_lane = lax.broadcasted_iota(jnp.int32, (SUBLANES, MEM_W), 1) // MEM_HD
    bank_of_row = lax.broadcasted_iota(jnp.int32, (SUBLANES, MEM_W), 0) // rows_per_bank
    a_row = lax.broadcasted_iota(jnp.int32, (n_bd, MEM_W), 0)
    a_col = lax.broadcasted_iota(jnp.int32, (n_bd, MEM_W), 1)
    own_head = a_col // MEM_HD == a_row // SUBLANES
    for grp in range(banks // per_group):
        q8 = q[grp * SUBLANES:(grp + 1) * SUBLANES, :]
        out8 = jnp.zeros((SUBLANES, MEM_W), F32)
        for w in range(per_group):
            j = grp * per_group + w
            q_bd = jnp.concatenate(
                [jnp.where((head_of_lane == h) & (bank_of_row == w), q8, 0.0) for h in range(MEM_HEADS)],
                axis=0).astype(BF16)
            s = _dot(q_bd, mk_ref[j].reshape(MEM_W, mem_len).astype(BF16))
            p = jnp.exp(s - jnp.max(s, axis=1, keepdims=True))
            p = p / jnp.sum(p, axis=1, keepdims=True)
            acc = _dot_nt(p.astype(BF16), mv_ref[j].reshape(MEM_W, mem_len).astype(BF16))
            res = jnp.where(own_head, acc, 0.0).reshape(MEM_HEADS, SUBLANES, MEM_W).sum(axis=0)
            out8 = jnp.where(bank_of_row == w, res, out8)
        o_ref[grp * SUBLANES:(grp + 1) * SUBLANES, :] = out8


def _mem_attn_kernel(q_ref, mk_ref, mv_ref, o_ref, *, banks, rows_per_bank):
    if rows_per_bank < SUBLANES:
        _mem_attn_few_rows(q_ref, mk_ref, mv_ref, o_ref, banks=banks, rows_per_bank=rows_per_bank)
        return
    assert banks == 1
    q = q_ref[...] * (MEM_HD ** -0.5)
    n_rows = q.shape[0]
    n_cols = max(n_rows, LANES)
    if n_cols > n_rows:
        q = jnp.concatenate([q, jnp.zeros((n_cols - n_rows, MEM_W), F32)], axis=0)
    qt = q.T.astype(BF16)
    heads = []
    for h in range(MEM_HEADS):
        mk = mk_ref[0, h].T.astype(BF16)
        s = _dot(mk, qt[h * MEM_HD:(h + 1) * MEM_HD, :])
        p = jnp.exp(s - jnp.max(s, axis=0, keepdims=True))
        p = p / jnp.sum(p, axis=0, keepdims=True)
        heads.append(_dot(mv_ref[0, h].astype(BF16), p.astype(BF16)))
    o_ref[...] = jnp.concatenate(heads, axis=0).T[:n_rows, :]


def mem_attention(h, q_col_block, mkt, mvt, k_block, v_block, *, row_block0, n_steps, rows, banks):
    mem_len = mkt.shape[3]
    bank_shape = (banks, MEM_HEADS, MEM_HD, mem_len)
    return pl.pallas_call(
        functools.partial(_mem_attn_kernel, banks=banks, rows_per_bank=rows // banks),
        out_shape=jax.ShapeDtypeStruct((n_steps * rows, MEM_W), F32),
        grid=(n_steps,),
        in_specs=[pl.BlockSpec((rows, MEM_W), lambda i: (row_block0 + i, q_col_block)),
                  pl.BlockSpec(bank_shape, lambda i: (k_block(i), 0, 0, 0)),
                  pl.BlockSpec(bank_shape, lambda i: (v_block(i), 0, 0, 0))],
        out_specs=pl.BlockSpec((rows, MEM_W), lambda i: (i, 0)),
        compiler_params=_cparams(1),
        name="mem_attention",
    )(h, mkt, mvt)


def _mem_kv_kernel(x_ref, wt_ref, o_ref):
    o_ref[0] = _dot_nt(wt_ref[...], x_ref[...].astype(BF16))


def mem_kv_transposed(mem, w_t_bf16):
    b, m, d = mem.shape
    f = w_t_bf16.shape[0]
    return pl.pallas_call(
        _mem_kv_kernel,
        out_shape=jax.ShapeDtypeStruct((b, f, m), F32),
        grid=(b,),
        in_specs=[pl.BlockSpec((m, d), lambda i: (i, 0)),
                  pl.BlockSpec((f, d), lambda i: (0, 0))],
        out_specs=pl.BlockSpec((1, f, m), lambda i: (i, 0, 0)),
        compiler_params=_cparams(1),
        name="mem_kv",
    )(mem.reshape(b * m, d), w_t_bf16)


def _cumsum_kernel(x_ref, o_ref, carry):
    @pl.when(pl.program_id(1) == 0)
    def _():
        carry[...] = jnp.zeros_like(carry)

    n = x_ref.shape[0]
    row = lax.broadcasted_iota(jnp.int32, (n, n), 0)
    col = lax.broadcasted_iota(jnp.int32, (n, n), 1)
    tri = (row >= col).astype(F32)
    c = jnp.dot(tri, x_ref[...], precision=HIGHEST, preferred_element_type=F32) + carry[...]
    o_ref[...] = c
    carry[...] = c[n - 1:n, :]


def cumsum_rows(x, batch, seq):
    tm = min(ROW_TILE, seq)
    n_i = seq // tm
    return pl.pallas_call(
        _cumsum_kernel,
        out_shape=jax.ShapeDtypeStruct((batch * seq, x.shape[1]), F32),
        grid=(batch, n_i),
        in_specs=[pl.BlockSpec((tm, x.shape[1]), lambda b, i: (b * n_i + i, 0))],
        out_specs=pl.BlockSpec((tm, x.shape[1]), lambda b, i: (b * n_i + i, 0)),
        scratch_shapes=[pltpu.VMEM((1, x.shape[1]), F32)],
        compiler_params=_cparams(2),
        name="cumsum_rows",
    )(x)


BIAS_COLS = 12
LOG2E = math.log2(math.e)


def _split3(x):
    hi = x.astype(BF16).astype(F32)
    mid = (x - hi).astype(BF16).astype(F32)
    lo = x - hi - mid
    return hi, mid, lo


def _place(lane, base, parts):
    out = jnp.zeros(lane.shape, F32)
    for idx, part in enumerate(parts):
        out = jnp.where(lane == base + idx, part, out)
    return out


def _fox_prep_kernel(k_ref, v_ref, c_ref, ka_ref, vt_ref, ktf_ref, vtf_ref):
    p = pl.program_id(0)
    rows = k_ref.shape[0]
    lane = lax.broadcasted_iota(jnp.int32, (rows, LANES), 1)
    c2 = c_ref[...]
    bias = jnp.zeros((rows, LANES), F32)
    for e in range(2):
        ck = jnp.sum(jnp.where(lane == 2 * p + e, c2, 0.0), axis=1, keepdims=True)
        base = e * (BIAS_COLS // 2)
        ones = jnp.where((lane >= base) & (lane < base + 3), 1.0, 0.0)
        bias = bias + ones + _place(lane, base + 3, [-part for part in _split3(ck * LOG2E)])
    ka_ref[0] = jnp.concatenate([k_ref[...], bias], axis=1).astype(BF16)
    v_t = v_ref[...].T
    vt_ref[0] = v_t.astype(BF16)
    vtf_ref[0] = v_t
    ktf_ref[0] = k_ref[...].T


def fox_prep(kv, c, batch, seq):
    pairs = FOX_HEADS // 2
    n_rows = batch * seq
    tm = min(2 * FOX_BLOCK, seq)
    per_b = seq // tm
    t_spec = pl.BlockSpec((1, LANES, tm), lambda p, i: (i // per_b, p, i % per_b))
    return pl.pallas_call(
        _fox_prep_kernel,
        out_shape=(jax.ShapeDtypeStruct((pairs, n_rows, 2 * LANES), BF16),
                   jax.ShapeDtypeStruct((pairs, LANES, n_rows), BF16),
                   jax.ShapeDtypeStruct((batch, FOX_W, seq), F32),
                   jax.ShapeDtypeStruct((batch, FOX_W, seq), F32)),
        grid=(pairs, n_rows // tm),
        in_specs=[pl.BlockSpec((tm, LANES), lambda p, i: (i, p)),
                  pl.BlockSpec((tm, LANES), lambda p, i: (i, pairs + p)),
                  pl.BlockSpec((tm, LANES), lambda p, i: (i, 0))],
        out_specs=(pl.BlockSpec((1, tm, 2 * LANES), lambda p, i: (p, i, 0)),
                   pl.BlockSpec((1, LANES, tm), lambda p, i: (p, 0, i)),
                   t_spec, t_spec),
        compiler_params=_cparams(2),
        name="fox_prep",
    )(kv, kv, c)


def _fox_prompt_kernel(q_ref, c_ref, ka_ref, vt_ref, o_ref, acc_a, acc_b, *, blk):
    p = pl.program_id(1)
    i = pl.program_id(2)
    lane = lax.broadcasted_iota(jnp.int32, (blk, LANES), 1)
    q2 = q_ref[...] * (FOX_HD ** -0.5 * LOG2E)
    c2 = c_ref[...] * LOG2E
    qts = []
    for e in range(2):
        head_lanes = (lane >= e * FOX_HD) & (lane < (e + 1) * FOX_HD)
        cq = jnp.sum(jnp.where(lane == 2 * p + e, c2, 0.0), axis=1, keepdims=True)
        base = e * (BIAS_COLS // 2)
        ones = jnp.where((lane >= base + 3) & (lane < base + 6), 1.0, 0.0)
        q_aug = jnp.concatenate([jnp.where(head_lanes, q2, 0.0), ones + _place(lane, base, _split3(cq))], axis=1)
        qts.append(q_aug.T.astype(BF16))
    accs = (acc_a, acc_b)
    acc_a[...] = jnp.zeros(acc_a.shape, F32)
    acc_b[...] = jnp.zeros(acc_b.shape, F32)

    def block(j, carry, masked):
        start = pl.multiple_of(j * blk, blk)
        kb = ka_ref[0, pl.ds(start, blk), :]
        vt = vt_ref[0, :, pl.ds(start, blk)]
        new = []
        for e in range(2):
            m_old, l_old = carry[2 * e], carry[2 * e + 1]
            s = _dot(kb, qts[e])
            if masked:
                key = lax.broadcasted_iota(jnp.int32, (blk, blk), 0)
                qry = lax.broadcasted_iota(jnp.int32, (blk, blk), 1)
                s = jnp.where(key <= qry, s, -jnp.inf)
            m_new = jnp.maximum(m_old, jnp.max(s, axis=0, keepdims=True))
            a = jnp.exp2(m_old - m_new)
            pe = jnp.exp2(s - m_new)
            new += [m_new, a * l_old + jnp.sum(pe, axis=0, keepdims=True)]
            acc = accs[e]
            acc[...] = a * acc[...] + _dot(vt[e * FOX_HD:(e + 1) * FOX_HD, :], pe.astype(BF16))
        return tuple(new)

    init = (jnp.full((1, blk), -jnp.inf, F32), jnp.zeros((1, blk), F32)) * 2
    carry = lax.fori_loop(0, i, lambda j, c: block(j, c, False), init)
    carry = block(i, carry, True)
    out_t = jnp.concatenate([acc_a[...] / carry[1], acc_b[...] / carry[3]], axis=0)
    o_ref[...] = out_t.T


def fox_prompt(h, c, k_aug, v_t, batch, seq):
    blk = min(FOX_BLOCK, seq)
    n_q = seq // blk
    pairs = FOX_HEADS // 2
    return pl.pallas_call(
        functools.partial(_fox_prompt_kernel, blk=blk),
        out_shape=jax.ShapeDtypeStruct((batch * seq, FOX_W), F32),
        grid=(batch, pairs, n_q),
        in_specs=[pl.BlockSpec((blk, LANES), lambda b, p, i: (b * n_q + i, p)),
                  pl.BlockSpec((blk, LANES), lambda b, p, i: (b * n_q + i, 0)),
                  pl.BlockSpec((1, seq, 2 * LANES), lambda b, p, i: (p, b, 0)),
                  pl.BlockSpec((1, LANES, seq), lambda b, p, i: (p, 0, b))],
        out_specs=pl.BlockSpec((blk, LANES), lambda b, p, i: (b * n_q + i, p)),
        scratch_shapes=[pltpu.VMEM((FOX_HD, blk), F32), pltpu.VMEM((FOX_HD, blk), F32)],
        compiler_params=_cparams(3),
        name="fox_prompt",
    )(h, c, k_aug, v_t)


FOX_COLS = 16


def _fox_sample_kernel(pt_ref, q_ref, kvn_ref, lfn_ref, *refs, n_pages, page, n_tok):
    del pt_ref
    kt_refs = refs[:n_pages]
    vt_refs = refs[n_pages:2 * n_pages]
    lf_refs = refs[2 * n_pages:3 * n_pages]
    o_ref = refs[3 * n_pages]
    n_rows = n_tok * FOX_COLS
    reqs = SUBLANES // n_tok
    which = pl.program_id(0) % reqs

    def pick(block):
        out = block[0:n_tok, :]
        for w in range(1, reqs):
            out = jnp.where(which == w, block[w * n_tok:(w + 1) * n_tok, :], out)
        return out

    q = pick(q_ref[...])[:, :FOX_W] * (FOX_HD ** -0.5)
    kvn = pick(kvn_ref[...])
    lfn = pick(lfn_ref[...])
    head_of_lane = lax.broadcasted_iota(jnp.int32, (FOX_COLS, FOX_W), 1) // FOX_HD
    head_of_row = lax.broadcasted_iota(jnp.int32, (FOX_COLS, FOX_W), 0)
    q_rows = [jnp.where(head_of_lane == head_of_row, jnp.broadcast_to(q[t:t + 1, :], (FOX_COLS, FOX_W)), 0.0)
              for t in range(n_tok)]
    q_bd = jnp.concatenate(q_rows, axis=0).astype(BF16)

    kt = jnp.concatenate([r[0].reshape(FOX_W, page) for r in kt_refs], axis=1).astype(BF16)
    s = _dot(q_bd, kt)

    t_row = lax.broadcasted_iota(jnp.int32, (page, page), 0)
    t_col = lax.broadcasted_iota(jnp.int32, (page, page), 1)
    upto = (t_row <= t_col).astype(F32)
    carry = jnp.zeros((FOX_COLS, 1), F32)
    c_pages = []
    for pg in range(n_pages):
        c_pg = jnp.dot(lf_refs[pg][0], upto, precision=HIGHEST, preferred_element_type=F32) + carry
        carry = c_pg[:, page - 1:page]
        c_pages.append(c_pg)
    c_past = jnp.concatenate(c_pages, axis=1)

    lfn_t = jnp.concatenate([lfn, jnp.zeros((SUBLANES - n_tok, LANES), F32)], axis=0).T[:FOX_COLS, :]
    c_new = []
    for t in range(n_tok):
        carry = carry + lfn_t[:, t:t + 1]
        c_new.append(carry)
    cq = jnp.concatenate(c_new, axis=0)
    s = s + (cq - jnp.concatenate([c_past] * n_tok, axis=0))

    s_new = _dot_nt(q_bd, kvn[:, :FOX_W].astype(BF16))
    ck_new = jnp.concatenate([jnp.concatenate([c] * n_tok, axis=0) for c in c_new], axis=1)
    tok_of_row = lax.broadcasted_iota(jnp.int32, (n_rows, n_tok), 0) // FOX_COLS
    key_tok = lax.broadcasted_iota(jnp.int32, (n_rows, n_tok), 1)
    s_new = jnp.where(key_tok <= tok_of_row, s_new + (cq - ck_new), -jnp.inf)

    m = jnp.maximum(jnp.max(s, axis=1, keepdims=True), jnp.max(s_new, axis=1, keepdims=True))
    p = jnp.exp(s - m)
    p_new = jnp.exp(s_new - m)
    inv = 1.0 / (jnp.sum(p, axis=1, keepdims=True) + jnp.sum(p_new, axis=1, keepdims=True))
    vt = jnp.concatenate([r[0].reshape(FOX_W, page) for r in vt_refs], axis=1).astype(BF16)
    acc = _dot_nt((p * inv).astype(BF16), vt)
    p_new = p_new * inv
    v_new = kvn[:, FOX_W:]
    for t in range(n_tok):
        acc = acc + p_new[:, t:t + 1] * v_new[t:t + 1, :]

    a_row = lax.broadcasted_iota(jnp.int32, (n_rows, FOX_W), 0)
    a_col = lax.broadcasted_iota(jnp.int32, (n_rows, FOX_W), 1)
    acc = jnp.where(a_col // FOX_HD == a_row % FOX_COLS, acc, 0.0)
    out = acc.reshape(n_tok, FOX_COLS, FOX_W).sum(axis=1)
    for w in range(reqs):
        @pl.when(which == w)
        def _(w=w):
            o_ref[w * n_tok:(w + 1) * n_tok, :] = out


def fox_sample(h, kv, logf, cache_kt, cache_vt, cache_lf, page_table, row0, n_req, n_tok):
    n_pages = page_table.shape[1]
    page = cache_kt.shape[3]
    reqs = SUBLANES // n_tok
    blk0 = row0 // SUBLANES

    def new_spec(width):
        return pl.BlockSpec((SUBLANES, width), lambda r, pt: (blk0 + r // reqs, 0))

    def kv_spec(pg):
        return pl.BlockSpec((1, FOX_HEADS, FOX_HD, page), lambda r, pt, pg=pg: (pt[r * n_pages + pg], 0, 0, 0))

    def lf_spec(pg):
        return pl.BlockSpec((1, FOX_COLS, page), lambda r, pt, pg=pg: (pt[r * n_pages + pg], 0, 0))

    grid_spec = pltpu.PrefetchScalarGridSpec(
        num_scalar_prefetch=1,
        grid=(n_req,),
        in_specs=[new_spec(h.shape[1]), new_spec(kv.shape[1]), new_spec(logf.shape[1])]
        + [kv_spec(pg) for pg in range(n_pages)] + [kv_spec(pg) for pg in range(n_pages)]
        + [lf_spec(pg) for pg in range(n_pages)],
        out_specs=pl.BlockSpec((SUBLANES, FOX_W), lambda r, pt: (r // reqs, 0)),
    )
    return pl.pallas_call(
        functools.partial(_fox_sample_kernel, n_pages=n_pages, page=page, n_tok=n_tok),
        out_shape=jax.ShapeDtypeStruct((n_req * n_tok, FOX_W), F32),
        grid_spec=grid_spec,
        compiler_params=_cparams(1),
        name="fox_sample",
    )(page_table.reshape(-1), h, kv, logf, *([cache_kt] * n_pages), *([cache_vt] * n_pages),
      *([cache_lf] * n_pages))


def _router_kernel(x_ref, wh_ref, wl_ref, b_ref, o_ref, cnt_ref, carry):
    @pl.when(pl.program_id(0) == 0)
    def _():
        carry[...] = jnp.zeros_like(carry)

    x = x_ref[...]
    x_hi = x.astype(BF16)
    x_lo = (x - x_hi.astype(F32)).astype(BF16)
    logits = (_dot(x_hi, wh_ref[...]) + (_dot(x_hi, wl_ref[...]) + _dot(x_lo, wh_ref[...]))) + b_ref[...]
    lane = lax.broadcasted_iota(jnp.int32, logits.shape, 1)
    big = jnp.int32(LANES)
    neg = -jnp.inf
    gl = jnp.where(lane < N_GROUPS, logits, neg)
    g_max = jnp.max(gl, axis=-1, keepdims=True)
    g_idx = jnp.min(jnp.where(gl == g_max, lane, big), axis=-1, keepdims=True)
    g_w = 1.0 / jnp.sum(jnp.exp(gl - g_max), axis=-1, keepdims=True)
    in_group = (lane >= N_GROUPS) & (lane < N_GROUPS + N_EXPERTS) & (((lane - N_GROUPS) >> 3) == g_idx)
    el = jnp.where(in_group, logits, neg)
    v1 = jnp.max(el, axis=-1, keepdims=True)
    i1 = jnp.min(jnp.where(el == v1, lane, big), axis=-1, keepdims=True)
    el2 = jnp.where(lane == i1, neg, el)
    v2 = jnp.max(el2, axis=-1, keepdims=True)
    i2 = jnp.min(jnp.where(el2 == v2, lane, big), axis=-1, keepdims=True)
    e2 = jnp.exp(v2 - v1)
    w1 = g_w / (1.0 + e2)
    w2 = g_w * e2 / (1.0 + e2)
    tm = logits.shape[0]
    e1 = i1 - N_GROUPS
    e2 = i2 - N_GROUPS
    oh1 = jnp.where(lane == e1, 1.0, 0.0)
    oh2 = jnp.where(lane == e2, 1.0, 0.0)
    row = lax.broadcasted_iota(jnp.int32, (tm, tm), 0)
    col = lax.broadcasted_iota(jnp.int32, (tm, tm), 1)
    before = jnp.where(col < row, 1.0, 0.0).astype(BF16)
    tot1 = jnp.sum(oh1, axis=0, keepdims=True)
    base = carry[...]
    r1 = jnp.sum(oh1 * (base + _dot(before, oh1.astype(BF16))), axis=-1, keepdims=True)
    r2 = jnp.sum(oh2 * (base + tot1 + _dot(before, oh2.astype(BF16))), axis=-1, keepdims=True)
    carry[...] = base + tot1 + jnp.sum(oh2, axis=0, keepdims=True)
    cnt_ref[...] = carry[...]
    out = jnp.zeros(logits.shape, F32)
    for idx, val in enumerate((e1.astype(F32), e2.astype(F32), w1, w2, r1, r2)):
        out = jnp.where(lane == idx, val, out)
    o_ref[...] = out


def router(x, w_pad, b_pad):
    t, d = x.shape
    tm = min(ROW_TILE, t)
    w_hi = w_pad.astype(BF16)
    w_lo = (w_pad - w_hi.astype(F32)).astype(BF16)
    return pl.pallas_call(
        _router_kernel,
        out_shape=(jax.ShapeDtypeStruct((t, LANES), F32), jax.ShapeDtypeStruct((1, LANES), F32)),
        grid=(t // tm,),
        in_specs=[pl.BlockSpec((tm, d), lambda i: (i, 0)),
                  pl.BlockSpec((d, LANES), lambda i: (0, 0)),
                  pl.BlockSpec((d, LANES), lambda i: (0, 0)),
                  pl.BlockSpec((1, LANES), lambda i: (0, 0))],
        out_specs=(pl.BlockSpec((tm, LANES), lambda i: (i, 0)), pl.BlockSpec((1, LANES), lambda i: (0, 0))),
        scratch_shapes=[pltpu.VMEM((1, LANES), F32)],
        compiler_params=_cparams(1),
        name="router",
    )(x, w_hi, w_lo, b_pad)


def _pack_rows(y):
    n = y.shape[1] // 2
    bits = lax.bitcast_convert_type(y.astype(BF16).astype(F32), jnp.uint32)
    return bits[:, :n] | (bits[:, n:] >> 16)


def _unpack_rows(w):
    return (lax.bitcast_convert_type(w & jnp.uint32(0xFFFF0000), F32),
            lax.bitcast_convert_type(w << 16, F32))


def _row_copy(src_ref, src_row, dst_ref, dst_row, sem):
    return pltpu.make_async_copy(src_ref.at[pl.ds(src_row, 1)], dst_ref.at[pl.ds(dst_row, 1)], sem)


def _drain_rows(src_ref, dst_ref, sem, n_rows):
    def drain(r, carry):
        _row_copy(src_ref, 0, dst_ref, 0, sem).wait()
        return carry

    lax.fori_loop(0, n_rows, drain, 0, unroll=8)


def _dispatch_kernel(pos_ref, x_ref, xs_in_ref, xs_ref, buf, sems, *, tm):
    del xs_in_ref
    i = pl.program_id(0)
    n = pl.num_programs(0)
    for slot in range(2):
        @pl.when(i % 2 == slot)
        def _(slot=slot):
            @pl.when(i >= 2)
            def _():
                _drain_rows(buf.at[slot], xs_ref, sems.at[slot], 2 * tm)

            buf[slot] = _pack_rows(x_ref[...])

            def issue(r, carry):
                for k in range(2):
                    _row_copy(buf.at[slot], r, xs_ref, pos_ref[0, 0, k * tm + r], sems.at[slot]).start(priority=k)
                return carry

            lax.fori_loop(0, tm, issue, 0, unroll=8)

            @pl.when(i == n - 1)
            def _():
                _drain_rows(buf.at[slot], xs_ref, sems.at[slot], 2 * tm)

                @pl.when(n >= 2)
                def _():
                    _drain_rows(buf.at[1 - slot], xs_ref, sems.at[1 - slot], 2 * tm)


def moe_dispatch(x, pos3, xs_init):
    t, d = x.shape
    tm = MOE_TILE
    return pl.pallas_call(
        functools.partial(_dispatch_kernel, tm=tm),
        out_shape=jax.ShapeDtypeStruct(xs_init.shape, jnp.uint32),
        grid=(t // tm,),
        in_specs=[pl.BlockSpec((1, 1, 2 * tm), lambda i: (i, 0, 0), memory_space=pltpu.SMEM),
                  pl.BlockSpec((tm, d), lambda i: (i, 0)),
                  pl.BlockSpec(memory_space=pl.ANY)],
        out_specs=pl.BlockSpec(memory_space=pl.ANY),
        scratch_shapes=[pltpu.VMEM((2, tm, d // 2), jnp.uint32), pltpu.SemaphoreType.DMA((2,))],
        input_output_aliases={2: 0},
        compiler_params=_cparams(1),
        name="moe_dispatch",
    )(pos3, x, xs_init)


def _expert_kernel(te_ref, na_ref, x_ref, wu_ref, wd_ref, o_ref, wu_bf, wd_bf):
    i = pl.program_id(0)
    active = i < na_ref[0]
    changed = jnp.logical_or(i == 0, te_ref[i] != te_ref[jnp.maximum(i - 1, 0)])

    @pl.when(jnp.logical_and(active, changed))
    def _():
        wu_bf[...] = wu_ref[0].astype(BF16)
        wd_bf[...] = wd_ref[0].astype(BF16)

    @pl.when(active)
    def _():
        half = wu_bf.shape[0] // 2
        xa, xb = _unpack_rows(x_ref[...])
        up = _dot(xa.astype(BF16), wu_bf[:half, :]) + _dot(xb.astype(BF16), wu_bf[half:, :])
        hid = jax.nn.silu(up[:, :D_EXPERT]) * up[:, D_EXPERT:]
        o_ref[...] = _dot(hid.astype(BF16), wd_bf[...])

    @pl.when(jnp.logical_not(active))
    def _():
        o_ref[...] = jnp.zeros_like(o_ref)


def moe_experts(xs, w_up, w_down, tile_expert, n_active):
    r = xs.shape[0]
    d = w_up.shape[1]
    tm = EXPERT_TILE
    n_tiles = r // tm

    def row_map(i, te, na):
        return (jnp.minimum(i, na[0] - 1), 0)

    grid_spec = pltpu.PrefetchScalarGridSpec(
        num_scalar_prefetch=2,
        grid=(n_tiles,),
        in_specs=[pl.BlockSpec((tm, d // 2), row_map),
                  pl.BlockSpec((1, d, 2 * D_EXPERT), lambda i, te, na: (te[i], 0, 0)),
                  pl.BlockSpec((1, D_EXPERT, d), lambda i, te, na: (te[i], 0, 0))],
        out_specs=pl.BlockSpec((tm, d), lambda i, te, na: (i, 0)),
        scratch_shapes=[pltpu.VMEM((d, 2 * D_EXPERT), BF16), pltpu.VMEM((D_EXPERT, d), BF16)],
    )
    return pl.pallas_call(
        _expert_kernel,
        out_shape=jax.ShapeDtypeStruct((r, d), F32),
        grid_spec=grid_spec,
        compiler_params=_cparams(1),
        name="moe_experts",
    )(tile_expert, n_active, xs, w_up, w_down)


def _combine_kernel(pos_ref, posn_ref, x_ref, rt_ref, g_ref, b_ref, ys_ref, o_ref, buf, sems, *, tm, alpha):
    i = pl.program_id(0)
    n = pl.num_programs(0)

    def gather(p_ref, slot):
        def issue(r, carry):
            for k in range(2):
                _row_copy(ys_ref, p_ref[0, 0, k * tm + r], buf.at[slot, k], r, sems.at[slot]).start(priority=k)
            return carry

        lax.fori_loop(0, tm, issue, 0, unroll=8)

    @pl.when(i == 0)
    def _():
        gather(pos_ref, 0)

    for slot in range(2):
        @pl.when(i % 2 == slot)
        def _(slot=slot):
            @pl.when(i + 1 < n)
            def _():
                gather(posn_ref, 1 - slot)

            _drain_rows(ys_ref, buf.at[slot, 0], sems.at[slot], 2 * tm)
            rt = rt_ref[...]
            ffn = rt[:, 2:3] * buf[slot, 0] + rt[:, 3:4] * buf[slot, 1]
            o_ref[...] = _layer_norm(alpha * x_ref[...] + ffn, g_ref[...], b_ref[...])


def moe_combine(x, routing, ys, pos3, g, b, alpha):
    t, d = x.shape
    tm = MOE_TILE
    n = t // tm
    return pl.pallas_call(
        functools.partial(_combine_kernel, tm=tm, alpha=alpha),
        out_shape=jax.ShapeDtypeStruct((t, d), F32),
        grid=(n,),
        in_specs=[pl.BlockSpec((1, 1, 2 * tm), lambda i: (i, 0, 0), memory_space=pltpu.SMEM),
                  pl.BlockSpec((1, 1, 2 * tm), lambda i: (jnp.minimum(i + 1, n - 1), 0, 0),
                               memory_space=pltpu.SMEM),
                  pl.BlockSpec((tm, d), lambda i: (i, 0)),
                  pl.BlockSpec((tm, LANES), lambda i: (i, 0)),
                  pl.BlockSpec((1, d), lambda i: (0, 0)),
                  pl.BlockSpec((1, d), lambda i: (0, 0)),
                  pl.BlockSpec(memory_space=pl.ANY)],
        out_specs=pl.BlockSpec((tm, d), lambda i: (i, 0)),
        scratch_shapes=[pltpu.VMEM((2, 2, tm, d), F32), pltpu.SemaphoreType.DMA((2,))],
        compiler_params=_cparams(1),
        name="moe_combine",
    )(pos3, pos3, x, routing, g, b, ys)


def moe_plan(routing, counts, n_tiles, expert0):
    t = routing.shape[0]
    tm = MOE_TILE
    experts = jnp.arange(N_EXPERTS, dtype=jnp.int32)
    cnt = counts[0, :N_EXPERTS].astype(jnp.int32)
    tiles_e = (cnt + EXPERT_TILE - 1) // EXPERT_TILE
    tile_end = jnp.cumsum(tiles_e)
    offs = (tile_end - tiles_e) * EXPERT_TILE
    ids = routing[:, 0:2].astype(jnp.int32)
    rank = routing[:, 4:6].astype(jnp.int32)
    pos = jnp.sum(jnp.where(ids[:, :, None] == experts, offs, 0), axis=-1) + rank
    n_active = tile_end[-1:]
    tile_ids = jnp.minimum(jnp.arange(n_tiles, dtype=jnp.int32), n_active[0] - 1)
    tile_expert = jnp.minimum(jnp.sum((tile_ids[:, None] >= tile_end[None, :]).astype(jnp.int32), axis=1),
                              N_EXPERTS - 1) + expert0
    pos3 = pos.reshape(t // tm, tm, 2).transpose(0, 2, 1).reshape(t // tm, 1, 2 * tm)
    return pos3, tile_expert, n_active


def hierarchical_moe_ln(x, w_router, b_router, w_up, w_down, expert0, g, b, alpha):
    t, d = x.shape
    n_tiles = (2 * t) // EXPERT_TILE + N_EXPERTS
    routing, counts = router(x, w_router, b_router)
    pos3, tile_expert, n_active = moe_plan(routing, counts, n_tiles, expert0)
    xs = moe_dispatch(x, pos3, jnp.zeros((n_tiles * EXPERT_TILE, d // 2), jnp.uint32))
    ys = moe_experts(xs, w_up, w_down, tile_expert, n_active)
    return moe_combine(x, routing, ys, pos3, g, b, alpha)


def _rotary_tables(pos):
    half = RET_D // 2
    inv_freq = ROPE_BASE ** (-jnp.arange(half, dtype=F32) / half)
    ang = pos.astype(F32)[:, None] * inv_freq[None, :]
    cos, sin = jnp.cos(ang), jnp.sin(ang)
    return jnp.concatenate([cos, cos], axis=1), jnp.concatenate([-sin, sin], axis=1)


def _retention_decays():
    log_gamma = jnp.log(1.0 - 2.0 ** (-5.0 - jnp.arange(RET_HEADS, dtype=F32)))
    c = RET_CHUNK
    i = jnp.arange(c, dtype=F32)
    diff = i[:, None] - i[None, :]
    dm = jnp.where(diff >= 0, jnp.exp(log_gamma[:, None, None] * jnp.maximum(diff, 0.0)[None]), 0.0)
    shape = (RET_HEADS, c, RET_D)
    qd = jnp.broadcast_to(jnp.exp(log_gamma[:, None] * (i + 1.0)[None, :])[:, :, None], shape)
    kd = jnp.broadcast_to(jnp.exp(log_gamma[:, None] * (c - 1 - i)[None, :])[:, :, None], shape)
    cd = jnp.broadcast_to(jnp.exp(log_gamma * c)[:, None, None], shape)
    return dm, qd, kd, cd


def kernel(x_prompt, x_sample, mem_prompt, state_ret, cache_fox_k, cache_fox_v, cache_fox_logf, cache_mem_k, cache_mem_v, page_table, w_in_a, w_in_b, w_kv_shared, b_forget, w_o, w_mem_kv, ln1_g, ln1_b, ln2_g, ln2_b, w_group, b_group, w_route, b_route, w_up, w_down):
    bp, seq, d = x_prompt.shape
    bs, n_tok, _ = x_sample.shape
    depth = w_o.shape[0]
    n_a = w_in_a.shape[0]
    mem_len = mem_prompt.shape[1]
    n_pages, page = page_table.shape[1], cache_fox_k.shape[1]
    past = n_pages * page
    tp, ts = bp * seq, bs * n_tok
    alpha = (2 * depth) ** 0.25
    assert w_in_a.shape[2] == 4 * RET_W + MEM_W and w_in_b.shape[2] == FOX_W + MEM_W
    assert w_kv_shared.shape[1] == 2 * FOX_W + FOX_HEADS and w_up.shape[1:3] == (N_GROUPS, EXPERTS_PER_GROUP)
    assert seq % RET_CHUNK == 0 and SUBLANES % n_tok == 0 and tp % ROW_TILE == 0 and ts % ROW_TILE == 0

    x = jnp.concatenate([x_prompt.reshape(tp, d), x_sample.reshape(ts, d)], axis=0)
    t_all = tp + ts

    w_mem_t = jnp.transpose(w_mem_kv, (0, 2, 1)).reshape(depth * 2 * MEM_W, d).astype(BF16)
    mem_kv_t = mem_kv_transposed(mem_prompt, w_mem_t)
    mem_kv6 = mem_kv_t.reshape(bp, depth, 2, MEM_HEADS, MEM_HD, mem_len)
    mem_k_prompt = jnp.transpose(mem_kv6[:, :, 0], (1, 0, 4, 2, 3))
    mem_v_prompt = jnp.transpose(mem_kv6[:, :, 1], (1, 0, 4, 2, 3))
    mem_banks_p = mem_kv_t.reshape(bp * depth * 2, MEM_HEADS, MEM_HD, mem_len)
    cache_mkt = jnp.transpose(cache_mem_k, (0, 1, 3, 4, 2)).reshape(depth * bs, MEM_HEADS, MEM_HD, mem_len)
    cache_mvt = jnp.transpose(cache_mem_v, (0, 1, 3, 4, 2)).reshape(depth * bs, MEM_HEADS, MEM_HD, mem_len)

    pos_all = jnp.concatenate([jnp.tile(jnp.arange(seq), bp), jnp.tile(past + jnp.arange(n_tok), bs)])
    cos2, sin2 = _rotary_tables(pos_all)
    decays = _retention_decays()
    zero_state = jnp.zeros((bp, RET_HEADS, RET_D, RET_D), F32)

    w_router = jnp.concatenate(
        [w_group, jnp.transpose(w_route, (0, 2, 1, 3)).reshape(depth, d, N_EXPERTS),
         jnp.zeros((depth, d, LANES - N_GROUPS - N_EXPERTS), F32)], axis=2)
    b_router = jnp.concatenate(
        [b_group, b_route.reshape(depth, N_EXPERTS), jnp.zeros((depth, LANES - N_GROUPS - N_EXPERTS), F32)],
        axis=1)[:, None, :]
    w_up_e = w_up.reshape(depth * N_EXPERTS, d, 2 * D_EXPERT)
    w_down_e = w_down.reshape(depth * N_EXPERTS, D_EXPERT, d)

    sample_rows = SAMPLE_BATCH_TILE * n_tok
    ret_prompt = []
    state_all = state_ret.reshape(n_a * bs, RET_HEADS, RET_D, RET_D)
    ret_sample = jnp.zeros(state_all.shape, F32)
    kv = logf = None
    for l in range(depth):
        if l < n_a:
            h = proj_a(x, w_in_a[l].astype(BF16), cos2, sin2)
            tok_p, s_p = retention_prompt(h, zero_state, decays, bp, seq)
            hs = h[tp:, :2 * RET_W].reshape(bs // SAMPLE_BATCH_TILE, sample_rows, 2, RET_HEADS, RET_D)
            hs = jnp.transpose(hs, (2, 0, 3, 4, 1))
            tok_s, ret_sample = retention_sample(h, hs[0], hs[1], state_all, ret_sample, l, tp, bs, n_tok)
            ret_prompt.append(s_p)
            mq_block = (4 * RET_W) // MEM_W
        else:
            if l == n_a:
                w_kv_pad = jnp.concatenate(
                    [w_kv_shared, jnp.zeros((d, LANES - FOX_HEADS), F32)], axis=1).astype(BF16)
                bf_pad = jnp.concatenate([b_forget, jnp.zeros((LANES - FOX_HEADS,), F32)])[None, :]
                kv, logf = kv_shared(x, w_kv_pad, bf_pad)
                c_prompt = cumsum_rows(logf, bp, seq)
                k_aug, v_t, kt_prompt, vt_prompt = fox_prep(kv, c_prompt, bp, seq)
                cache_kt = jnp.transpose(cache_fox_k, (0, 2, 3, 1))
                cache_vt = jnp.transpose(cache_fox_v, (0, 2, 3, 1))
                cache_lf = jnp.pad(jnp.transpose(cache_fox_logf, (0, 2, 1)),
                                   ((0, 0), (0, FOX_COLS - FOX_HEADS), (0, 0)))
            h = matmul(x, w_in_b[l - n_a].astype(BF16))
            tok_p = fox_prompt(h, c_prompt, k_aug, v_t, bp, seq)
            tok_s = fox_sample(h, kv, logf, cache_kt, cache_vt, cache_lf, page_table, tp, bs, n_tok)
            mq_block = FOX_W // MEM_W
        steps_per_b = seq // ROW_TILE
        mem_p = mem_attention(h, mq_block, mem_banks_p, mem_banks_p,
                              lambda i, l=l: (i // steps_per_b) * (2 * depth) + 2 * l,
                              lambda i, l=l: (i // steps_per_b) * (2 * depth) + 2 * l + 1,
                              row_block0=0, n_steps=tp // ROW_TILE, rows=ROW_TILE, banks=1)
        bank_block = lambda i, l=l: l * (bs // SAMPLE_BATCH_TILE) + i
        mem_s = mem_attention(h, mq_block, cache_mkt, cache_mvt, bank_block, bank_block,
                              row_block0=tp // sample_rows, n_steps=bs // SAMPLE_BATCH_TILE,
                              rows=sample_rows, banks=SAMPLE_BATCH_TILE)
        wt = tok_p.shape[1]
        x = out_proj_ln(tok_p, mem_p, tok_s, mem_s, x, w_o[l, :wt].astype(BF16), w_o[l, wt:].astype(BF16),
                        ln1_g[l][None, :], ln1_b[l][None, :], alpha)
        x = hierarchical_moe_ln(x, w_router[l], b_router[l], w_up_e, w_down_e, l * N_EXPERTS,
                                ln2_g[l][None, :], ln2_b[l][None, :], alpha)

    y_prompt = x[:tp].reshape(bp, seq, d)
    y_sample = x[tp:].reshape(bs, n_tok, d)
    fox_k_s = kv[tp:, :FOX_W]
    fox_v_s = kv[tp:, FOX_W:]
    fox_lf = logf[:, :FOX_HEADS]
    return (y_prompt, y_sample, jnp.stack(ret_prompt), ret_sample.reshape(state_ret.shape),
            jnp.transpose(kt_prompt.reshape(bp, FOX_HEADS, FOX_HD, seq), (0, 3, 1, 2)),
            jnp.transpose(vt_prompt.reshape(bp, FOX_HEADS, FOX_HD, seq), (0, 3, 1, 2)),
            fox_lf[:tp].reshape(bp, seq, FOX_HEADS),
            fox_k_s.reshape(bs, n_tok, FOX_HEADS, FOX_HD), fox_v_s.reshape(bs, n_tok, FOX_HEADS, FOX_HD),
            fox_lf[tp:].reshape(bs, n_tok, FOX_HEADS),
            mem_k_prompt, mem_v_prompt)
```

```python
import functools
import math

import jax
import jax.numpy as jnp
from jax import lax
from jax.experimental import pallas as pl
from jax.experimental.pallas import tpu as pltpu

F32 = jnp.float32
BF16 = jnp.bfloat16
HIGHEST = lax.Precision.HIGHEST

RET_HEADS = 6
RET_D = 128
RET_W = RET_HEADS * RET_D
RET_CHUNK = 128
ROPE_BASE = 10000.0
FOX_HEADS = 12
FOX_HD = 64
FOX_W = FOX_HEADS * FOX_HD
MEM_HEADS = 4
MEM_HD = 64
MEM_W = MEM_HEADS * MEM_HD
N_GROUPS = 4
EXPERTS_PER_GROUP = 8
N_EXPERTS = N_GROUPS * EXPERTS_PER_GROUP
D_EXPERT = 256
LN_EPS = 1e-5
HEAD_NORM_EPS = 1e-6

LANES = 128
SUBLANES = 8
VMEM_LIMIT_BYTES = 48 * 1024 * 1024

ROW_TILE = 512
MOE_TILE = 512
EXPERT_TILE = 512
FOX_BLOCK = 1024
SAMPLE_BATCH_TILE = 8


def _cparams(n_axes):
    return pltpu.CompilerParams(
        dimension_semantics=("arbitrary",) * n_axes, vmem_limit_bytes=VMEM_LIMIT_BYTES)


def _dot(a, b):
    return jnp.dot(a, b, preferred_element_type=F32)


def _dot_nt(a, b):
    return lax.dot_general(a, b, (((1,), (1,)), ((), ())), preferred_element_type=F32)


def _dot_tn(a, b):
    return lax.dot_general(a, b, (((0,), (0,)), ((), ())), preferred_element_type=F32)


def _layer_norm(y, g, b):
    mu = jnp.mean(y, axis=-1, keepdims=True)
    d = y - mu
    var = jnp.mean(d * d, axis=-1, keepdims=True)
    return d * lax.rsqrt(var + LN_EPS) * g + b


def _matmul_kernel(x_ref, w_ref, o_ref):
    o_ref[...] = _dot(x_ref[...].astype(BF16), w_ref[...])


def matmul(x, w_bf16):
    t, k = x.shape
    n = w_bf16.shape[1]
    tm = min(ROW_TILE, t)
    return pl.pallas_call(
        _matmul_kernel,
        out_shape=jax.ShapeDtypeStruct((t, n), F32),
        grid=(t // tm,),
        in_specs=[pl.BlockSpec((tm, k), lambda i: (i, 0)),
                  pl.BlockSpec((k, n), lambda i: (0, 0))],
        out_specs=pl.BlockSpec((tm, n), lambda i: (i, 0)),
        compiler_params=_cparams(1),
        name="matmul",
    )(x, w_bf16)


def _proj_a_kernel(x_ref, w_ref, cos_ref, sin_ref, o_ref, *, k_scale):
    x = x_ref[...].astype(BF16)
    cos2 = cos_ref[...]
    sin2 = sin_ref[...]
    for c in range(2 * RET_HEADS):
        cols = slice(c * RET_D, (c + 1) * RET_D)
        y = _dot(x, w_ref[:, cols])
        y = y * cos2 + pltpu.roll(y, RET_D // 2, 1) * sin2
        if c >= RET_HEADS:
            y = y * k_scale
        o_ref[:, cols] = y
    o_ref[:, 2 * RET_W:] = _dot(x, w_ref[:, 2 * RET_W:])


def proj_a(x, w_bf16, cos2, sin2):
    t, k = x.shape
    n = w_bf16.shape[1]
    tm = min(ROW_TILE, t)
    return pl.pallas_call(
        functools.partial(_proj_a_kernel, k_scale=RET_D ** -0.5),
        out_shape=jax.ShapeDtypeStruct((t, n), F32),
        grid=(t // tm,),
        in_specs=[pl.BlockSpec((tm, k), lambda i: (i, 0)),
                  pl.BlockSpec((k, n), lambda i: (0, 0)),
                  pl.BlockSpec((tm, RET_D), lambda i: (i, 0)),
                  pl.BlockSpec((tm, RET_D), lambda i: (i, 0))],
        out_specs=pl.BlockSpec((tm, n), lambda i: (i, 0)),
        compiler_params=_cparams(1),
        name="proj_a",
    )(x, w_bf16, cos2, sin2)


def _kv_kernel(x_ref, w_ref, bf_ref, kv_ref, logf_ref):
    x = x_ref[...].astype(BF16)
    kv_ref[...] = _dot(x, w_ref[:, :2 * FOX_W])
    z = _dot(x, w_ref[:, 2 * FOX_W:]) + bf_ref[...]
    logf_ref[...] = jnp.minimum(z, 0.0) - jnp.log(1.0 + jnp.exp(-jnp.abs(z)))


def kv_shared(x, w_pad_bf16, b_forget_pad):
    t, k = x.shape
    n = w_pad_bf16.shape[1]
    tm = min(ROW_TILE, t)
    return pl.pallas_call(
        _kv_kernel,
        out_shape=(jax.ShapeDtypeStruct((t, 2 * FOX_W), F32),
                   jax.ShapeDtypeStruct((t, LANES), F32)),
        grid=(t // tm,),
        in_specs=[pl.BlockSpec((tm, k), lambda i: (i, 0)),
                  pl.BlockSpec((k, n), lambda i: (0, 0)),
                  pl.BlockSpec((1, LANES), lambda i: (0, 0))],
        out_specs=(pl.BlockSpec((tm, 2 * FOX_W), lambda i: (i, 0)),
                   pl.BlockSpec((tm, LANES), lambda i: (i, 0))),
        compiler_params=_cparams(1),
        name="kv_shared",
    )(x, w_pad_bf16, b_forget_pad)


def _out_ln_kernel(tokp_ref, memp_ref, toks_ref, mems_ref, x_ref, w1_ref, w2_ref, g_ref, b_ref, o_ref,
                   *, alpha, prompt_tiles):
    def run(tok_ref, mem_ref):
        mixed = _dot(tok_ref[...].astype(BF16), w1_ref[...]) + _dot(mem_ref[...].astype(BF16), w2_ref[...])
        o_ref[...] = _layer_norm(alpha * x_ref[...] + mixed, g_ref[...], b_ref[...])

    @pl.when(pl.program_id(0) < prompt_tiles)
    def _():
        run(tokp_ref, memp_ref)

    @pl.when(pl.program_id(0) >= prompt_tiles)
    def _():
        run(toks_ref, mems_ref)


def out_proj_ln(tok_p, mem_p, tok_s, mem_s, x, w1_bf16, w2_bf16, g, b, alpha):
    t, d = x.shape
    tm = min(ROW_TILE, t)
    wt, wm = tok_p.shape[1], mem_p.shape[1]
    n_p = tok_p.shape[0] // tm
    n_s = tok_s.shape[0] // tm

    def prompt_map(i):
        return (jnp.minimum(i, n_p - 1), 0)

    def sample_map(i):
        return (jnp.clip(i - n_p, 0, n_s - 1), 0)

    return pl.pallas_call(
        functools.partial(_out_ln_kernel, alpha=alpha, prompt_tiles=n_p),
        out_shape=jax.ShapeDtypeStruct((t, d), F32),
        grid=(t // tm,),
        in_specs=[pl.BlockSpec((tm, wt), prompt_map),
                  pl.BlockSpec((tm, wm), prompt_map),
                  pl.BlockSpec((tm, wt), sample_map),
                  pl.BlockSpec((tm, wm), sample_map),
                  pl.BlockSpec((tm, d), lambda i: (i, 0)),
                  pl.BlockSpec((wt, d), lambda i: (0, 0)),
                  pl.BlockSpec((wm, d), lambda i: (0, 0)),
                  pl.BlockSpec((1, d), lambda i: (0, 0)),
                  pl.BlockSpec((1, d), lambda i: (0, 0))],
        out_specs=pl.BlockSpec((tm, d), lambda i: (i, 0)),
        compiler_params=_cparams(1),
        name="out_proj_ln",
    )(tok_p, mem_p, tok_s, mem_s, x, w1_bf16, w2_bf16, g, b)


def _head_norm_gate(r, g):
    mu = jnp.mean(r, axis=-1, keepdims=True)
    d = r - mu
    var = jnp.mean(d * d, axis=-1, keepdims=True)
    return d * lax.rsqrt(var + HEAD_NORM_EPS) * (g * jax.nn.sigmoid(g))


def _retention_prompt_kernel(q_ref, k_ref, v_ref, g_ref, s0_ref, dm_ref, qd_ref, kd_ref, cd_ref,
                             o_ref, sout_ref, s_scr, *, chunks):
    i = pl.program_id(1)

    @pl.when(i == 0)
    def _():
        s_scr[...] = s0_ref[0]

    for c in range(chunks):
        rows = slice(c * RET_CHUNK, (c + 1) * RET_CHUNK)
        for h in range(RET_HEADS):
            cols = slice(h * RET_D, (h + 1) * RET_D)
            q = q_ref[rows, cols]
            k = k_ref[rows, cols]
            v = v_ref[rows, cols].astype(BF16)
            s_prev = s_scr[h]
            scores = _dot_nt(q.astype(BF16), k.astype(BF16)) * dm_ref[h]
            intra = _dot(scores.astype(BF16), v)
            cross = _dot((q * qd_ref[h]).astype(BF16), s_prev.astype(BF16))
            kd = (k * kd_ref[h]).astype(BF16)
            s_scr[h] = cd_ref[h] * s_prev + _dot_tn(kd, v)
            o_ref[rows, cols] = _head_norm_gate(intra + cross, g_ref[rows, cols])

    @pl.when(i == pl.num_programs(1) - 1)
    def _():
        sout_ref[0] = s_scr[...]


def retention_prompt(h, s0, decays, batch, seq):
    chunks = math.gcd(4, seq // RET_CHUNK)
    rows = chunks * RET_CHUNK
    n_i = seq // rows
    dm, qd, kd, cd = decays
    const = pl.BlockSpec((RET_HEADS, RET_CHUNK, RET_D), lambda b, i: (0, 0, 0))
    state = pl.BlockSpec((1, RET_HEADS, RET_D, RET_D), lambda b, i: (b, 0, 0, 0))

    def col(j):
        return pl.BlockSpec((rows, RET_W), lambda b, i, j=j: (b * n_i + i, j))

    return pl.pallas_call(
        functools.partial(_retention_prompt_kernel, chunks=chunks),
        out_shape=(jax.ShapeDtypeStruct((batch * seq, RET_W), F32),
                   jax.ShapeDtypeStruct((batch, RET_HEADS, RET_D, RET_D), F32)),
        grid=(batch, n_i),
        in_specs=[col(0), col(1), col(2), col(3), state, const, const, const, const],
        out_specs=(pl.BlockSpec((rows, RET_W), lambda b, i: (b * n_i + i, 0)), state),
        scratch_shapes=[pltpu.VMEM((RET_HEADS, RET_D, RET_D), F32)],
        compiler_params=_cparams(2),
        name="retention_prompt",
    )(h, h, h, h, s0, dm, qd, kd, cd)


def _retention_sample_kernel(qt_ref, kt_ref, v_ref, g_ref, s0_ref, sprev_ref, o_ref, sout_ref, r_scr,
                             *, n_req, n_tok, gammas):
    del sprev_ref
    for j in range(n_req):
        for h in range(RET_HEADS):
            cols = slice(h * RET_D, (h + 1) * RET_D)
            s = s0_ref[j, h]
            for t in range(n_tok):
                r = j * n_tok + t
                s = gammas[h] * s + kt_ref[0, h, :, r:r + 1] * v_ref[r:r + 1, cols]
                r_scr[r:r + 1, cols] = jnp.sum(qt_ref[0, h, :, r:r + 1] * s, axis=0, keepdims=True)
            sout_ref[j, h] = s
    for h in range(RET_HEADS):
        cols = slice(h * RET_D, (h + 1) * RET_D)
        o_ref[:, cols] = _head_norm_gate(r_scr[:, cols], g_ref[:, cols])


def retention_sample(h, qt, kt, s_all, s_new_all, layer, row0, n_req_total, n_tok):
    n_req = SAMPLE_BATCH_TILE
    rows = n_req * n_tok
    steps = n_req_total // n_req
    blk0 = row0 // rows
    gammas = tuple(1.0 - 2.0 ** (-5.0 - hh) for hh in range(RET_HEADS))
    state = pl.BlockSpec((n_req, RET_HEADS, RET_D, RET_D), lambda i: (layer * steps + i, 0, 0, 0))
    tr = pl.BlockSpec((1, RET_HEADS, RET_D, rows), lambda i: (i, 0, 0, 0))
    return pl.pallas_call(
        functools.partial(_retention_sample_kernel, n_req=n_req, n_tok=n_tok, gammas=gammas),
        out_shape=(jax.ShapeDtypeStruct((n_req_total * n_tok, RET_W), F32),
                   jax.ShapeDtypeStruct(s_new_all.shape, F32)),
        grid=(steps,),
        in_specs=[tr, tr,
                  pl.BlockSpec((rows, RET_W), lambda i: (blk0 + i, 2)),
                  pl.BlockSpec((rows, RET_W), lambda i: (blk0 + i, 3)),
                  state,
                  pl.BlockSpec(memory_space=pl.ANY)],
        out_specs=(pl.BlockSpec((rows, RET_W), lambda i: (i, 0)), state),
        scratch_shapes=[pltpu.VMEM((rows, RET_W), F32)],
        input_output_aliases={5: 1},
        compiler_params=_cparams(1),
        name="retention_sample",
    )(qt, kt, h, h, s_all, s_new_all)


def _mem_attn_few_rows(q_ref, mk_ref, mv_ref, o_ref, *, banks, rows_per_bank):
    q = q_ref[...] * (MEM_HD ** -0.5)
    mem_len = mk_ref.shape[3]
    per_group = SUBLANES // rows_per_bank
    n_bd = MEM_HEADS * SUBLANES
    head_of_lane = lax.broadcasted_iota(jnp.int32, (SUBLANES, MEM_W), 1) // MEM_HD
    bank_of_row = lax.broadcasted_iota(jnp.int32, (SUBLANES, MEM_W), 0) // rows_per_bank
    a_row = lax.broadcasted_iota(jnp.int32, (n_bd, MEM_W), 0)
    a_col = lax.broadcasted_iota(jnp.int32, (n_bd, MEM_W), 1)
    own_head = a_col // MEM_HD == a_row // SUBLANES
    for grp in range(banks // per_group):
        q8 = q[grp * SUBLANES:(grp + 1) * SUBLANES, :]
        out8 = jnp.zeros((SUBLANES, MEM_W), F32)
        for w in range(per_group):
            j = grp * per_group + w
            q_bd = jnp.concatenate(
                [jnp.where((head_of_lane == h) & (bank_of_row == w), q8, 0.0) for h in range(MEM_HEADS)],
                axis=0).astype(BF16)
            s = _dot(q_bd, mk_ref[j].reshape(MEM_W, mem_len).astype(BF16))
            p = jnp.exp(s - jnp.max(s, axis=1, keepdims=True))
            p = p / jnp.sum(p, axis=1, keepdims=True)
            acc = _dot_nt(p.astype(BF16), mv_ref[j].reshape(MEM_W, mem_len).astype(BF16))
            res = jnp.where(own_head, acc, 0.0).reshape(MEM_HEADS, SUBLANES, MEM_W).sum(axis=0)
            out8 = jnp.where(bank_of_row == w, res, out8)
        o_ref[grp * SUBLANES:(grp + 1) * SUBLANES, :] = out8


def _mem_attn_kernel(q_ref, mk_ref, mv_ref, o_ref, *, banks, rows_per_bank):
    if rows_per_bank < SUBLANES:
        _mem_attn_few_rows(q_ref, mk_ref, mv_ref, o_ref, banks=banks, rows_per_bank=rows_per_bank)
        return
    assert banks == 1
    q = q_ref[...] * (MEM_HD ** -0.5)
    n_rows = q.shape[0]
    n_cols = max(n_rows, LANES)
    if n_cols > n_rows:
        q = jnp.concatenate([q, jnp.zeros((n_cols - n_rows, MEM_W), F32)], axis=0)
    qt = q.T.astype(BF16)
    heads = []
    for h in range(MEM_HEADS):
        mk = mk_ref[0, h].T.astype(BF16)
        s = _dot(mk, qt[h * MEM_HD:(h + 1) * MEM_HD, :])
        p = jnp.exp(s - jnp.max(s, axis=0, keepdims=True))
        p = p / jnp.sum(p, axis=0, keepdims=True)
        heads.append(_dot(mv_ref[0, h].astype(BF16), p.astype(BF16)))
    o_ref[...] = jnp.concatenate(heads, axis=0).T[:n_rows, :]


def mem_attention(h, q_col_block, mkt, mvt, k_block, v_block, *, row_block0, n_steps, rows, banks):
    mem_len = mkt.shape[3]
    bank_shape = (banks, MEM_HEADS, MEM_HD, mem_len)
    return pl.pallas_call(
        functools.partial(_mem_attn_kernel, banks=banks, rows_per_bank=rows // banks),
        out_shape=jax.ShapeDtypeStruct((n_steps * rows, MEM_W), F32),
        grid=(n_steps,),
        in_specs=[pl.BlockSpec((rows, MEM_W), lambda i: (row_block0 + i, q_col_block)),
                  pl.BlockSpec(bank_shape, lambda i: (k_block(i), 0, 0, 0)),
                  pl.BlockSpec(bank_shape, lambda i: (v_block(i), 0, 0, 0))],
        out_specs=pl.BlockSpec((rows, MEM_W), lambda i: (i, 0)),
        compiler_params=_cparams(1),
        name="mem_attention",
    )(h, mkt, mvt)


def _mem_kv_kernel(x_ref, wt_ref, o_ref):
    o_ref[0] = _dot_nt(wt_ref[...], x_ref[...].astype(BF16))


def mem_kv_transposed(mem, w_t_bf16):
    b, m, d = mem.shape
    f = w_t_bf16.shape[0]
    return pl.pallas_call(
        _mem_kv_kernel,
        out_shape=jax.ShapeDtypeStruct((b, f, m), F32),
        grid=(b,),
        in_specs=[pl.BlockSpec((m, d), lambda i: (i, 0)),
                  pl.BlockSpec((f, d), lambda i: (0, 0))],
        out_specs=pl.BlockSpec((1, f, m), lambda i: (i, 0, 0)),
        compiler_params=_cparams(1),
        name="mem_kv",
    )(mem.reshape(b * m, d), w_t_bf16)


def _cumsum_kernel(x_ref, o_ref, carry):
    @pl.when(pl.program_id(1) == 0)
    def _():
        carry[...] = jnp.zeros_like(carry)

    n = x_ref.shape[0]
    row = lax.broadcasted_iota(jnp.int32, (n, n), 0)
    col = lax.broadcasted_iota(jnp.int32, (n, n), 1)
    tri = (row >= col).astype(F32)
    c = jnp.dot(tri, x_ref[...], precision=HIGHEST, preferred_element_type=F32) + carry[...]
    o_ref[...] = c
    carry[...] = c[n - 1:n, :]


def cumsum_rows(x, batch, seq):
    tm = min(ROW_TILE, seq)
    n_i = seq // tm
    return pl.pallas_call(
        _cumsum_kernel,
        out_shape=jax.ShapeDtypeStruct((batch * seq, x.shape[1]), F32),
        grid=(batch, n_i),
        in_specs=[pl.BlockSpec((tm, x.shape[1]), lambda b, i: (b * n_i + i, 0))],
        out_specs=pl.BlockSpec((tm, x.shape[1]), lambda b, i: (b * n_i + i, 0)),
        scratch_shapes=[pltpu.VMEM((1, x.shape[1]), F32)],
        compiler_params=_cparams(2),
        name="cumsum_rows",
    )(x)


BIAS_COLS = 12
LOG2E = math.log2(math.e)


def _split3(x):
    hi = x.astype(BF16).astype(F32)
    mid = (x - hi).astype(BF16).astype(F32)
    lo = x - hi - mid
    return hi, mid, lo


def _place(lane, base, parts):
    out = jnp.zeros(lane.shape, F32)
    for idx, part in enumerate(parts):
        out = jnp.where(lane == base + idx, part, out)
    return out


def _fox_prep_kernel(k_ref, v_ref, c_ref, ka_ref, vt_ref, ktf_ref, vtf_ref):
    p = pl.program_id(0)
    rows = k_ref.shape[0]
    lane = lax.broadcasted_iota(jnp.int32, (rows, LANES), 1)
    c2 = c_ref[...]
    bias = jnp.zeros((rows, LANES), F32)
    for e in range(2):
        ck = jnp.sum(jnp.where(lane == 2 * p + e, c2, 0.0), axis=1, keepdims=True)
        base = e * (BIAS_COLS // 2)
        ones = jnp.where((lane >= base) & (lane < base + 3), 1.0, 0.0)
        bias = bias + ones + _place(lane, base + 3, [-part for part in _split3(ck * LOG2E)])
    ka_ref[0] = jnp.concatenate([k_ref[...], bias], axis=1).astype(BF16)
    v_t = v_ref[...].T
    vt_ref[0] = v_t.astype(BF16)
    vtf_ref[0] = v_t
    ktf_ref[0] = k_ref[...].T


def fox_prep(kv, c, batch, seq):
    pairs = FOX_HEADS // 2
    n_rows = batch * seq
    tm = min(2 * FOX_BLOCK, seq)
    per_b = seq // tm
    t_spec = pl.BlockSpec((1, LANES, tm), lambda p, i: (i // per_b, p, i % per_b))
    return pl.pallas_call(
        _fox_prep_kernel,
        out_shape=(jax.ShapeDtypeStruct((pairs, n_rows, 2 * LANES), BF16),
                   jax.ShapeDtypeStruct((pairs, LANES, n_rows), BF16),
                   jax.ShapeDtypeStruct((batch, FOX_W, seq), F32),
                   jax.ShapeDtypeStruct((batch, FOX_W, seq), F32)),
        grid=(pairs, n_rows // tm),
        in_specs=[pl.BlockSpec((tm, LANES), lambda p, i: (i, p)),
                  pl.BlockSpec((tm, LANES), lambda p, i: (i, pairs + p)),
                  pl.BlockSpec((tm, LANES), lambda p, i: (i, 0))],
        out_specs=(pl.BlockSpec((1, tm, 2 * LANES), lambda p, i: (p, i, 0)),
                   pl.BlockSpec((1, LANES, tm), lambda p, i: (p, 0, i)),
                   t_spec, t_spec),
        compiler_params=_cparams(2),
        name="fox_prep",
    )(kv, kv, c)


def _fox_prompt_kernel(q_ref, c_ref, ka_ref, vt_ref, o_ref, acc_a, acc_b, *, blk):
    p = pl.program_id(1)
    i = pl.program_id(2)
    lane = lax.broadcasted_iota(jnp.int32, (blk, LANES), 1)
    q2 = q_ref[...] * (FOX_HD ** -0.5 * LOG2E)
    c2 = c_ref[...] * LOG2E
    qts = []
    for e in range(2):
        head_lanes = (lane >= e * FOX_HD) & (lane < (e + 1) * FOX_HD)
        cq = jnp.sum(jnp.where(lane == 2 * p + e, c2, 0.0), axis=1, keepdims=True)
        base = e * (BIAS_COLS // 2)
        ones = jnp.where((lane >= base + 3) & (lane < base + 6), 1.0, 0.0)
        q_aug = jnp.concatenate([jnp.where(head_lanes, q2, 0.0), ones + _place(lane, base, _split3(cq))], axis=1)
        qts.append(q_aug.T.astype(BF16))
    accs = (acc_a, acc_b)
    acc_a[...] = jnp.zeros(acc_a.shape, F32)
    acc_b[...] = jnp.zeros(acc_b.shape, F32)

    def block(j, carry, masked):
        start = pl.multiple_of(j * blk, blk)
        kb = ka_ref[0, pl.ds(start, blk), :]
        vt = vt_ref[0, :, pl.ds(start, blk)]
        new = []
        for e in range(2):
            m_old, l_old = carry[2 * e], carry[2 * e + 1]
            s = _dot(kb, qts[e])
            if masked:
                key = lax.broadcasted_iota(jnp.int32, (blk, blk), 0)
                qry = lax.broadcasted_iota(jnp.int32, (blk, blk), 1)
                s = jnp.where(key <= qry, s, -jnp.inf)
            m_new = jnp.maximum(m_old, jnp.max(s, axis=0, keepdims=True))
            a = jnp.exp2(m_old - m_new)
            pe = jnp.exp2(s - m_new)
            new += [m_new, a * l_old + jnp.sum(pe, axis=0, keepdims=True)]
            acc = accs[e]
            acc[...] = a * acc[...] + _dot(vt[e * FOX_HD:(e + 1) * FOX_HD, :], pe.astype(BF16))
        return tuple(new)

    init = (jnp.full((1, blk), -jnp.inf, F32), jnp.zeros((1, blk), F32)) * 2
    carry = lax.fori_loop(0, i, lambda j, c: block(j, c, False), init)
    carry = block(i, carry, True)
    out_t = jnp.concatenate([acc_a[...] / carry[1], acc_b[...] / carry[3]], axis=0)
    o_ref[...] = out_t.T


def fox_prompt(h, c, k_aug, v_t, batch, seq):
    blk = min(FOX_BLOCK, seq)
    n_q = seq // blk
    pairs = FOX_HEADS // 2
    return pl.pallas_call(
        functools.partial(_fox_prompt_kernel, blk=blk),
        out_shape=jax.ShapeDtypeStruct((batch * seq, FOX_W), F32),
        grid=(batch, pairs, n_q),
        in_specs=[pl.BlockSpec((blk, LANES), lambda b, p, i: (b * n_q + i, p)),
                  pl.BlockSpec((blk, LANES), lambda b, p, i: (b * n_q + i, 0)),
                  pl.BlockSpec((1, seq, 2 * LANES), lambda b, p, i: (p, b, 0)),
                  pl.BlockSpec((1, LANES, seq), lambda b, p, i: (p, 0, b))],
        out_specs=pl.BlockSpec((blk, LANES), lambda b, p, i: (b * n_q + i, p)),
        scratch_shapes=[pltpu.VMEM((FOX_HD, blk), F32), pltpu.VMEM((FOX_HD, blk), F32)],
        compiler_params=_cparams(3),
        name="fox_prompt",
    )(h, c, k_aug, v_t)


FOX_COLS = 16


def _fox_sample_kernel(pt_ref, q_ref, kvn_ref, lfn_ref, *refs, n_pages, page, n_tok):
    del pt_ref
    kt_refs = refs[:n_pages]
    vt_refs = refs[n_pages:2 * n_pages]
    lf_refs = refs[2 * n_pages:3 * n_pages]
    o_ref = refs[3 * n_pages]
    n_rows = n_tok * FOX_COLS
    reqs = SUBLANES // n_tok
    which = pl.program_id(0) % reqs

    def pick(block):
        out = block[0:n_tok, :]
        for w in range(1, reqs):
            out = jnp.where(which == w, block[w * n_tok:(w + 1) * n_tok, :], out)
        return out

    q = pick(q_ref[...])[:, :FOX_W] * (FOX_HD ** -0.5)
    kvn = pick(kvn_ref[...])
    lfn = pick(lfn_ref[...])
    head_of_lane = lax.broadcasted_iota(jnp.int32, (FOX_COLS, FOX_W), 1) // FOX_HD
    head_of_row = lax.broadcasted_iota(jnp.int32, (FOX_COLS, FOX_W), 0)
    q_rows = [jnp.where(head_of_lane == head_of_row, jnp.broadcast_to(q[t:t + 1, :], (FOX_COLS, FOX_W)), 0.0)
              for t in range(n_tok)]
    q_bd = jnp.concatenate(q_rows, axis=0).astype(BF16)

    kt = jnp.concatenate([r[0].reshape(FOX_W, page) for r in kt_refs], axis=1).astype(BF16)
    s = _dot(q_bd, kt)

    t_row = lax.broadcasted_iota(jnp.int32, (page, page), 0)
    t_col = lax.broadcasted_iota(jnp.int32, (page, page), 1)
    upto = (t_row <= t_col).astype(F32)
    carry = jnp.zeros((FOX_COLS, 1), F32)
    c_pages = []
    for pg in range(n_pages):
        c_pg = jnp.dot(lf_refs[pg][0], upto, precision=HIGHEST, preferred_element_type=F32) + carry
        carry = c_pg[:, page - 1:page]
        c_pages.append(c_pg)
    c_past = jnp.concatenate(c_pages, axis=1)

    lfn_t = jnp.concatenate([lfn, jnp.zeros((SUBLANES - n_tok, LANES), F32)], axis=0).T[:FOX_COLS, :]
    c_new = []
    for t in range(n_tok):
        carry = carry + lfn_t[:, t:t + 1]
        c_new.append(carry)
    cq = jnp.concatenate(c_new, axis=0)
    s = s + (cq - jnp.concatenate([c_past] * n_tok, axis=0))

    s_new = _dot_nt(q_bd, kvn[:, :FOX_W].astype(BF16))
    ck_new = jnp.concatenate([jnp.concatenate([c] * n_tok, axis=0) for c in c_new], axis=1)
    tok_of_row = lax.broadcasted_iota(jnp.int32, (n_rows, n_tok), 0) // FOX_COLS
    key_tok = lax.broadcasted_iota(jnp.int32, (n_rows, n_tok), 1)
    s_new = jnp.where(key_tok <= tok_of_row, s_new + (cq - ck_new), -jnp.inf)

    m = jnp.maximum(jnp.max(s, axis=1, keepdims=True), jnp.max(s_new, axis=1, keepdims=True))
    p = jnp.exp(s - m)
    p_new = jnp.exp(s_new - m)
    inv = 1.0 / (jnp.sum(p, axis=1, keepdims=True) + jnp.sum(p_new, axis=1, keepdims=True))
    vt = jnp.concatenate([r[0].reshape(FOX_W, page) for r in vt_refs], axis=1).astype(BF16)
    acc = _dot_nt((p * inv).astype(BF16), vt)
    p_new = p_new * inv
    v_new = kvn[:, FOX_W:]
    for t in range(n_tok):
        acc = acc + p_new[:, t:t + 1] * v_new[t:t + 1, :]

    a_row = lax.broadcasted_iota(jnp.int32, (n_rows, FOX_W), 0)
    a_col = lax.broadcasted_iota(jnp.int32, (n_rows, FOX_W), 1)
    acc = jnp.where(a_col // FOX_HD == a_row % FOX_COLS, acc, 0.0)
    out = acc.reshape(n_tok, FOX_COLS, FOX_W).sum(axis=1)
    for w in range(reqs):
        @pl.when(which == w)
        def _(w=w):
            o_ref[w * n_tok:(w + 1) * n_tok, :] = out


def fox_sample(h, kv, logf, cache_kt, cache_vt, cache_lf, page_table, row0, n_req, n_tok):
    n_pages = page_table.shape[1]
    page = cache_kt.shape[3]
    reqs = SUBLANES // n_tok
    blk0 = row0 // SUBLANES

    def new_spec(width):
        return pl.BlockSpec((SUBLANES, width), lambda r, pt: (blk0 + r // reqs, 0))

    def kv_spec(pg):
        return pl.BlockSpec((1, FOX_HEADS, FOX_HD, page), lambda r, pt, pg=pg: (pt[r * n_pages + pg], 0, 0, 0))

    def lf_spec(pg):
        return pl.BlockSpec((1, FOX_COLS, page), lambda r, pt, pg=pg: (pt[r * n_pages + pg], 0, 0))

    grid_spec = pltpu.PrefetchScalarGridSpec(
        num_scalar_prefetch=1,
        grid=(n_req,),
        in_specs=[new_spec(h.shape[1]), new_spec(kv.shape[1]), new_spec(logf.shape[1])]
        + [kv_spec(pg) for pg in range(n_pages)] + [kv_spec(pg) for pg in range(n_pages)]
        + [lf_spec(pg) for pg in range(n_pages)],
        out_specs=pl.BlockSpec((SUBLANES, FOX_W), lambda r, pt: (r // reqs, 0)),
    )
    return pl.pallas_call(
        functools.partial(_fox_sample_kernel, n_pages=n_pages, page=page, n_tok=n_tok),
        out_shape=jax.ShapeDtypeStruct((n_req * n_tok, FOX_W), F32),
        grid_spec=grid_spec,
        compiler_params=_cparams(1),
        name="fox_sample",
    )(page_table.reshape(-1), h, kv, logf, *([cache_kt] * n_pages), *([cache_vt] * n_pages),
      *([cache_lf] * n_pages))


def _router_kernel(x_ref, wh_ref, wl_ref, b_ref, o_ref, cnt_ref, carry):
    @pl.when(pl.program_id(0) == 0)
    def _():
        carry[...] = jnp.zeros_like(carry)

    x = x_ref[...]
    x_hi = x.astype(BF16)
    x_lo = (x - x_hi.astype(F32)).astype(BF16)
    logits = (_dot(x_hi, wh_ref[...]) + (_dot(x_hi, wl_ref[...]) + _dot(x_lo, wh_ref[...]))) + b_ref[...]
    lane = lax.broadcasted_iota(jnp.int32, logits.shape, 1)
    big = jnp.int32(LANES)
    neg = -jnp.inf
    gl = jnp.where(lane < N_GROUPS, logits, neg)
    g_max = jnp.max(gl, axis=-1, keepdims=True)
    g_idx = jnp.min(jnp.where(gl == g_max, lane, big), axis=-1, keepdims=True)
    g_w = 1.0 / jnp.sum(jnp.exp(gl - g_max), axis=-1, keepdims=True)
    in_group = (lane >= N_GROUPS) & (lane < N_GROUPS + N_EXPERTS) & (((lane - N_GROUPS) >> 3) == g_idx)
    el = jnp.where(in_group, logits, neg)
    v1 = jnp.max(el, axis=-1, keepdims=True)
    i1 = jnp.min(jnp.where(el == v1, lane, big), axis=-1, keepdims=True)
    el2 = jnp.where(lane == i1, neg, el)
    v2 = jnp.max(el2, axis=-1, keepdims=True)
    i2 = jnp.min(jnp.where(el2 == v2, lane, big), axis=-1, keepdims=True)
    e2 = jnp.exp(v2 - v1)
    w1 = g_w / (1.0 + e2)
    w2 = g_w * e2 / (1.0 + e2)
    tm = logits.shape[0]
    e1 = i1 - N_GROUPS
    e2 = i2 - N_GROUPS
    oh1 = jnp.where(lane == e1, 1.0, 0.0)
    oh2 = jnp.where(lane == e2, 1.0, 0.0)
    row = lax.broadcasted_iota(jnp.int32, (tm, tm), 0)
    col = lax.broadcasted_iota(jnp.int32, (tm, tm), 1)
    before = jnp.where(col < row, 1.0, 0.0).astype(BF16)
    tot1 = jnp.sum(oh1, axis=0, keepdims=True)
    base = carry[...]
    r1 = jnp.sum(oh1 * (base + _dot(before, oh1.astype(BF16))), axis=-1, keepdims=True)
    r2 = jnp.sum(oh2 * (base + tot1 + _dot(before, oh2.astype(BF16))), axis=-1, keepdims=True)
    carry[...] = base + tot1 + jnp.sum(oh2, axis=0, keepdims=True)
    cnt_ref[...] = carry[...]
    out = jnp.zeros(logits.shape, F32)
    for idx, val in enumerate((e1.astype(F32), e2.astype(F32), w1, w2, r1, r2)):
        out = jnp.where(lane == idx, val, out)
    o_ref[...] = out


def router(x, w_pad, b_pad):
    t, d = x.shape
    tm = min(ROW_TILE, t)
    w_hi = w_pad.astype(BF16)
    w_lo = (w_pad - w_hi.astype(F32)).astype(BF16)
    return pl.pallas_call(
        _router_kernel,
        out_shape=(jax.ShapeDtypeStruct((t, LANES), F32), jax.ShapeDtypeStruct((1, LANES), F32)),
        grid=(t // tm,),
        in_specs=[pl.BlockSpec((tm, d), lambda i: (i, 0)),
                  pl.BlockSpec((d, LANES), lambda i: (0, 0)),
                  pl.BlockSpec((d, LANES), lambda i: (0, 0)),
                  pl.BlockSpec((1, LANES), lambda i: (0, 0))],
        out_specs=(pl.BlockSpec((tm, LANES), lambda i: (i, 0)), pl.BlockSpec((1, LANES), lambda i: (0, 0))),
        scratch_shapes=[pltpu.VMEM((1, LANES), F32)],
        compiler_params=_cparams(1),
        name="router",
    )(x, w_hi, w_lo, b_pad)


def _pack_rows(y):
    n = y.shape[1] // 2
    bits = lax.bitcast_convert_type(y.astype(BF16).astype(F32), jnp.uint32)
    return bits[:, :n] | (bits[:, n:] >> 16)


def _unpack_rows(w):
    return (lax.bitcast_convert_type(w & jnp.uint32(0xFFFF0000), F32),
            lax.bitcast_convert_type(w << 16, F32))


def _row_copy(src_ref, src_row, dst_ref, dst_row, sem):
    return pltpu.make_async_copy(src_ref.at[pl.ds(src_row, 1)], dst_ref.at[pl.ds(dst_row, 1)], sem)


def _drain_rows(src_ref, dst_ref, sem, n_rows):
    def drain(r, carry):
        _row_copy(src_ref, 0, dst_ref, 0, sem).wait()
        return carry

    lax.fori_loop(0, n_rows, drain, 0, unroll=8)


def _dispatch_kernel(pos_ref, x_ref, xs_in_ref, xs_ref, buf, sems, *, tm):
    del xs_in_ref
    i = pl.program_id(0)
    n = pl.num_programs(0)
    for slot in range(2):
        @pl.when(i % 2 == slot)
        def _(slot=slot):
            @pl.when(i >= 2)
            def _():
                _drain_rows(buf.at[slot], xs_ref, sems.at[slot], 2 * tm)

            buf[slot] = _pack_rows(x_ref[...])

            def issue(r, carry):
                for k in range(2):
                    _row_copy(buf.at[slot], r, xs_ref, pos_ref[0, 0, k * tm + r], sems.at[slot]).start(priority=k)
                return carry

            lax.fori_loop(0, tm, issue, 0, unroll=8)

            @pl.when(i == n - 1)
            def _():
                _drain_rows(buf.at[slot], xs_ref, sems.at[slot], 2 * tm)

                @pl.when(n >= 2)
                def _():
                    _drain_rows(buf.at[1 - slot], xs_ref, sems.at[1 - slot], 2 * tm)


def moe_dispatch(x, pos3, xs_init):
    t, d = x.shape
    tm = MOE_TILE
    return pl.pallas_call(
        functools.partial(_dispatch_kernel, tm=tm),
        out_shape=jax.ShapeDtypeStruct(xs_init.shape, jnp.uint32),
        grid=(t // tm,),
        in_specs=[pl.BlockSpec((1, 1, 2 * tm), lambda i: (i, 0, 0), memory_space=pltpu.SMEM),
                  pl.BlockSpec((tm, d), lambda i: (i, 0)),
                  pl.BlockSpec(memory_space=pl.ANY)],
        out_specs=pl.BlockSpec(memory_space=pl.ANY),
        scratch_shapes=[pltpu.VMEM((2, tm, d // 2), jnp.uint32), pltpu.SemaphoreType.DMA((2,))],
        input_output_aliases={2: 0},
        compiler_params=_cparams(1),
        name="moe_dispatch",
    )(pos3, x, xs_init)


def _expert_kernel(te_ref, na_ref, x_ref, wu_ref, wd_ref, o_ref, wu_bf, wd_bf):
    i = pl.program_id(0)
    active = i < na_ref[0]
    changed = jnp.logical_or(i == 0, te_ref[i] != te_ref[jnp.maximum(i - 1, 0)])

    @pl.when(jnp.logical_and(active, changed))
    def _():
        wu_bf[...] = wu_ref[0].astype(BF16)
        wd_bf[...] = wd_ref[0].astype(BF16)

    @pl.when(active)
    def _():
        half = wu_bf.shape[0] // 2
        xa, xb = _unpack_rows(x_ref[...])
        up = _dot(xa.astype(BF16), wu_bf[:half, :]) + _dot(xb.astype(BF16), wu_bf[half:, :])
        hid = jax.nn.silu(up[:, :D_EXPERT]) * up[:, D_EXPERT:]
        o_ref[...] = _dot(hid.astype(BF16), wd_bf[...])

    @pl.when(jnp.logical_not(active))
    def _():
        o_ref[...] = jnp.zeros_like(o_ref)


def moe_experts(xs, w_up, w_down, tile_expert, n_active):
    r = xs.shape[0]
    d = w_up.shape[1]
    tm = EXPERT_TILE
    n_tiles = r // tm

    def row_map(i, te, na):
        return (jnp.minimum(i, na[0] - 1), 0)

    grid_spec = pltpu.PrefetchScalarGridSpec(
        num_scalar_prefetch=2,
        grid=(n_tiles,),
        in_specs=[pl.BlockSpec((tm, d // 2), row_map),
                  pl.BlockSpec((1, d, 2 * D_EXPERT), lambda i, te, na: (te[i], 0, 0)),
                  pl.BlockSpec((1, D_EXPERT, d), lambda i, te, na: (te[i], 0, 0))],
        out_specs=pl.BlockSpec((tm, d), lambda i, te, na: (i, 0)),
        scratch_shapes=[pltpu.VMEM((d, 2 * D_EXPERT), BF16), pltpu.VMEM((D_EXPERT, d), BF16)],
    )
    return pl.pallas_call(
        _expert_kernel,
        out_shape=jax.ShapeDtypeStruct((r, d), F32),
        grid_spec=grid_spec,
        compiler_params=_cparams(1),
        name="moe_experts",
    )(tile_expert, n_active, xs, w_up, w_down)


def _combine_kernel(pos_ref, posn_ref, x_ref, rt_ref, g_ref, b_ref, ys_ref, o_ref, buf, sems, *, tm, alpha):
    i = pl.program_id(0)
    n = pl.num_programs(0)

    def gather(p_ref, slot):
        def issue(r, carry):
            for k in range(2):
                _row_copy(ys_ref, p_ref[0, 0, k * tm + r], buf.at[slot, k], r, sems.at[slot]).start(priority=k)
            return carry

        lax.fori_loop(0, tm, issue, 0, unroll=8)

    @pl.when(i == 0)
    def _():
        gather(pos_ref, 0)

    for slot in range(2):
        @pl.when(i % 2 == slot)
        def _(slot=slot):
            @pl.when(i + 1 < n)
            def _():
                gather(posn_ref, 1 - slot)

            _drain_rows(ys_ref, buf.at[slot, 0], sems.at[slot], 2 * tm)
            rt = rt_ref[...]
            ffn = rt[:, 2:3] * buf[slot, 0] + rt[:, 3:4] * buf[slot, 1]
            o_ref[...] = _layer_norm(alpha * x_ref[...] + ffn, g_ref[...], b_ref[...])


def moe_combine(x, routing, ys, pos3, g, b, alpha):
    t, d = x.shape
    tm = MOE_TILE
    n = t // tm
    return pl.pallas_call(
        functools.partial(_combine_kernel, tm=tm, alpha=alpha),
        out_shape=jax.ShapeDtypeStruct((t, d), F32),
        grid=(n,),
        in_specs=[pl.BlockSpec((1, 1, 2 * tm), lambda i: (i, 0, 0), memory_space=pltpu.SMEM),
                  pl.BlockSpec((1, 1, 2 * tm), lambda i: (jnp.minimum(i + 1, n - 1), 0, 0),
                               memory_space=pltpu.SMEM),
                  pl.BlockSpec((tm, d), lambda i: (i, 0)),
                  pl.BlockSpec((tm, LANES), lambda i: (i, 0)),
                  pl.BlockSpec((1, d), lambda i: (0, 0)),
                  pl.BlockSpec((1, d), lambda i: (0, 0)),
                  pl.BlockSpec(memory_space=pl.ANY)],
        out_specs=pl.BlockSpec((tm, d), lambda i: (i, 0)),
        scratch_shapes=[pltpu.VMEM((2, 2, tm, d), F32), pltpu.SemaphoreType.DMA((2,))],
        compiler_params=_cparams(1),
        name="moe_combine",
    )(pos3, pos3, x, routing, g, b, ys)


def moe_plan(routing, counts, n_tiles, expert0):
    t = routing.shape[0]
    tm = MOE_TILE
    experts = jnp.arange(N_EXPERTS, dtype=jnp.int32)
    cnt = counts[0, :N_EXPERTS].astype(jnp.int32)
    tiles_e = (cnt + EXPERT_TILE - 1) // EXPERT_TILE
    tile_end = jnp.cumsum(tiles_e)
    offs = (tile_end - tiles_e) * EXPERT_TILE
    ids = routing[:, 0:2].astype(jnp.int32)
    rank = routing[:, 4:6].astype(jnp.int32)
    pos = jnp.sum(jnp.where(ids[:, :, None] == experts, offs, 0), axis=-1) + rank
    n_active = tile_end[-1:]
    tile_ids = jnp.minimum(jnp.arange(n_tiles, dtype=jnp.int32), n_active[0] - 1)
    tile_expert = jnp.minimum(jnp.sum((tile_ids[:, None] >= tile_end[None, :]).astype(jnp.int32), axis=1),
                              N_EXPERTS - 1) + expert0
    pos3 = pos.reshape(t // tm, tm, 2).transpose(0, 2, 1).reshape(t // tm, 1, 2 * tm)
    return pos3, tile_expert, n_active


def moe_row_buffer(t, d):
    n_tiles = (2 * t) // EXPERT_TILE + N_EXPERTS
    return jnp.zeros((n_tiles * EXPERT_TILE, d // 2), jnp.uint32)


def hierarchical_moe_ln(x, xs_buf, w_router, b_router, w_up, w_down, expert0, g, b, alpha):
    t, d = x.shape
    n_tiles = xs_buf.shape[0] // EXPERT_TILE
    routing, counts = router(x, w_router, b_router)
    pos3, tile_expert, n_active = moe_plan(routing, counts, n_tiles, expert0)
    xs = moe_dispatch(x, pos3, xs_buf)
    ys = moe_experts(xs, w_up, w_down, tile_expert, n_active)
    return moe_combine(x, routing, ys, pos3, g, b, alpha), xs


def _rotary_tables(pos):
    half = RET_D // 2
    inv_freq = ROPE_BASE ** (-jnp.arange(half, dtype=F32) / half)
    ang = pos.astype(F32)[:, None] * inv_freq[None, :]
    cos, sin = jnp.cos(ang), jnp.sin(ang)
    return jnp.concatenate([cos, cos], axis=1), jnp.concatenate([-sin, sin], axis=1)


def _retention_decays():
    log_gamma = jnp.log(1.0 - 2.0 ** (-5.0 - jnp.arange(RET_HEADS, dtype=F32)))
    c = RET_CHUNK
    i = jnp.arange(c, dtype=F32)
    diff = i[:, None] - i[None, :]
    dm = jnp.where(diff >= 0, jnp.exp(log_gamma[:, None, None] * jnp.maximum(diff, 0.0)[None]), 0.0)
    shape = (RET_HEADS, c, RET_D)
    qd = jnp.broadcast_to(jnp.exp(log_gamma[:, None] * (i + 1.0)[None, :])[:, :, None], shape)
    kd = jnp.broadcast_to(jnp.exp(log_gamma[:, None] * (c - 1 - i)[None, :])[:, :, None], shape)
    cd = jnp.broadcast_to(jnp.exp(log_gamma * c)[:, None, None], shape)
    return dm, qd, kd, cd


def kernel(x_prompt, x_sample, mem_prompt, state_ret, cache_fox_k, cache_fox_v, cache_fox_logf, cache_mem_k, cache_mem_v, page_table, w_in_a, w_in_b, w_kv_shared, b_forget, w_o, w_mem_kv, ln1_g, ln1_b, ln2_g, ln2_b, w_group, b_group, w_route, b_route, w_up, w_down):
    bp, seq, d = x_prompt.shape
    bs, n_tok, _ = x_sample.shape
    depth = w_o.shape[0]
    n_a = w_in_a.shape[0]
    mem_len = mem_prompt.shape[1]
    n_pages, page = page_table.shape[1], cache_fox_k.shape[1]
    past = n_pages * page
    tp, ts = bp * seq, bs * n_tok
    alpha = (2 * depth) ** 0.25
    assert w_in_a.shape[2] == 4 * RET_W + MEM_W and w_in_b.shape[2] == FOX_W + MEM_W
    assert w_kv_shared.shape[1] == 2 * FOX_W + FOX_HEADS and w_up.shape[1:3] == (N_GROUPS, EXPERTS_PER_GROUP)
    assert seq % RET_CHUNK == 0 and SUBLANES % n_tok == 0 and tp % ROW_TILE == 0 and ts % ROW_TILE == 0

    x = jnp.concatenate([x_prompt.reshape(tp, d), x_sample.reshape(ts, d)], axis=0)
    t_all = tp + ts

    w_mem_t = jnp.transpose(w_mem_kv, (0, 2, 1)).reshape(depth * 2 * MEM_W, d).astype(BF16)
    mem_kv_t = mem_kv_transposed(mem_prompt, w_mem_t)
    mem_kv6 = mem_kv_t.reshape(bp, depth, 2, MEM_HEADS, MEM_HD, mem_len)
    mem_k_prompt = jnp.transpose(mem_kv6[:, :, 0], (1, 0, 4, 2, 3))
    mem_v_prompt = jnp.transpose(mem_kv6[:, :, 1], (1, 0, 4, 2, 3))
    mem_banks_p = mem_kv_t.reshape(bp * depth * 2, MEM_HEADS, MEM_HD, mem_len)
    cache_mkt = jnp.transpose(cache_mem_k, (0, 1, 3, 4, 2)).reshape(depth * bs, MEM_HEADS, MEM_HD, mem_len)
    cache_mvt = jnp.transpose(cache_mem_v, (0, 1, 3, 4, 2)).reshape(depth * bs, MEM_HEADS, MEM_HD, mem_len)

    pos_all = jnp.concatenate([jnp.tile(jnp.arange(seq), bp), jnp.tile(past + jnp.arange(n_tok), bs)])
    cos2, sin2 = _rotary_tables(pos_all)
    decays = _retention_decays()
    zero_state = jnp.zeros((bp, RET_HEADS, RET_D, RET_D), F32)

    w_router = jnp.concatenate(
        [w_group, jnp.transpose(w_route, (0, 2, 1, 3)).reshape(depth, d, N_EXPERTS),
         jnp.zeros((depth, d, LANES - N_GROUPS - N_EXPERTS), F32)], axis=2)
    b_router = jnp.concatenate(
        [b_group, b_route.reshape(depth, N_EXPERTS), jnp.zeros((depth, LANES - N_GROUPS - N_EXPERTS), F32)],
        axis=1)[:, None, :]
    w_up_e = w_up.reshape(depth * N_EXPERTS, d, 2 * D_EXPERT)
    w_down_e = w_down.reshape(depth * N_EXPERTS, D_EXPERT, d)

    sample_rows = SAMPLE_BATCH_TILE * n_tok
    ret_prompt = []
    state_all = state_ret.reshape(n_a * bs, RET_HEADS, RET_D, RET_D)
    ret_sample = jnp.zeros(state_all.shape, F32)
    xs_buf = moe_row_buffer(t_all, d)
    kv = logf = None
    for l in range(depth):
        if l < n_a:
            h = proj_a(x, w_in_a[l].astype(BF16), cos2, sin2)
            tok_p, s_p = retention_prompt(h, zero_state, decays, bp, seq)
            hs = h[tp:, :2 * RET_W].reshape(bs // SAMPLE_BATCH_TILE, sample_rows, 2, RET_HEADS, RET_D)
            hs = jnp.transpose(hs, (2, 0, 3, 4, 1))
            tok_s, ret_sample = retention_sample(h, hs[0], hs[1], state_all, ret_sample, l, tp, bs, n_tok)
            ret_prompt.append(s_p)
            mq_block = (4 * RET_W) // MEM_W
        else:
            if l == n_a:
                w_kv_pad = jnp.concatenate(
                    [w_kv_shared, jnp.zeros((d, LANES - FOX_HEADS), F32)], axis=1).astype(BF16)
                bf_pad = jnp.concatenate([b_forget, jnp.zeros((LANES - FOX_HEADS,), F32)])[None, :]
                kv, logf = kv_shared(x, w_kv_pad, bf_pad)
                c_prompt = cumsum_rows(logf, bp, seq)
                k_aug, v_t, kt_prompt, vt_prompt = fox_prep(kv, c_prompt, bp, seq)
                cache_kt = jnp.transpose(cache_fox_k, (0, 2, 3, 1))
                cache_vt = jnp.transpose(cache_fox_v, (0, 2, 3, 1))
                cache_lf = jnp.pad(jnp.transpose(cache_fox_logf, (0, 2, 1)),
                                   ((0, 0), (0, FOX_COLS - FOX_HEADS), (0, 0)))
            h = matmul(x, w_in_b[l - n_a].astype(BF16))
            tok_p = fox_prompt(h, c_prompt, k_aug, v_t, bp, seq)
            tok_s = fox_sample(h, kv, logf, cache_kt, cache_vt, cache_lf, page_table, tp, bs, n_tok)
            mq_block = FOX_W // MEM_W
        steps_per_b = seq // ROW_TILE
        mem_p = mem_attention(h, mq_block, mem_banks_p, mem_banks_p,
                              lambda i, l=l: (i // steps_per_b) * (2 * depth) + 2 * l,
                              lambda i, l=l: (i // steps_per_b) * (2 * depth) + 2 * l + 1,
                              row_block0=0, n_steps=tp // ROW_TILE, rows=ROW_TILE, banks=1)
        bank_block = lambda i, l=l: l * (bs // SAMPLE_BATCH_TILE) + i
        mem_s = mem_attention(h, mq_block, cache_mkt, cache_mvt, bank_block, bank_block,
                              row_block0=tp // sample_rows, n_steps=bs // SAMPLE_BATCH_TILE,
                              rows=sample_rows, banks=SAMPLE_BATCH_TILE)
        wt = tok_p.shape[1]
        x = out_proj_ln(tok_p, mem_p, tok_s, mem_s, x, w_o[l, :wt].astype(BF16), w_o[l, wt:].astype(BF16),
                        ln1_g[l][None, :], ln1_b[l][None, :], alpha)
        x, xs_buf = hierarchical_moe_ln(x, xs_buf, w_router[l], b_router[l], w_up_e, w_down_e, l * N_EXPERTS,
                                        ln2_g[l][None, :], ln2_b[l][None, :], alpha)

    y_prompt = x[:tp].reshape(bp, seq, d)
    y_sample = x[tp:].reshape(bs, n_tok, d)
    fox_k_s = kv[tp:, :FOX_W]
    fox_v_s = kv[tp:, FOX_W:]
    fox_lf = logf[:, :FOX_HEADS]
    return (y_prompt, y_sample, jnp.stack(ret_prompt), ret_sample.reshape(state_ret.shape),
            jnp.transpose(kt_prompt.reshape(bp, FOX_HEADS, FOX_HD, seq), (0, 3, 1, 2)),
            jnp.transpose(vt_prompt.reshape(bp, FOX_HEADS, FOX_HD, seq), (0, 3, 1, 2)),
            fox_lf[:tp].reshape(bp, seq, FOX_HEADS),
            fox_k_s.reshape(bs, n_tok, FOX_HEADS, FOX_HD), fox_v_s.reshape(bs, n_tok, FOX_HEADS, FOX_HD),
            fox_lf[tp:].reshape(bs, n_tok, FOX_HEADS),
            mem_k_prompt, mem_v_prompt)
```
